```python
import math
import jax, jax.numpy as jnp
from jax import lax
import numpy as np

D_MODEL = 2048
BATCH = 8
SEQ = 2048
DEPTH = 4

S5_WIDTH = D_MODEL
S5_GROUP = 16
S5_GROUPS = S5_WIDTH // S5_GROUP
S5_STATE = 64
S5_DT_MIN = 1e-3
S5_DT_MAX = 1e-1
ATT_HEADS = 16
ATT_HEAD_DIM = D_MODEL // ATT_HEADS
Q_LORA = 512
KV_LORA = 512
IDX_HEADS = 16
IDX_DIM = 128
IDX_TOPK = 256
Q_BLOCK = 128
DSA_IN_WIDTH = Q_LORA + KV_LORA + IDX_DIM + IDX_HEADS
REL_BUCKETS = 32
REL_MAX_DIST = 128
MOE_GROUPS = 4
MOE_PER_GROUP = 8
MOE_EXPERTS = MOE_GROUPS * MOE_PER_GROUP
MOE_TOPK = 2
MOE_FF = 256
DN_ALPHA = (2 * DEPTH) ** 0.25
DN_BETA = (8 * DEPTH) ** -0.25
N_S5_LAYERS = (DEPTH + 1) // 2
N_DSA_LAYERS = DEPTH // 2
LN_EPS = 1e-5
RMS_EPS = 1e-6

kernel_name = "hybrid_s5_dsa_hmoe_deepnorm"


def _layer_norm(x, g, b):
    xf = x.astype(jnp.float32)
    xc = xf - jnp.mean(xf, -1, keepdims=True)
    var = jnp.mean(xc * xc, -1, keepdims=True)
    return (xc * lax.rsqrt(var + LN_EPS) * g.astype(jnp.float32) + b.astype(jnp.float32)).astype(x.dtype)


def _rms_norm(x, g):
    xf = x.astype(jnp.float32)
    return (xf * lax.rsqrt(jnp.mean(xf * xf, -1, keepdims=True) + RMS_EPS) * g.astype(jnp.float32)).astype(x.dtype)


def _rel_bucket(rel):
    n = jnp.maximum(rel, 0)
    max_exact = REL_BUCKETS // 2
    nf = jnp.maximum(n, 1).astype(jnp.float32)
    large = max_exact + (jnp.log(nf / max_exact) / math.log(REL_MAX_DIST / max_exact)
                         * (REL_BUCKETS - max_exact)).astype(jnp.int32)
    large = jnp.minimum(large, REL_BUCKETS - 1)
    return jnp.where(n < max_exact, n, large)


def _s5_mixer(x, w_in, a_re, a_im, log_dt, b_re, b_im, c_re, c_im, d, w_glu, w_out):
    bsz, L, _ = x.shape
    f32 = jnp.float32
    u = (x @ w_in).astype(f32).reshape(bsz, L, S5_GROUPS, S5_GROUP)
    lam = lax.complex(jnp.minimum(a_re.astype(f32), -1e-4), a_im.astype(f32))
    dt = jnp.exp(log_dt.astype(f32))[:, None]
    a_bar = jnp.exp(lam * dt)
    b_bar = ((a_bar - 1.0) / lam)[..., None] * lax.complex(b_re.astype(f32), b_im.astype(f32))
    bu = jnp.einsum('blgp,gnp->blgn', u.astype(jnp.complex64), b_bar)
    a_seq = jnp.broadcast_to(a_bar, (1, L) + a_bar.shape)

    def combine(e1, e2):
        a1, b1 = e1
        a2, b2 = e2
        return a2 * a1, a2 * b1 + b2

    _, h = lax.associative_scan(combine, (a_seq, bu), axis=1)
    c = lax.complex(c_re.astype(f32), c_im.astype(f32))
    y = jnp.einsum('blgn,gpn->blgp', h, c).real + d.astype(f32) * u
    z = jax.nn.gelu(y.reshape(bsz, L, S5_WIDTH)).astype(x.dtype)
    return (z * jax.nn.sigmoid(z @ w_glu)) @ w_out


def _dsa_mixer(x, rel_bias, w_in, q_norm, kv_norm, w_uq, w_qidx, w_uk, w_uv, w_out):
    bsz, L, _ = x.shape
    proj = x @ w_in
    c_q, c_kv, k_idx, w_idx = jnp.split(proj, [Q_LORA, Q_LORA + KV_LORA, Q_LORA + KV_LORA + IDX_DIM], axis=-1)
    c_q = _rms_norm(c_q, q_norm)
    c_kv = _rms_norm(c_kv, kv_norm)
    q = (c_q @ w_uq).reshape(bsz, L, ATT_HEADS, ATT_HEAD_DIM)
    q_idx = (c_q @ w_qidx).reshape(bsz, L, IDX_HEADS, IDX_DIM)
    w_idx = w_idx * (IDX_HEADS ** -0.5)
    n_sel = min(IDX_TOPK, L // 4)
    n_blk = L // Q_BLOCK
    att_scale = ATT_HEAD_DIM ** -0.5
    idx_scale = IDX_DIM ** -0.5
    key_pos = jnp.arange(L, dtype=jnp.int32)
    gather_rows = jax.vmap(lambda rows, ids: rows[ids])

    def block(jb):
        t0 = jb * Q_BLOCK
        q_pos = t0 + jnp.arange(Q_BLOCK, dtype=jnp.int32)
        qb = lax.dynamic_slice_in_dim(q, t0, Q_BLOCK, axis=1)
        qib = lax.dynamic_slice_in_dim(q_idx, t0, Q_BLOCK, axis=1)
        wib = lax.dynamic_slice_in_dim(w_idx, t0, Q_BLOCK, axis=1)
        rel_scores = jax.nn.relu(jnp.einsum('bqhd,bsd->bqhs', qib, k_idx) * idx_scale)
        index = jnp.einsum('bqhs,bqh->bqs', rel_scores, wib).astype(jnp.float32)
        causal = key_pos[None, :] <= q_pos[:, None]
        index = jnp.where(causal[None], index, -jnp.inf)
        _, sel = lax.top_k(index, n_sel)
        valid = sel <= q_pos[None, :, None]
        kv_sel = gather_rows(c_kv, sel)
        q_lat = jnp.einsum('bqhd,hdc->bqhc', qb, w_uk)
        logits = jnp.einsum('bqhc,bqkc->bhqk', q_lat, kv_sel).astype(jnp.float32) * att_scale
        bias = rel_bias[_rel_bucket(q_pos[None, :, None] - sel)]
        logits = logits + jnp.moveaxis(bias, -1, 1).astype(jnp.float32)
        logits = jnp.where(valid[:, None], logits, -jnp.inf)
        p = jax.nn.softmax(logits, axis=-1).astype(x.dtype)
        o_lat = jnp.einsum('bhqk,bqkc->bqhc', p, kv_sel)
        o = jnp.einsum('bqhc,hcd->bqhd', o_lat, w_uv)
        return o.reshape(bsz, Q_BLOCK, ATT_HEADS * ATT_HEAD_DIM)

    out = lax.map(block, jnp.arange(n_blk, dtype=jnp.int32))
    out = jnp.moveaxis(out, 0, 1).reshape(bsz, L, ATT_HEADS * ATT_HEAD_DIM)
    return out @ w_out


def _hier_moe(x, w_group, b_group, w_expert, b_expert, w_gate, w_up, w_down):
    bsz, L, dm = x.shape
    t = x.reshape(bsz * L, dm)
    g_prob = jax.nn.softmax((t @ w_group).astype(jnp.float32) + b_group.astype(jnp.float32), axis=-1)
    g_p, g_idx = lax.top_k(g_prob, 1)
    g_onehot = jax.nn.one_hot(g_idx[:, 0], MOE_GROUPS, dtype=jnp.float32)
    e_logits = ((t @ w_expert).astype(jnp.float32) + b_expert.astype(jnp.float32)).reshape(-1, MOE_GROUPS, MOE_PER_GROUP)
    e_logits = jnp.einsum('tge,tg->te', e_logits, g_onehot)
    e_p, e_idx = lax.top_k(jax.nn.softmax(e_logits, axis=-1), MOE_TOPK)
    e_p = e_p / jnp.sum(e_p, -1, keepdims=True)
    within = jnp.einsum('tk,tke->te', e_p, jax.nn.one_hot(e_idx, MOE_PER_GROUP, dtype=jnp.float32))
    gate = ((g_onehot * g_p)[:, :, None] * within[:, None, :]).reshape(-1, MOE_EXPERTS).astype(x.dtype)
    h = jax.nn.silu(jnp.einsum('td,edf->tef', t, w_gate)) * jnp.einsum('td,edf->tef', t, w_up)
    y = jnp.einsum('tef,efd->td', h * gate[:, :, None], w_down)
    return y.reshape(bsz, L, dm)


def setup_inputs(seed: int = 0) -> dict:
    key = jax.random.key(seed)
    ks = iter(jax.random.split(key, 40))
    f32 = jnp.float32

    def nrm(shape, scale):
        return jax.random.normal(next(ks), shape, f32) * scale

    NS, ND = N_S5_LAYERS, N_DSA_LAYERS
    HD = ATT_HEADS * ATT_HEAD_DIM
    n_idx = jnp.arange(S5_STATE, dtype=f32)
    x = nrm((BATCH, SEQ, D_MODEL), 1.0)
    rel_bias = nrm((REL_BUCKETS, ATT_HEADS), 0.5)
    s5_w_in = nrm((NS, D_MODEL, S5_WIDTH), D_MODEL ** -0.5)
    s5_a_re = -0.5 + nrm((NS, S5_GROUPS, S5_STATE), 0.01)
    s5_a_im = math.pi * n_idx + nrm((NS, S5_GROUPS, S5_STATE), 0.01)
    s5_log_dt = jax.random.uniform(next(ks), (NS, S5_GROUPS), f32, math.log(S5_DT_MIN), math.log(S5_DT_MAX))
    s5_b_re = nrm((NS, S5_GROUPS, S5_STATE, S5_GROUP), (2 * S5_GROUP) ** -0.5)
    s5_b_im = nrm((NS, S5_GROUPS, S5_STATE, S5_GROUP), (2 * S5_GROUP) ** -0.5)
    s5_c_re = nrm((NS, S5_GROUPS, S5_GROUP, S5_STATE), S5_STATE ** -0.5)
    s5_c_im = nrm((NS, S5_GROUPS, S5_GROUP, S5_STATE), S5_STATE ** -0.5)
    s5_d = nrm((NS, S5_GROUPS, S5_GROUP), 1.0)
    s5_w_glu = nrm((NS, S5_WIDTH, S5_WIDTH), S5_WIDTH ** -0.5)
    s5_w_out = nrm((NS, S5_WIDTH, D_MODEL), S5_WIDTH ** -0.5 * DN_BETA)
    dsa_w_in = nrm((ND, D_MODEL, DSA_IN_WIDTH), D_MODEL ** -0.5)
    dsa_q_norm = 1.0 + nrm((ND, Q_LORA), 0.02)
    dsa_kv_norm = 1.0 + nrm((ND, KV_LORA), 0.02)
    dsa_w_uq = nrm((ND, Q_LORA, HD), Q_LORA ** -0.5)
    dsa_w_qidx = nrm((ND, Q_LORA, IDX_HEADS * IDX_DIM), Q_LORA ** -0.5)
    dsa_w_uk = nrm((ND, ATT_HEADS, ATT_HEAD_DIM, KV_LORA), KV_LORA ** -0.5)
    dsa_w_uv = nrm((ND, ATT_HEADS, KV_LORA, ATT_HEAD_DIM), KV_LORA ** -0.5)
    dsa_w_out = nrm((ND, HD, D_MODEL), HD ** -0.5 * DN_BETA)
    moe_w_group = nrm((DEPTH, D_MODEL, MOE_GROUPS), D_MODEL ** -0.5)
    moe_b_group = nrm((DEPTH, MOE_GROUPS), 0.01)
    moe_w_expert = nrm((DEPTH, D_MODEL, MOE_EXPERTS), D_MODEL ** -0.5)
    moe_b_expert = nrm((DEPTH, MOE_EXPERTS), 0.01)
    moe_w_gate = nrm((DEPTH, MOE_EXPERTS, D_MODEL, MOE_FF), D_MODEL ** -0.5)
    moe_w_up = nrm((DEPTH, MOE_EXPERTS, D_MODEL, MOE_FF), D_MODEL ** -0.5)
    moe_w_down = nrm((DEPTH, MOE_EXPERTS, MOE_FF, D_MODEL), MOE_FF ** -0.5 * DN_BETA)
    ln_mix_g = 1.0 + nrm((DEPTH, D_MODEL), 0.02)
    ln_mix_b = nrm((DEPTH, D_MODEL), 0.02)
    ln_ffn_g = 1.0 + nrm((DEPTH, D_MODEL), 0.02)
    ln_ffn_b = nrm((DEPTH, D_MODEL), 0.02)
    return {
        "x": x, "rel_bias": rel_bias,
        "s5_w_in": s5_w_in, "s5_a_re": s5_a_re, "s5_a_im": s5_a_im, "s5_log_dt": s5_log_dt,
        "s5_b_re": s5_b_re, "s5_b_im": s5_b_im, "s5_c_re": s5_c_re, "s5_c_im": s5_c_im,
        "s5_d": s5_d, "s5_w_glu": s5_w_glu, "s5_w_out": s5_w_out,
        "dsa_w_in": dsa_w_in, "dsa_q_norm": dsa_q_norm, "dsa_kv_norm": dsa_kv_norm,
        "dsa_w_uq": dsa_w_uq, "dsa_w_qidx": dsa_w_qidx, "dsa_w_uk": dsa_w_uk,
        "dsa_w_uv": dsa_w_uv, "dsa_w_out": dsa_w_out,
        "moe_w_group": moe_w_group, "moe_b_group": moe_b_group,
        "moe_w_expert": moe_w_expert, "moe_b_expert": moe_b_expert,
        "moe_w_gate": moe_w_gate, "moe_w_up": moe_w_up, "moe_w_down": moe_w_down,
        "ln_mix_g": ln_mix_g, "ln_mix_b": ln_mix_b, "ln_ffn_g": ln_ffn_g, "ln_ffn_b": ln_ffn_b,
    }


def reference(x, rel_bias, s5_w_in, s5_a_re, s5_a_im, s5_log_dt, s5_b_re, s5_b_im, s5_c_re, s5_c_im,
              s5_d, s5_w_glu, s5_w_out, dsa_w_in, dsa_q_norm, dsa_kv_norm, dsa_w_uq, dsa_w_qidx,
              dsa_w_uk, dsa_w_uv, dsa_w_out, moe_w_group, moe_b_group, moe_w_expert, moe_b_expert,
              moe_w_gate, moe_w_up, moe_w_down, ln_mix_g, ln_mix_b, ln_ffn_g, ln_ffn_b):
    h = x
    for i in range(DEPTH):
        j = i // 2
        if i % 2 == 0:
            mix = _s5_mixer(h, s5_w_in[j], s5_a_re[j], s5_a_im[j], s5_log_dt[j], s5_b_re[j], s5_b_im[j],
                            s5_c_re[j], s5_c_im[j], s5_d[j], s5_w_glu[j], s5_w_out[j])
        else:
            mix = _dsa_mixer(h, rel_bias, dsa_w_in[j], dsa_q_norm[j], dsa_kv_norm[j], dsa_w_uq[j],
                             dsa_w_qidx[j], dsa_w_uk[j], dsa_w_uv[j], dsa_w_out[j])
        h = _layer_norm(DN_ALPHA * h + mix, ln_mix_g[i], ln_mix_b[i])
        ffn = _hier_moe(h, moe_w_group[i], moe_b_group[i], moe_w_expert[i], moe_b_expert[i],
                        moe_w_gate[i], moe_w_up[i], moe_w_down[i])
        h = _layer_norm(DN_ALPHA * h + ffn, ln_ffn_g[i], ln_ffn_b[i])
    return h
```

```python
import functools
import math

import numpy as np
import jax
import jax.numpy as jnp
from jax import lax
from jax.experimental import pallas as pl
from jax.experimental.pallas import tpu as pltpu

F32 = jnp.float32
BF16 = jnp.bfloat16

D_MODEL = 2048
DEPTH = 4
S5_GROUP = 16
S5_GROUPS = D_MODEL // S5_GROUP
S5_STATE = 64
S5_CHUNK = 64
ATT_HEADS = 16
ATT_HEAD_DIM = 128
Q_LORA = 512
KV_LORA = 512
IDX_HEADS = 16
IDX_DIM = 128
IDX_TOPK = 256
Q_BLOCK = 128
KEY_CHUNK = 512
REL_BUCKETS = 32
REL_MAX_DIST = 128
MOE_GROUPS = 4
MOE_PER_GROUP = 8
MOE_EXPERTS = 32
MOE_FF = 256
DN_ALPHA = (2 * DEPTH) ** 0.25
LN_EPS = 1e-5
RMS_EPS = 1e-6

LANES = 128
MASKED = -1e30
VMEM_LIMIT = 56 * 1024 * 1024


def _params(*sem):
    return pltpu.CompilerParams(dimension_semantics=sem, vmem_limit_bytes=VMEM_LIMIT)


def _layer_norm_rows(y, g, b):
    mean = jnp.mean(y, axis=-1, keepdims=True)
    yc = y - mean
    var = jnp.mean(yc * yc, axis=-1, keepdims=True)
    return yc * lax.rsqrt(var + LN_EPS) * g + b


def _mm_plain_kernel(x_ref, w_ref, o_ref, *, tn):
    xb = x_ref[...].astype(BF16)
    for n0 in range(0, o_ref.shape[1], tn):
        acc = jnp.dot(xb, w_ref[:, n0:n0 + tn], preferred_element_type=F32)
        o_ref[:, n0:n0 + tn] = acc.astype(o_ref.dtype)


def _mm_glu_kernel(z_ref, w_ref, o_ref, *, tn):
    zb = z_ref[...]
    for n0 in range(0, o_ref.shape[1], tn):
        acc = jnp.dot(zb, w_ref[:, n0:n0 + tn], preferred_element_type=F32)
        zc = z_ref[:, n0:n0 + tn].astype(F32)
        o_ref[:, n0:n0 + tn] = (zc * jax.nn.sigmoid(acc)).astype(o_ref.dtype)


def _mm_ln_kernel(x_ref, w_ref, res_ref, g_ref, b_ref, o_ref, *, tn):
    xb = x_ref[...].astype(BF16)
    for n0 in range(0, o_ref.shape[1], tn):
        acc = jnp.dot(xb, w_ref[:, n0:n0 + tn], preferred_element_type=F32)
        o_ref[:, n0:n0 + tn] = DN_ALPHA * res_ref[:, n0:n0 + tn] + acc
    o_ref[...] = _layer_norm_rows(o_ref[...], g_ref[...], b_ref[...])


def _linear(x, w, *, tm, out_dtype, tn=512):
    m, k = x.shape
    n = w.shape[1]
    return pl.pallas_call(
        functools.partial(_mm_plain_kernel, tn=tn),
        out_shape=jax.ShapeDtypeStruct((m, n), out_dtype),
        grid=(m // tm,),
        in_specs=[pl.BlockSpec((tm, k), lambda i: (i, 0)),
                  pl.BlockSpec((k, n), lambda i: (0, 0))],
        out_specs=pl.BlockSpec((tm, n), lambda i: (i, 0)),
        compiler_params=_params("parallel"),
        name="linear",
    )(x, w)


def _linear_glu(z, w, *, tm, tn=512):
    m, k = z.shape
    return pl.pallas_call(
        functools.partial(_mm_glu_kernel, tn=tn),
        out_shape=jax.ShapeDtypeStruct((m, k), BF16),
        grid=(m // tm,),
        in_specs=[pl.BlockSpec((tm, k), lambda i: (i, 0)),
                  pl.BlockSpec((k, k), lambda i: (0, 0))],
        out_specs=pl.BlockSpec((tm, k), lambda i: (i, 0)),
        compiler_params=_params("parallel"),
        name="linear_glu",
    )(z, w)


def _linear_ln(x, w, res, g, b, *, tm, tn=512):
    m, k = x.shape
    n = w.shape[1]
    return pl.pallas_call(
        functools.partial(_mm_ln_kernel, tn=tn),
        out_shape=jax.ShapeDtypeStruct((m, n), F32),
        grid=(m // tm,),
        in_specs=[pl.BlockSpec((tm, k), lambda i: (i, 0)),
                  pl.BlockSpec((k, n), lambda i: (0, 0)),
                  pl.BlockSpec((tm, n), lambda i: (i, 0)),
                  pl.BlockSpec((1, n), lambda i: (0, 0)),
                  pl.BlockSpec((1, n), lambda i: (0, 0))],
        out_specs=pl.BlockSpec((tm, n), lambda i: (i, 0)),
        compiler_params=_params("parallel"),
        name="linear_ln",
    )(x, w, res, g.reshape(1, n), b.reshape(1, n))


def _gelu_tanh(x):
    return 0.5 * x * (1.0 + jnp.tanh(math.sqrt(2.0 / math.pi) * (x + 0.044715 * x * x * x)))


def _s5_kernel(u_ref, ar_ref, ai_ref, ldt_ref, br_ref, bi_ref, cr_ref, ci_ref, d_ref, o_ref, w_scr,
               *, batch, n_chunks):
    cs, p, n = S5_CHUNK, S5_GROUP, S5_STATE
    width = cs * p
    n2 = 2 * n

    @pl.when(pl.program_id(0) == 0)
    def _():
        w_scr[...] = jnp.zeros(w_scr.shape, w_scr.dtype)

    dt = jnp.exp(ldt_ref[0])
    lam_r = jnp.minimum(ar_ref[0], -1e-4)
    lam_i = ai_ref[0]
    lo_half = lax.broadcasted_iota(jnp.int32, (1, n2), 1) < n

    def a_pow(tau):
        mag = jnp.exp(lam_r * dt * tau)
        ph = lam_i * dt * tau
        return mag * jnp.cos(ph), mag * jnp.sin(ph)

    tau = lax.broadcasted_iota(jnp.int32, (cs, 1), 0).astype(F32)
    a0r, a0i = a_pow(tau)
    a1r, a1i = a_pow(tau + 1.0)
    avr, avi = a_pow((cs - 1.0) - tau)
    acr, aci = a_pow(jnp.full((1, 1), float(cs), F32))

    abr, abi = a1r[0:1], a1i[0:1]
    den = lam_r * lam_r + lam_i * lam_i
    kr = ((abr - 1.0) * lam_r + abi * lam_i) / den
    ki = (abi * lam_r - (abr - 1.0) * lam_i) / den
    b_r, b_i = br_ref[0], bi_ref[0]
    bbr = kr * b_r - ki * b_i
    bbi = kr * b_i + ki * b_r
    c_r, c_i = cr_ref[0], ci_ref[0]

    def rep_rows(x):
        return jnp.concatenate([jnp.broadcast_to(x[t:t + 1, :], (p, n2)) for t in range(cs)], axis=0)

    def tile_rows(x):
        return jnp.concatenate([x] * cs, axis=0)

    def sel(lo, hi):
        return jnp.where(lo_half, lo, hi)

    crr, cii = tile_rows(c_r), tile_rows(c_i)
    brr, bii = tile_rows(bbr), tile_rows(bbi)
    c_taps = crr * rep_rows(sel(a0r, -a0i)) - cii * rep_rows(sel(a0i, a0r))
    c_out = crr * rep_rows(sel(a1r, -a1i)) - cii * rep_rows(sel(a1i, a1r))
    b_in = brr * rep_rows(sel(avr, avi)) + bii * rep_rows(sel(-avi, avr))
    b_in_sw = brr * rep_rows(sel(avi, avr)) + bii * rep_rows(sel(avr, -avi))

    b_mat = sel(bbr, bbi)
    taps = lax.dot_general(b_mat, c_taps, (((1,), (1,)), ((), ())),
                           precision=lax.Precision.HIGHEST, preferred_element_type=F32)

    lane = lax.broadcasted_iota(jnp.int32, (p, width), 1)
    band = [taps]
    for j in range(1, LANES // p):
        band.append(jnp.where(lane >= p * j, pltpu.roll(taps, p * j, axis=1), 0.0))
    band = jnp.concatenate(band, axis=0).astype(BF16)
    for q in range(width // LANES):
        w_scr[q * LANES:(q + 1) * LANES, q * LANES:width] = band[:, 0:width - q * LANES]
    w_scr[:, width:width + n2] = b_in.astype(BF16)
    w_scr[:, width + n2:width + 2 * n2] = b_in_sw.astype(BF16)

    u = u_ref[0]
    res = jnp.dot(u, w_scr[...], preferred_element_type=F32)
    y = res[:, 0:width]
    s_in = res[:, width:width + n2]
    s_sw = res[:, width + n2:width + 2 * n2]

    m_rr = acr
    m_ni = sel(-aci, aci)
    m_pi = sel(aci, -aci)
    h = jnp.zeros((batch, n2), F32)
    h_sw = jnp.zeros((batch, n2), F32)
    starts = []
    for c in range(n_chunks):
        starts.append(h)
        rows = slice(c * batch, (c + 1) * batch)
        h, h_sw = (h * m_rr + h_sw * m_ni + s_in[rows], h_sw * m_rr + h * m_pi + s_sw[rows])
    h_start = jnp.concatenate(starts, axis=0).astype(BF16)
    y = y + lax.dot_general(h_start, c_out.astype(BF16), (((1,), (1,)), ((), ())),
                            preferred_element_type=F32)
    y = y + d_ref[0] * u.astype(F32)
    o_ref[0] = _gelu_tanh(y).astype(o_ref.dtype)


def _s5_core(u_t, a_re, a_im, log_dt, b_re, b_im, c_re, c_im, d, *, batch, n_chunks):
    g, rows, width = u_t.shape
    dup = lambda x: jnp.concatenate([x, x], axis=-1)
    a_re2 = dup(a_re).reshape(g, 1, 2 * S5_STATE)
    a_im2 = dup(a_im).reshape(g, 1, 2 * S5_STATE)
    ldt = log_dt.reshape(g, 1, 1)
    bt_re2 = dup(jnp.swapaxes(b_re, 1, 2))
    bt_im2 = dup(jnp.swapaxes(b_im, 1, 2))
    c_re2 = dup(c_re)
    c_im2 = dup(c_im)
    d_t = jnp.tile(d, (1, S5_CHUNK)).reshape(g, 1, width)
    vec = lambda w: pl.BlockSpec((1, 1, w), lambda i: (i, 0, 0))
    mat = pl.BlockSpec((1, S5_GROUP, 2 * S5_STATE), lambda i: (i, 0, 0))
    return pl.pallas_call(
        functools.partial(_s5_kernel, batch=batch, n_chunks=n_chunks),
        out_shape=jax.ShapeDtypeStruct((g, rows, width), BF16),
        grid=(g,),
        in_specs=[pl.BlockSpec((1, rows, width), lambda i: (i, 0, 0)),
                  vec(2 * S5_STATE), vec(2 * S5_STATE), vec(1), mat, mat, mat, mat, vec(width)],
        out_specs=pl.BlockSpec((1, rows, width), lambda i: (i, 0, 0)),
        scratch_shapes=[pltpu.VMEM((width, width + 4 * S5_STATE), BF16)],
        compiler_params=_params("arbitrary"),
        name="s5_core",
    )(u_t, a_re2, a_im2, ldt, bt_re2, bt_im2, c_re2, c_im2, d_t)


def _s5_mixer(h, res_g, res_b, w_in, a_re, a_im, log_dt, b_re, b_im, c_re, c_im, d, w_glu, w_out,
              *, batch, seq):
    t = batch * seq
    nc = seq // S5_CHUNK
    u = _linear(h, w_in, tm=512, out_dtype=BF16)
    u_t = u.reshape(batch, nc, S5_CHUNK, S5_GROUPS, S5_GROUP).transpose(3, 1, 0, 2, 4)
    u_t = u_t.reshape(S5_GROUPS, nc * batch, S5_CHUNK * S5_GROUP)
    z_t = _s5_core(u_t, a_re, a_im, log_dt, b_re, b_im, c_re, c_im, d, batch=batch, n_chunks=nc)
    z = z_t.reshape(S5_GROUPS, nc, batch, S5_CHUNK, S5_GROUP).transpose(2, 1, 3, 0, 4)
    z = z.reshape(t, D_MODEL)
    zg = _linear_glu(z, w_glu, tm=512)
    return _linear_ln(zg, w_out, h, res_g, res_b, tm=256)


def _dsa_proj_kernel(x_ref, w_ref, qn_ref, kn_ref, cq_ref, ckv_ref, kidx_ref, widx_ref):
    xb = x_ref[...].astype(BF16)

    def rms(v, g):
        return v * lax.rsqrt(jnp.mean(v * v, axis=-1, keepdims=True) + RMS_EPS) * g

    o1 = Q_LORA + KV_LORA
    cq = jnp.dot(xb, w_ref[:, 0:Q_LORA], preferred_element_type=F32)
    cq_ref[...] = rms(cq, qn_ref[...]).astype(BF16)
    ckv = jnp.dot(xb, w_ref[:, Q_LORA:o1], preferred_element_type=F32)
    ckv_ref[...] = rms(ckv, kn_ref[...]).astype(BF16)
    kidx_ref[...] = jnp.dot(xb, w_ref[:, o1:o1 + IDX_DIM], preferred_element_type=F32).astype(BF16)
    widx = jnp.dot(xb, w_ref[:, o1 + IDX_DIM:o1 + IDX_DIM + LANES], preferred_element_type=F32)
    widx_ref[...] = widx * (IDX_HEADS ** -0.5)


def _dsa_proj(h, w_in_pad, q_norm, kv_norm, *, tm=512):
    t = h.shape[0]
    npad = w_in_pad.shape[1]
    row = lambda w: pl.BlockSpec((tm, w), lambda i: (i, 0))
    return pl.pallas_call(
        _dsa_proj_kernel,
        out_shape=(jax.ShapeDtypeStruct((t, Q_LORA), BF16), jax.ShapeDtypeStruct((t, KV_LORA), BF16),
                   jax.ShapeDtypeStruct((t, IDX_DIM), BF16), jax.ShapeDtypeStruct((t, LANES), F32)),
        grid=(t // tm,),
        in_specs=[row(D_MODEL), pl.BlockSpec((D_MODEL, npad), lambda i: (0, 0)),
                  pl.BlockSpec((1, Q_LORA), lambda i: (0, 0)), pl.BlockSpec((1, KV_LORA), lambda i: (0, 0))],
        out_specs=(row(Q_LORA), row(KV_LORA), row(IDX_DIM), row(LANES)),
        compiler_params=_params("parallel"),
        name="dsa_proj",
    )(h, w_in_pad, q_norm.reshape(1, Q_LORA), kv_norm.reshape(1, KV_LORA))


def _indexer_kernel(q_ref, k_ref, w_ref, s_ref, thr_ref, key_scr, *, seq):
    jb = pl.program_id(1)
    k = k_ref[...]
    acc = jnp.zeros((Q_BLOCK, seq), F32)
    for hh in range(IDX_HEADS):
        qh = q_ref[:, hh * IDX_DIM:(hh + 1) * IDX_DIM]
        s = lax.dot_general(qh, k, (((1,), (1,)), ((), ())), preferred_element_type=F32)
        acc = acc + jnp.maximum(s * (IDX_DIM ** -0.5), 0.0) * w_ref[:, hh:hh + 1]
    q_pos = jb * Q_BLOCK + lax.broadcasted_iota(jnp.int32, (Q_BLOCK, seq), 0)
    k_pos = lax.broadcasted_iota(jnp.int32, (Q_BLOCK, seq), 1)
    acc = jnp.where(k_pos <= q_pos, acc, -jnp.inf)
    s_ref[...] = acc

    bits = lax.bitcast_convert_type(acc, jnp.int32)
    key_scr[...] = jnp.where(bits < 0, bits ^ jnp.int32(0x7FFFFFFF), bits)
    n_sel = float(min(IDX_TOPK, seq // 4))

    def count_ge(cand):
        return jnp.sum((key_scr[...] >= cand).astype(F32), axis=1, keepdims=True)

    int_min = jnp.int32(-2 ** 31)
    lo = jnp.where(count_ge(jnp.zeros((Q_BLOCK, 1), jnp.int32)) >= n_sel, jnp.int32(0), int_min)

    def body(i, lo):
        cand = lo + jnp.left_shift(jnp.int32(1), jnp.int32(30) - i)
        return jnp.where(count_ge(cand) >= n_sel, cand, lo)

    lo = lax.fori_loop(0, 31, body, lo)
    thr = lax.bitcast_convert_type(jnp.where(lo < 0, lo ^ jnp.int32(0x7FFFFFFF), lo), F32)
    thr_ref[...] = jnp.broadcast_to(thr, (Q_BLOCK, LANES))


def _indexer(qcat, kidx, widx, *, batch, seq):
    t = batch * seq
    n_blk = seq // Q_BLOCK
    return pl.pallas_call(
        functools.partial(_indexer_kernel, seq=seq),
        out_shape=(jax.ShapeDtypeStruct((t, seq), F32), jax.ShapeDtypeStruct((t, LANES), F32)),
        grid=(batch, n_blk),
        in_specs=[pl.BlockSpec((Q_BLOCK, IDX_HEADS * IDX_DIM), lambda b, j: (b * n_blk + j, 1)),
                  pl.BlockSpec((seq, IDX_DIM), lambda b, j: (b, 0)),
                  pl.BlockSpec((Q_BLOCK, LANES), lambda b, j: (b * n_blk + j, 0))],
        out_specs=(pl.BlockSpec((Q_BLOCK, seq), lambda b, j: (b * n_blk + j, 0)),
                   pl.BlockSpec((Q_BLOCK, LANES), lambda b, j: (b * n_blk + j, 0))),
        scratch_shapes=[pltpu.VMEM((Q_BLOCK, seq), jnp.int32)],
        compiler_params=_params("parallel", "parallel"),
        name="indexer",
    )(qcat, kidx, widx)


def _attn_kernel(q_ref, kv_ref, s_ref, thr_ref, wuk_ref, wuv_ref, bd_ref, bp_ref, bf_ref, o_ref,
                 ql_scr, m_scr, l_scr, acc_scr, s_scr, p_scr):
    jb = pl.program_id(1)
    kc = pl.program_id(2)
    last = (jb * Q_BLOCK) // KEY_CHUNK
    hd, qb = ATT_HEAD_DIM, Q_BLOCK
    sub = KEY_CHUNK // qb

    @pl.when(kc == 0)
    def _():
        for h in range(ATT_HEADS):
            ql = jnp.dot(q_ref[:, h * hd:(h + 1) * hd], wuk_ref[h], preferred_element_type=F32)
            ql_scr[h * qb:(h + 1) * qb, :] = ql.astype(BF16)
        m_scr[...] = jnp.full(m_scr.shape, MASKED, F32)
        l_scr[...] = jnp.zeros(l_scr.shape, F32)
        acc_scr[...] = jnp.zeros(acc_scr.shape, F32)

    @pl.when(kc <= last)
    def _():
        kv = kv_ref[...]
        s_scr[...] = lax.dot_general(ql_scr[...], kv, (((1,), (1,)), ((), ())),
                                     preferred_element_type=F32)
        selected = s_ref[...] >= thr_ref[:, 0:1]
        for h in range(ATT_HEADS):
            rows = slice(h * qb, (h + 1) * qb)
            tiles = []
            for c in range(sub):
                gblk = kc * sub + c
                far = jnp.where(gblk < jb, bf_ref[h], MASKED)
                tiles.append(jnp.where(gblk == jb, bd_ref[h],
                                       jnp.where(gblk == jb - 1, bp_ref[h], far)))
            bias = jnp.concatenate(tiles, axis=1)
            lg = jnp.where(selected, s_scr[rows, :] * (hd ** -0.5) + bias, MASKED)
            m_prev = m_scr[rows, 0:1]
            m_new = jnp.maximum(m_prev, jnp.max(lg, axis=1, keepdims=True))
            corr = jnp.exp(m_prev - m_new)
            pr = jnp.exp(lg - m_new)
            l_scr[rows, :] = jnp.broadcast_to(corr * l_scr[rows, 0:1] + jnp.sum(pr, axis=1, keepdims=True),
                                              (qb, LANES))
            m_scr[rows, :] = jnp.broadcast_to(m_new, (qb, LANES))
            p_scr[rows, :] = pr.astype(BF16)
            acc_scr[rows, :] = acc_scr[rows, :] * corr
        acc_scr[...] += jnp.dot(p_scr[...], kv, preferred_element_type=F32)

    @pl.when(kc == last)
    def _():
        for h in range(ATT_HEADS):
            rows = slice(h * qb, (h + 1) * qb)
            ol = (acc_scr[rows, :] / l_scr[rows, 0:1]).astype(BF16)
            o_ref[:, h * hd:(h + 1) * hd] = jnp.dot(ol, wuv_ref[h], preferred_element_type=F32).astype(o_ref.dtype)


def _rel_bucket_table():
    n = np.arange(2 * Q_BLOCK)
    max_exact = REL_BUCKETS // 2
    nf = np.maximum(n, 1).astype(np.float32)
    large = max_exact + (np.log(nf / np.float32(max_exact)) / np.float32(math.log(REL_MAX_DIST / max_exact))
                         * np.float32(REL_BUCKETS - max_exact)).astype(np.int32)
    large = np.minimum(large, REL_BUCKETS - 1)
    return np.where(n < max_exact, n, large)


def _attention(qcat, ckv, scores, thr, w_uk, w_uv, rel_bias, *, batch, seq):
    t = batch * seq
    n_blk = seq // Q_BLOCK
    n_kc = seq // KEY_CHUNK
    sub = KEY_CHUNK // Q_BLOCK
    bucket = _rel_bucket_table()
    i = np.arange(Q_BLOCK)[:, None]
    j = np.arange(Q_BLOCK)[None, :]
    rb = rel_bias.astype(F32)
    diag = jnp.where(jnp.asarray(j <= i)[None], jnp.moveaxis(rb[bucket[np.maximum(i - j, 0)]], -1, 0), MASKED)
    prev = jnp.moveaxis(rb[bucket[Q_BLOCK + i - j]], -1, 0)
    far = jnp.broadcast_to(rb[REL_BUCKETS - 1][:, None, None], (ATT_HEADS, 1, LANES))
    hq = ATT_HEADS * Q_BLOCK

    def kv_map(b, jj, kc):
        return (b * n_kc + jnp.minimum(kc, (jj * Q_BLOCK) // KEY_CHUNK), 0)

    def s_map(b, jj, kc):
        return (b * n_blk + jj, jnp.minimum(kc, (jj * Q_BLOCK) // KEY_CHUNK))

    qrow = lambda b, jj, kc: (b * n_blk + jj, 0)
    full3 = lambda b, jj, kc: (0, 0, 0)
    return pl.pallas_call(
        _attn_kernel,
        out_shape=jax.ShapeDtypeStruct((t, ATT_HEADS * ATT_HEAD_DIM), BF16),
        grid=(batch, n_blk, n_kc),
        in_specs=[pl.BlockSpec((Q_BLOCK, ATT_HEADS * ATT_HEAD_DIM), qrow),
                  pl.BlockSpec((KEY_CHUNK, KV_LORA), kv_map),
                  pl.BlockSpec((Q_BLOCK, KEY_CHUNK), s_map),
                  pl.BlockSpec((Q_BLOCK, LANES), qrow),
                  pl.BlockSpec((ATT_HEADS, ATT_HEAD_DIM, KV_LORA), full3),
                  pl.BlockSpec((ATT_HEADS, KV_LORA, ATT_HEAD_DIM), full3),
                  pl.BlockSpec((ATT_HEADS, Q_BLOCK, Q_BLOCK), full3),
                  pl.BlockSpec((ATT_HEADS, Q_BLOCK, Q_BLOCK), full3),
                  pl.BlockSpec((ATT_HEADS, 1, LANES), full3)],
        out_specs=pl.BlockSpec((Q_BLOCK, ATT_HEADS * ATT_HEAD_DIM), qrow),
        scratch_shapes=[pltpu.VMEM((hq, KV_LORA), BF16),
                        pltpu.VMEM((hq, LANES), F32),
                        pltpu.VMEM((hq, LANES), F32),
                        pltpu.VMEM((hq, KV_LORA), F32),
                        pltpu.VMEM((hq, KEY_CHUNK), F32),
                        pltpu.VMEM((hq, KEY_CHUNK), BF16)],
        compiler_params=_params("parallel", "parallel", "arbitrary"),
        name="latent_attention",
    )(qcat, ckv, scores, thr, w_uk, w_uv, diag, prev, far)


def _dsa_mixer(h, res_g, res_b, rel_bias, w_in_pad, q_norm, kv_norm, w_q, w_uk, w_uv, w_out, *, batch, seq):
    cq, ckv, kidx, widx = _dsa_proj(h, w_in_pad, q_norm, kv_norm)
    qcat = _linear(cq, w_q, tm=512, out_dtype=BF16)
    scores, thr = _indexer(qcat, kidx, widx, batch=batch, seq=seq)
    o = _attention(qcat, ckv, scores, thr, w_uk, w_uv, rel_bias, batch=batch, seq=seq)
    return _linear_ln(o, w_out, h, res_g, res_b, tm=256)


def _router_kernel(x_ref, w_ref, b_ref, gate_ref):
    logits = jnp.dot(x_ref[...], w_ref[...], precision=lax.Precision.HIGHEST,
                     preferred_element_type=F32) + b_ref[...]
    lane = lax.broadcasted_iota(jnp.int32, logits.shape, 1)
    big = jnp.int32(LANES)
    is_group = (lane >= MOE_EXPERTS) & (lane < MOE_EXPERTS + MOE_GROUPS)
    gl = jnp.where(is_group, logits, -jnp.inf)
    gm = jnp.max(gl, axis=1, keepdims=True)
    g_p = 1.0 / jnp.sum(jnp.exp(gl - gm), axis=1, keepdims=True)
    g_idx = jnp.min(jnp.where(gl == gm, lane, big), axis=1, keepdims=True) - MOE_EXPERTS
    in_group = (lane < MOE_EXPERTS) & (jnp.right_shift(lane, 3) == g_idx)
    el = jnp.where(in_group, logits, -jnp.inf)
    em = jnp.max(el, axis=1, keepdims=True)
    ee = jnp.exp(el - em)
    prob = ee / jnp.sum(ee, axis=1, keepdims=True)
    p1 = jnp.max(prob, axis=1, keepdims=True)
    i1 = jnp.min(jnp.where(in_group & (prob == p1), lane, big), axis=1, keepdims=True)
    rest = in_group & (lane != i1)
    p2 = jnp.max(jnp.where(rest, prob, -1.0), axis=1, keepdims=True)
    i2 = jnp.min(jnp.where(rest & (prob == p2), lane, big), axis=1, keepdims=True)
    den = p1 + p2
    within = jnp.where(lane == i1, p1 / den, jnp.where(lane == i2, p2 / den, 0.0))
    gate_ref[...] = g_p * within


def _router(h, w_r, b_r, *, tm=512):
    t = h.shape[0]
    return pl.pallas_call(
        _router_kernel,
        out_shape=jax.ShapeDtypeStruct((t, LANES), F32),
        grid=(t // tm,),
        in_specs=[pl.BlockSpec((tm, D_MODEL), lambda i: (i, 0)),
                  pl.BlockSpec((D_MODEL, LANES), lambda i: (0, 0)),
                  pl.BlockSpec((1, LANES), lambda i: (0, 0))],
        out_specs=pl.BlockSpec((tm, LANES), lambda i: (i, 0)),
        compiler_params=_params("parallel"),
        name="moe_router",
    )(h, w_r, b_r)


def _moe_dense_kernel(x_ref, gate_ref, wg_ref, wu_ref, wd_ref, g_ref, b_ref, o_ref, xb_scr, acc_scr):
    e = pl.program_id(1)

    @pl.when(e == 0)
    def _():
        xb_scr[...] = x_ref[...].astype(BF16)
        acc_scr[...] = jnp.zeros(acc_scr.shape, F32)

    xb = xb_scr[...]
    gp = jnp.dot(xb, wg_ref[0], preferred_element_type=F32)
    up = jnp.dot(xb, wu_ref[0], preferred_element_type=F32)
    gate = gate_ref[...]
    lane = lax.broadcasted_iota(jnp.int32, gate.shape, 1)
    ge = jnp.sum(jnp.where(lane == e, gate, 0.0), axis=1, keepdims=True)
    hidden = (gp * jax.nn.sigmoid(gp)) * up * ge
    acc_scr[...] += jnp.dot(hidden.astype(BF16), wd_ref[0], preferred_element_type=F32)

    @pl.when(e == MOE_EXPERTS - 1)
    def _():
        o_ref[...] = _layer_norm_rows(DN_ALPHA * x_ref[...] + acc_scr[...], g_ref[...], b_ref[...])


def _moe_dense(h, gate, w_gate, w_up, w_down, ln_g, ln_b, *, tm=512):
    t = h.shape[0]
    row = lambda w: pl.BlockSpec((tm, w), lambda i, e: (i, 0))
    return pl.pallas_call(
        _moe_dense_kernel,
        out_shape=jax.ShapeDtypeStruct((t, D_MODEL), F32),
        grid=(t // tm, MOE_EXPERTS),
        in_specs=[row(D_MODEL), row(LANES),
                  pl.BlockSpec((1, D_MODEL, MOE_FF), lambda i, e: (e, 0, 0)),
                  pl.BlockSpec((1, D_MODEL, MOE_FF), lambda i, e: (e, 0, 0)),
                  pl.BlockSpec((1, MOE_FF, D_MODEL), lambda i, e: (e, 0, 0)),
                  pl.BlockSpec((1, D_MODEL), lambda i, e: (0, 0)),
                  pl.BlockSpec((1, D_MODEL), lambda i, e: (0, 0))],
        out_specs=row(D_MODEL),
        scratch_shapes=[pltpu.VMEM((tm, D_MODEL), BF16), pltpu.VMEM((tm, D_MODEL), F32)],
        compiler_params=_params("parallel", "arbitrary"),
        name="moe_dense",
    )(h, gate, w_gate, w_up, w_down, ln_g.reshape(1, D_MODEL), ln_b.reshape(1, D_MODEL))


def _hier_moe(h, w_group, b_group, w_expert, b_expert, w_gate, w_up, w_down, ln_g, ln_b):
    pad = LANES - MOE_EXPERTS - MOE_GROUPS
    w_r = jnp.concatenate([w_expert, w_group, jnp.zeros((D_MODEL, pad), F32)], axis=1)
    b_r = jnp.concatenate([b_expert, b_group, jnp.zeros((pad,), F32)]).reshape(1, LANES)
    gate = _router(h, w_r, b_r)
    return _moe_dense(h, gate, w_gate.astype(BF16), w_up.astype(BF16), w_down.astype(BF16), ln_g, ln_b)


def kernel(x, rel_bias, s5_w_in, s5_a_re, s5_a_im, s5_log_dt, s5_b_re, s5_b_im, s5_c_re, s5_c_im, s5_d, s5_w_glu, s5_w_out, dsa_w_in, dsa_q_norm, dsa_kv_norm, dsa_w_uq, dsa_w_qidx, dsa_w_uk, dsa_w_uv, dsa_w_out, moe_w_group, moe_b_group, moe_w_expert, moe_b_expert, moe_w_gate, moe_w_up, moe_w_down, ln_mix_g, ln_mix_b, ln_ffn_g, ln_ffn_b):
    batch, seq, dm = x.shape
    h = x.reshape(batch * seq, dm)
    for i in range(DEPTH):
        j = i // 2
        if i % 2 == 0:
            h = _s5_mixer(h, ln_mix_g[i], ln_mix_b[i], s5_w_in[j].astype(BF16), s5_a_re[j], s5_a_im[j],
                          s5_log_dt[j], s5_b_re[j], s5_b_im[j], s5_c_re[j], s5_c_im[j], s5_d[j],
                          s5_w_glu[j].astype(BF16), s5_w_out[j].astype(BF16), batch=batch, seq=seq)
        else:
            n_in = dsa_w_in.shape[-1]
            n_pad = Q_LORA + KV_LORA + IDX_DIM + LANES
            w_in_pad = jnp.pad(dsa_w_in[j], ((0, 0), (0, n_pad - n_in))).astype(BF16)
            w_q = jnp.concatenate([dsa_w_uq[j], dsa_w_qidx[j]], axis=1).astype(BF16)
            h = _dsa_mixer(h, ln_mix_g[i], ln_mix_b[i], rel_bias, w_in_pad, dsa_q_norm[j], dsa_kv_norm[j], w_q,
                           dsa_w_uk[j].astype(BF16), dsa_w_uv[j].astype(BF16), dsa_w_out[j].astype(BF16),
                           batch=batch, seq=seq)
        h = _hier_moe(h, moe_w_group[i], moe_b_group[i], moe_w_expert[i], moe_b_expert[i],
                      moe_w_gate[i], moe_w_up[i], moe_w_down[i], ln_ffn_g[i], ln_ffn_b[i])
    return h.reshape(batch, seq, dm)
```

```python
import functools
import math

import numpy as np
import jax
import jax.numpy as jnp
from jax import lax
from jax.experimental import pallas as pl
from jax.experimental.pallas import tpu as pltpu

F32 = jnp.float32
BF16 = jnp.bfloat16

D_MODEL = 2048
DEPTH = 4
S5_GROUP = 16
S5_GROUPS = D_MODEL // S5_GROUP
S5_STATE = 64
S5_CHUNK = 64
ATT_HEADS = 16
ATT_HEAD_DIM = 128
Q_LORA = 512
KV_LORA = 512
IDX_HEADS = 16
IDX_DIM = 128
IDX_TOPK = 256
Q_BLOCK = 128
KEY_CHUNK = 512
REL_BUCKETS = 32
REL_MAX_DIST = 128
MOE_GROUPS = 4
MOE_PER_GROUP = 8
MOE_EXPERTS = 32
MOE_FF = 256
EXPERT_TILE = 256
COMBINE_TILE = 256
DN_ALPHA = (2 * DEPTH) ** 0.25
LN_EPS = 1e-5
RMS_EPS = 1e-6

LANES = 128
MASKED = -1e30
VMEM_LIMIT = 56 * 1024 * 1024


def _params(*sem):
    return pltpu.CompilerParams(dimension_semantics=sem, vmem_limit_bytes=VMEM_LIMIT)


def _layer_norm_rows(y, g, b):
    mean = jnp.mean(y, axis=-1, keepdims=True)
    yc = y - mean
    var = jnp.mean(yc * yc, axis=-1, keepdims=True)
    return yc * lax.rsqrt(var + LN_EPS) * g + b


def _mm_plain_kernel(x_ref, w_ref, o_ref, *, tn):
    xb = x_ref[...].astype(BF16)
    for n0 in range(0, o_ref.shape[1], tn):
        acc = jnp.dot(xb, w_ref[:, n0:n0 + tn], preferred_element_type=F32)
        o_ref[:, n0:n0 + tn] = acc.astype(o_ref.dtype)


def _mm_glu_kernel(z_ref, w_ref, o_ref, *, tn):
    zb = z_ref[...]
    for n0 in range(0, o_ref.shape[1], tn):
        acc = jnp.dot(zb, w_ref[:, n0:n0 + tn], preferred_element_type=F32)
        zc = z_ref[:, n0:n0 + tn].astype(F32)
        o_ref[:, n0:n0 + tn] = (zc * jax.nn.sigmoid(acc)).astype(o_ref.dtype)


def _store_row_tiles(o3_ref, y):
    for c in range(y.shape[1] // LANES):
        o3_ref[:, c, :] = y[:, c * LANES:(c + 1) * LANES]


def _load_row_tiles(x3_ref):
    return jnp.concatenate([x3_ref[:, c, :] for c in range(x3_ref.shape[1])], axis=1)


def _mm_ln_kernel(x_ref, w_ref, res_ref, g_ref, b_ref, o_ref, o3_ref, *, tn):
    xb = x_ref[...].astype(BF16)
    for n0 in range(0, o_ref.shape[1], tn):
        acc = jnp.dot(xb, w_ref[:, n0:n0 + tn], preferred_element_type=F32)
        o_ref[:, n0:n0 + tn] = DN_ALPHA * res_ref[:, n0:n0 + tn] + acc
    y = _layer_norm_rows(o_ref[...], g_ref[...], b_ref[...])
    o_ref[...] = y
    _store_row_tiles(o3_ref, y)


def _linear(x, w, *, tm, out_dtype, tn=512):
    m, k = x.shape
    n = w.shape[1]
    return pl.pallas_call(
        functools.partial(_mm_plain_kernel, tn=tn),
        out_shape=jax.ShapeDtypeStruct((m, n), out_dtype),
        grid=(m // tm,),
        in_specs=[pl.BlockSpec((tm, k), lambda i: (i, 0)),
                  pl.BlockSpec((k, n), lambda i: (0, 0))],
        out_specs=pl.BlockSpec((tm, n), lambda i: (i, 0)),
        compiler_params=_params("parallel"),
        name="linear",
    )(x, w)


def _linear_glu(z, w, *, tm, tn=512):
    m, k = z.shape
    return pl.pallas_call(
        functools.partial(_mm_glu_kernel, tn=tn),
        out_shape=jax.ShapeDtypeStruct((m, k), BF16),
        grid=(m // tm,),
        in_specs=[pl.BlockSpec((tm, k), lambda i: (i, 0)),
                  pl.BlockSpec((k, k), lambda i: (0, 0))],
        out_specs=pl.BlockSpec((tm, k), lambda i: (i, 0)),
        compiler_params=_params("parallel"),
        name="linear_glu",
    )(z, w)


def _linear_ln(x, w, res, g, b, *, tm, tn=512):
    m, k = x.shape
    n = w.shape[1]
    return pl.pallas_call(
        functools.partial(_mm_ln_kernel, tn=tn),
        out_shape=(jax.ShapeDtypeStruct((m, n), F32), jax.ShapeDtypeStruct((m, n // LANES, LANES), F32)),
        grid=(m // tm,),
        in_specs=[pl.BlockSpec((tm, k), lambda i: (i, 0)),
                  pl.BlockSpec((k, n), lambda i: (0, 0)),
                  pl.BlockSpec((tm, n), lambda i: (i, 0)),
                  pl.BlockSpec((1, n), lambda i: (0, 0)),
                  pl.BlockSpec((1, n), lambda i: (0, 0))],
        out_specs=(pl.BlockSpec((tm, n), lambda i: (i, 0)),
                   pl.BlockSpec((tm, n // LANES, LANES), lambda i: (i, 0, 0))),
        compiler_params=_params("parallel"),
        name="linear_ln",
    )(x, w, res, g.reshape(1, n), b.reshape(1, n))


def _gelu_tanh(x):
    return 0.5 * x * (1.0 + jnp.tanh(math.sqrt(2.0 / math.pi) * (x + 0.044715 * x * x * x)))


def _s5_kernel(u_ref, ar_ref, ai_ref, ldt_ref, br_ref, bi_ref, cr_ref, ci_ref, d_ref, o_ref, w_scr,
               *, batch, n_chunks):
    cs, p, n = S5_CHUNK, S5_GROUP, S5_STATE
    width = cs * p
    n2 = 2 * n

    @pl.when(pl.program_id(0) == 0)
    def _():
        w_scr[...] = jnp.zeros(w_scr.shape, w_scr.dtype)

    dt = jnp.exp(ldt_ref[0])
    lam_r = jnp.minimum(ar_ref[0], -1e-4)
    lam_i = ai_ref[0]
    lo_half = lax.broadcasted_iota(jnp.int32, (1, n2), 1) < n

    def a_pow(tau):
        mag = jnp.exp(lam_r * dt * tau)
        ph = lam_i * dt * tau
        return mag * jnp.cos(ph), mag * jnp.sin(ph)

    tau = lax.broadcasted_iota(jnp.int32, (cs, 1), 0).astype(F32)
    a0r, a0i = a_pow(tau)
    a1r, a1i = a_pow(tau + 1.0)
    avr, avi = a_pow((cs - 1.0) - tau)
    acr, aci = a_pow(jnp.full((1, 1), float(cs), F32))

    abr, abi = a1r[0:1], a1i[0:1]
    den = lam_r * lam_r + lam_i * lam_i
    kr = ((abr - 1.0) * lam_r + abi * lam_i) / den
    ki = (abi * lam_r - (abr - 1.0) * lam_i) / den
    b_r, b_i = br_ref[0], bi_ref[0]
    bbr = kr * b_r - ki * b_i
    bbi = kr * b_i + ki * b_r
    c_r, c_i = cr_ref[0], ci_ref[0]

    def rep_rows(x):
        return jnp.concatenate([jnp.broadcast_to(x[t:t + 1, :], (p, n2)) for t in range(cs)], axis=0)

    def tile_rows(x):
        return jnp.concatenate([x] * cs, axis=0)

    def sel(lo, hi):
        return jnp.where(lo_half, lo, hi)

    crr, cii = tile_rows(c_r), tile_rows(c_i)
    brr, bii = tile_rows(bbr), tile_rows(bbi)
    c_taps = crr * rep_rows(sel(a0r, -a0i)) - cii * rep_rows(sel(a0i, a0r))
    c_out = crr * rep_rows(sel(a1r, -a1i)) - cii * rep_rows(sel(a1i, a1r))
    b_in = brr * rep_rows(sel(avr, avi)) + bii * rep_rows(sel(-avi, avr))
    b_in_sw = brr * rep_rows(sel(avi, avr)) + bii * rep_rows(sel(avr, -avi))

    b_mat = sel(bbr, bbi)
    taps = lax.dot_general(b_mat, c_taps, (((1,), (1,)), ((), ())),
                           precision=lax.Precision.HIGHEST, preferred_element_type=F32)

    lane = lax.broadcasted_iota(jnp.int32, (p, width), 1)
    band = [taps]
    for j in range(1, LANES // p):
        band.append(jnp.where(lane >= p * j, pltpu.roll(taps, p * j, axis=1), 0.0))
    band = jnp.concatenate(band, axis=0).astype(BF16)
    for q in range(width // LANES):
        w_scr[q * LANES:(q + 1) * LANES, q * LANES:width] = band[:, 0:width - q * LANES]
    w_scr[:, width:width + n2] = b_in.astype(BF16)
    w_scr[:, width + n2:width + 2 * n2] = b_in_sw.astype(BF16)

    u = u_ref[0]
    res = jnp.dot(u, w_scr[...], preferred_element_type=F32)
    y = res[:, 0:width]
    s_in = res[:, width:width + n2]
    s_sw = res[:, width + n2:width + 2 * n2]

    m_rr = acr
    m_ni = sel(-aci, aci)
    m_pi = sel(aci, -aci)
    h = jnp.zeros((batch, n2), F32)
    h_sw = jnp.zeros((batch, n2), F32)
    starts = []
    for c in range(n_chunks):
        starts.append(h)
        rows = slice(c * batch, (c + 1) * batch)
        h, h_sw = (h * m_rr + h_sw * m_ni + s_in[rows], h_sw * m_rr + h * m_pi + s_sw[rows])
    h_start = jnp.concatenate(starts, axis=0).astype(BF16)
    y = y + lax.dot_general(h_start, c_out.astype(BF16), (((1,), (1,)), ((), ())),
                            preferred_element_type=F32)
    y = y + d_ref[0] * u.astype(F32)
    o_ref[0] = _gelu_tanh(y).astype(o_ref.dtype)


def _s5_core(u_t, a_re, a_im, log_dt, b_re, b_im, c_re, c_im, d, *, batch, n_chunks):
    g, rows, width = u_t.shape
    dup = lambda x: jnp.concatenate([x, x], axis=-1)
    a_re2 = dup(a_re).reshape(g, 1, 2 * S5_STATE)
    a_im2 = dup(a_im).reshape(g, 1, 2 * S5_STATE)
    ldt = log_dt.reshape(g, 1, 1)
    bt_re2 = dup(jnp.swapaxes(b_re, 1, 2))
    bt_im2 = dup(jnp.swapaxes(b_im, 1, 2))
    c_re2 = dup(c_re)
    c_im2 = dup(c_im)
    d_t = jnp.tile(d, (1, S5_CHUNK)).reshape(g, 1, width)
    vec = lambda w: pl.BlockSpec((1, 1, w), lambda i: (i, 0, 0))
    mat = pl.BlockSpec((1, S5_GROUP, 2 * S5_STATE), lambda i: (i, 0, 0))
    return pl.pallas_call(
        functools.partial(_s5_kernel, batch=batch, n_chunks=n_chunks),
        out_shape=jax.ShapeDtypeStruct((g, rows, width), BF16),
        grid=(g,),
        in_specs=[pl.BlockSpec((1, rows, width), lambda i: (i, 0, 0)),
                  vec(2 * S5_STATE), vec(2 * S5_STATE), vec(1), mat, mat, mat, mat, vec(width)],
        out_specs=pl.BlockSpec((1, rows, width), lambda i: (i, 0, 0)),
        scratch_shapes=[pltpu.VMEM((width, width + 4 * S5_STATE), BF16)],
        compiler_params=_params("arbitrary"),
        name="s5_core",
    )(u_t, a_re2, a_im2, ldt, bt_re2, bt_im2, c_re2, c_im2, d_t)


def _s5_mixer(h, res_g, res_b, w_in, a_re, a_im, log_dt, b_re, b_im, c_re, c_im, d, w_glu, w_out,
              *, batch, seq):
    t = batch * seq
    nc = seq // S5_CHUNK
    u = _linear(h, w_in, tm=512, out_dtype=BF16)
    u_t = u.reshape(batch, nc, S5_CHUNK, S5_GROUPS, S5_GROUP).transpose(3, 1, 0, 2, 4)
    u_t = u_t.reshape(S5_GROUPS, nc * batch, S5_CHUNK * S5_GROUP)
    z_t = _s5_core(u_t, a_re, a_im, log_dt, b_re, b_im, c_re, c_im, d, batch=batch, n_chunks=nc)
    z = z_t.reshape(S5_GROUPS, nc, batch, S5_CHUNK, S5_GROUP).transpose(2, 1, 3, 0, 4)
    z = z.reshape(t, D_MODEL)
    zg = _linear_glu(z, w_glu, tm=512)
    return _linear_ln(zg, w_out, h, res_g, res_b, tm=256)


def _dsa_proj_kernel(x_ref, w_ref, qn_ref, kn_ref, cq_ref, ckv_ref, kidx_ref, widx_ref):
    xb = x_ref[...].astype(BF16)

    def rms(v, g):
        return v * lax.rsqrt(jnp.mean(v * v, axis=-1, keepdims=True) + RMS_EPS) * g

    o1 = Q_LORA + KV_LORA
    cq = jnp.dot(xb, w_ref[:, 0:Q_LORA], preferred_element_type=F32)
    cq_ref[...] = rms(cq, qn_ref[...]).astype(BF16)
    ckv = jnp.dot(xb, w_ref[:, Q_LORA:o1], preferred_element_type=F32)
    ckv_ref[...] = rms(ckv, kn_ref[...]).astype(BF16)
    kidx_ref[...] = jnp.dot(xb, w_ref[:, o1:o1 + IDX_DIM], preferred_element_type=F32).astype(BF16)
    widx = jnp.dot(xb, w_ref[:, o1 + IDX_DIM:o1 + IDX_DIM + LANES], preferred_element_type=F32)
    widx_ref[...] = widx * (IDX_HEADS ** -0.5)


def _dsa_proj(h, w_in_pad, q_norm, kv_norm, *, tm=512):
    t = h.shape[0]
    npad = w_in_pad.shape[1]
    row = lambda w: pl.BlockSpec((tm, w), lambda i: (i, 0))
    return pl.pallas_call(
        _dsa_proj_kernel,
        out_shape=(jax.ShapeDtypeStruct((t, Q_LORA), BF16), jax.ShapeDtypeStruct((t, KV_LORA), BF16),
                   jax.ShapeDtypeStruct((t, IDX_DIM), BF16), jax.ShapeDtypeStruct((t, LANES), F32)),
        grid=(t // tm,),
        in_specs=[row(D_MODEL), pl.BlockSpec((D_MODEL, npad), lambda i: (0, 0)),
                  pl.BlockSpec((1, Q_LORA), lambda i: (0, 0)), pl.BlockSpec((1, KV_LORA), lambda i: (0, 0))],
        out_specs=(row(Q_LORA), row(KV_LORA), row(IDX_DIM), row(LANES)),
        compiler_params=_params("parallel"),
        name="dsa_proj",
    )(h, w_in_pad, q_norm.reshape(1, Q_LORA), kv_norm.reshape(1, KV_LORA))


def _indexer_kernel(q_ref, k_ref, w_ref, s_ref, thr_ref, key_scr, *, seq):
    jb = pl.program_id(1)
    k = k_ref[...]
    acc = jnp.zeros((Q_BLOCK, seq), F32)
    for hh in range(IDX_HEADS):
        qh = q_ref[:, hh * IDX_DIM:(hh + 1) * IDX_DIM]
        s = lax.dot_general(qh, k, (((1,), (1,)), ((), ())), preferred_element_type=F32)
        acc = acc + jnp.maximum(s * (IDX_DIM ** -0.5), 0.0) * w_ref[:, hh:hh + 1]
    q_pos = jb * Q_BLOCK + lax.broadcasted_iota(jnp.int32, (Q_BLOCK, seq), 0)
    k_pos = lax.broadcasted_iota(jnp.int32, (Q_BLOCK, seq), 1)
    acc = jnp.where(k_pos <= q_pos, acc, -jnp.inf)
    s_ref[...] = acc

    bits = lax.bitcast_convert_type(acc, jnp.int32)
    key_scr[...] = jnp.where(bits < 0, bits ^ jnp.int32(0x7FFFFFFF), bits)
    n_sel = float(min(IDX_TOPK, seq // 4))

    def count_ge(cand):
        return jnp.sum((key_scr[...] >= cand).astype(F32), axis=1, keepdims=True)

    int_min = jnp.int32(-2 ** 31)
    lo = jnp.where(count_ge(jnp.zeros((Q_BLOCK, 1), jnp.int32)) >= n_sel, jnp.int32(0), int_min)

    def body(i, lo):
        cand = lo + jnp.left_shift(jnp.int32(1), jnp.int32(30) - i)
        return jnp.where(count_ge(cand) >= n_sel, cand, lo)

    lo = lax.fori_loop(0, 31, body, lo)
    thr = lax.bitcast_convert_type(jnp.where(lo < 0, lo ^ jnp.int32(0x7FFFFFFF), lo), F32)
    thr_ref[...] = jnp.broadcast_to(thr, (Q_BLOCK, LANES))


def _indexer(qcat, kidx, widx, *, batch, seq):
    t = batch * seq
    n_blk = seq // Q_BLOCK
    return pl.pallas_call(
        functools.partial(_indexer_kernel, seq=seq),
        out_shape=(jax.ShapeDtypeStruct((t, seq), F32), jax.ShapeDtypeStruct((t, LANES), F32)),
        grid=(batch, n_blk),
        in_specs=[pl.BlockSpec((Q_BLOCK, IDX_HEADS * IDX_DIM), lambda b, j: (b * n_blk + j, 1)),
                  pl.BlockSpec((seq, IDX_DIM), lambda b, j: (b, 0)),
                  pl.BlockSpec((Q_BLOCK, LANES), lambda b, j: (b * n_blk + j, 0))],
        out_specs=(pl.BlockSpec((Q_BLOCK, seq), lambda b, j: (b * n_blk + j, 0)),
                   pl.BlockSpec((Q_BLOCK, LANES), lambda b, j: (b * n_blk + j, 0))),
        scratch_shapes=[pltpu.VMEM((Q_BLOCK, seq), jnp.int32)],
        compiler_params=_params("parallel", "parallel"),
        name="indexer",
    )(qcat, kidx, widx)


def _attn_kernel(q_ref, kv_ref, s_ref, thr_ref, wuk_ref, wuv_ref, bd_ref, bp_ref, bf_ref, o_ref,
                 ql_scr, m_scr, l_scr, acc_scr, s_scr, p_scr):
    jb = pl.program_id(1)
    kc = pl.program_id(2)
    last = (jb * Q_BLOCK) // KEY_CHUNK
    hd, qb = ATT_HEAD_DIM, Q_BLOCK
    sub = KEY_CHUNK // qb

    @pl.when(kc == 0)
    def _():
        for h in range(ATT_HEADS):
            ql = jnp.dot(q_ref[:, h * hd:(h + 1) * hd], wuk_ref[h], preferred_element_type=F32)
            ql_scr[h * qb:(h + 1) * qb, :] = ql.astype(BF16)
        m_scr[...] = jnp.full(m_scr.shape, MASKED, F32)
        l_scr[...] = jnp.zeros(l_scr.shape, F32)
        acc_scr[...] = jnp.zeros(acc_scr.shape, F32)

    @pl.when(kc <= last)
    def _():
        kv = kv_ref[...]
        s_scr[...] = lax.dot_general(ql_scr[...], kv, (((1,), (1,)), ((), ())),
                                     preferred_element_type=F32)
        selected = s_ref[...] >= thr_ref[:, 0:1]
        for h in range(ATT_HEADS):
            rows = slice(h * qb, (h + 1) * qb)
            tiles = []
            for c in range(sub):
                gblk = kc * sub + c
                far = jnp.where(gblk < jb, bf_ref[h], MASKED)
                tiles.append(jnp.where(gblk == jb, bd_ref[h],
                                       jnp.where(gblk == jb - 1, bp_ref[h], far)))
            bias = jnp.concatenate(tiles, axis=1)
            lg = jnp.where(selected, s_scr[rows, :] * (hd ** -0.5) + bias, MASKED)
            m_prev = m_scr[rows, 0:1]
            m_new = jnp.maximum(m_prev, jnp.max(lg, axis=1, keepdims=True))
            corr = jnp.exp(m_prev - m_new)
            pr = jnp.exp(lg - m_new)
            l_scr[rows, :] = jnp.broadcast_to(corr * l_scr[rows, 0:1] + jnp.sum(pr, axis=1, keepdims=True),
                                              (qb, LANES))
            m_scr[rows, :] = jnp.broadcast_to(m_new, (qb, LANES))
            p_scr[rows, :] = pr.astype(BF16)
            acc_scr[rows, :] = acc_scr[rows, :] * corr
        acc_scr[...] += jnp.dot(p_scr[...], kv, preferred_element_type=F32)

    @pl.when(kc == last)
    def _():
        for h in range(ATT_HEADS):
            rows = slice(h * qb, (h + 1) * qb)
            ol = (acc_scr[rows, :] / l_scr[rows, 0:1]).astype(BF16)
            o_ref[:, h * hd:(h + 1) * hd] = jnp.dot(ol, wuv_ref[h], preferred_element_type=F32).astype(o_ref.dtype)


def _rel_bucket_table():
    n = np.arange(2 * Q_BLOCK)
    max_exact = REL_BUCKETS // 2
    nf = np.maximum(n, 1).astype(np.float32)
    large = max_exact + (np.log(nf / np.float32(max_exact)) / np.float32(math.log(REL_MAX_DIST / max_exact))
                         * np.float32(REL_BUCKETS - max_exact)).astype(np.int32)
    large = np.minimum(large, REL_BUCKETS - 1)
    return np.where(n < max_exact, n, large)


def _attention(qcat, ckv, scores, thr, w_uk, w_uv, rel_bias, *, batch, seq):
    t = batch * seq
    n_blk = seq // Q_BLOCK
    n_kc = seq // KEY_CHUNK
    sub = KEY_CHUNK // Q_BLOCK
    bucket = _rel_bucket_table()
    i = np.arange(Q_BLOCK)[:, None]
    j = np.arange(Q_BLOCK)[None, :]
    rb = rel_bias.astype(F32)
    diag = jnp.where(jnp.asarray(j <= i)[None], jnp.moveaxis(rb[bucket[np.maximum(i - j, 0)]], -1, 0), MASKED)
    prev = jnp.moveaxis(rb[bucket[Q_BLOCK + i - j]], -1, 0)
    far = jnp.broadcast_to(rb[REL_BUCKETS - 1][:, None, None], (ATT_HEADS, 1, LANES))
    hq = ATT_HEADS * Q_BLOCK

    def kv_map(b, jj, kc):
        return (b * n_kc + jnp.minimum(kc, (jj * Q_BLOCK) // KEY_CHUNK), 0)

    def s_map(b, jj, kc):
        return (b * n_blk + jj, jnp.minimum(kc, (jj * Q_BLOCK) // KEY_CHUNK))

    qrow = lambda b, jj, kc: (b * n_blk + jj, 0)
    full3 = lambda b, jj, kc: (0, 0, 0)
    return pl.pallas_call(
        _attn_kernel,
        out_shape=jax.ShapeDtypeStruct((t, ATT_HEADS * ATT_HEAD_DIM), BF16),
        grid=(batch, n_blk, n_kc),
        in_specs=[pl.BlockSpec((Q_BLOCK, ATT_HEADS * ATT_HEAD_DIM), qrow),
                  pl.BlockSpec((KEY_CHUNK, KV_LORA), kv_map),
                  pl.BlockSpec((Q_BLOCK, KEY_CHUNK), s_map),
                  pl.BlockSpec((Q_BLOCK, LANES), qrow),
                  pl.BlockSpec((ATT_HEADS, ATT_HEAD_DIM, KV_LORA), full3),
                  pl.BlockSpec((ATT_HEADS, KV_LORA, ATT_HEAD_DIM), full3),
                  pl.BlockSpec((ATT_HEADS, Q_BLOCK, Q_BLOCK), full3),
                  pl.BlockSpec((ATT_HEADS, Q_BLOCK, Q_BLOCK), full3),
                  pl.BlockSpec((ATT_HEADS, 1, LANES), full3)],
        out_specs=pl.BlockSpec((Q_BLOCK, ATT_HEADS * ATT_HEAD_DIM), qrow),
        scratch_shapes=[pltpu.VMEM((hq, KV_LORA), BF16),
                        pltpu.VMEM((hq, LANES), F32),
                        pltpu.VMEM((hq, LANES), F32),
                        pltpu.VMEM((hq, KV_LORA), F32),
                        pltpu.VMEM((hq, KEY_CHUNK), F32),
                        pltpu.VMEM((hq, KEY_CHUNK), BF16)],
        compiler_params=_params("parallel", "parallel", "arbitrary"),
        name="latent_attention",
    )(qcat, ckv, scores, thr, w_uk, w_uv, diag, prev, far)


def _dsa_mixer(h, res_g, res_b, rel_bias, w_in_pad, q_norm, kv_norm, w_q, w_uk, w_uv, w_out, *, batch, seq):
    cq, ckv, kidx, widx = _dsa_proj(h, w_in_pad, q_norm, kv_norm)
    qcat = _linear(cq, w_q, tm=512, out_dtype=BF16)
    scores, thr = _indexer(qcat, kidx, widx, batch=batch, seq=seq)
    o = _attention(qcat, ckv, scores, thr, w_uk, w_uv, rel_bias, batch=batch, seq=seq)
    return _linear_ln(o, w_out, h, res_g, res_b, tm=256)


def _router_kernel(x_ref, w_ref, b_ref, route_ref, cnt_ref, cnt_scr):
    logits = jnp.dot(x_ref[...], w_ref[...], precision=lax.Precision.HIGHEST,
                     preferred_element_type=F32) + b_ref[...]
    lane = lax.broadcasted_iota(jnp.int32, logits.shape, 1)
    big = jnp.int32(LANES)
    is_group = (lane >= MOE_EXPERTS) & (lane < MOE_EXPERTS + MOE_GROUPS)
    gl = jnp.where(is_group, logits, -jnp.inf)
    gm = jnp.max(gl, axis=1, keepdims=True)
    g_p = 1.0 / jnp.sum(jnp.exp(gl - gm), axis=1, keepdims=True)
    g_idx = jnp.min(jnp.where(gl == gm, lane, big), axis=1, keepdims=True) - MOE_EXPERTS
    in_group = (lane < MOE_EXPERTS) & (jnp.right_shift(lane, 3) == g_idx)
    el = jnp.where(in_group, logits, -jnp.inf)
    em = jnp.max(el, axis=1, keepdims=True)
    ee = jnp.exp(el - em)
    prob = ee / jnp.sum(ee, axis=1, keepdims=True)
    p1 = jnp.max(prob, axis=1, keepdims=True)
    i1 = jnp.min(jnp.where(in_group & (prob == p1), lane, big), axis=1, keepdims=True)
    rest = in_group & (lane != i1)
    p2 = jnp.max(jnp.where(rest, prob, -1.0), axis=1, keepdims=True)
    i2 = jnp.min(jnp.where(rest & (prob == p2), lane, big), axis=1, keepdims=True)
    den = p1 + p2
    g1 = g_p * (p1 / den)
    g2 = g_p * (p2 / den)

    @pl.when(pl.program_id(0) == 0)
    def _():
        cnt_scr[...] = jnp.zeros(cnt_scr.shape, F32)

    tm = logits.shape[0]
    oh1 = (lane == i1).astype(BF16)
    oh2 = (lane == i2).astype(BF16)
    r_i = lax.broadcasted_iota(jnp.int32, (tm, tm), 0)
    c_i = lax.broadcasted_iota(jnp.int32, (tm, tm), 1)
    tri = (c_i < r_i).astype(BF16)
    pre1 = jnp.dot(tri, oh1, preferred_element_type=F32)
    pre2 = jnp.dot(tri, oh2, preferred_element_type=F32)
    tot1 = jnp.sum(oh1.astype(F32), axis=0, keepdims=True)
    tot2 = jnp.sum(oh2.astype(F32), axis=0, keepdims=True)
    base = cnt_scr[...]
    rank1 = jnp.sum(jnp.where(lane == i1, base + pre1, 0.0), axis=1, keepdims=True)
    rank2 = jnp.sum(jnp.where(lane == i2, base + tot1 + pre2, 0.0), axis=1, keepdims=True)
    cnt_scr[...] = base + tot1 + tot2
    cnt_ref[...] = jnp.broadcast_to(cnt_scr[...], cnt_ref.shape)

    out = jnp.where(lane == 0, i1.astype(F32), jnp.where(lane == 1, i2.astype(F32), 0.0))
    out = jnp.where(lane == 2, g1, jnp.where(lane == 3, g2, out))
    out = jnp.where(lane == 4, rank1, jnp.where(lane == 5, rank2, out))
    route_ref[...] = out


def _router(h, w_r, b_r, *, tm=512):
    t = h.shape[0]
    return pl.pallas_call(
        _router_kernel,
        out_shape=(jax.ShapeDtypeStruct((t, LANES), F32), jax.ShapeDtypeStruct((8, LANES), F32)),
        grid=(t // tm,),
        in_specs=[pl.BlockSpec((tm, D_MODEL), lambda i: (i, 0)),
                  pl.BlockSpec((D_MODEL, LANES), lambda i: (0, 0)),
                  pl.BlockSpec((1, LANES), lambda i: (0, 0))],
        out_specs=(pl.BlockSpec((tm, LANES), lambda i: (i, 0)),
                   pl.BlockSpec((8, LANES), lambda i: (0, 0))),
        scratch_shapes=[pltpu.VMEM((1, LANES), F32)],
        compiler_params=_params("arbitrary"),
        name="moe_router",
    )(h, w_r, b_r)


def _moe_expert_kernel(te_ref, nu_ref, rt_ref, x_hbm, wg_ref, wu_ref, wd_ref, y_ref,
                       xbuf, sem, wgb, wub, wdb):
    i = pl.program_id(0)
    n_used = nu_ref[0]
    slot = lax.rem(i, 2)
    tme = EXPERT_TILE

    def start_gather(tile, buf):
        def body(r, carry):
            tok = rt_ref[tile * tme + r]
            pltpu.make_async_copy(x_hbm.at[tok], xbuf.at[buf, r], sem.at[buf]).start()
            return carry
        lax.fori_loop(0, tme, body, 0, unroll=8)

    @pl.when(i == 0)
    def _():
        start_gather(0, 0)

    @pl.when(i + 1 < n_used)
    def _():
        start_gather(i + 1, 1 - slot)

    te = te_ref[i]
    prev = te_ref[jnp.maximum(i - 1, 0)]

    @pl.when((i == 0) | (te != prev))
    def _():
        wgb[...] = wg_ref[0].astype(BF16)
        wub[...] = wu_ref[0].astype(BF16)
        wdb[...] = wd_ref[0].astype(BF16)

    @pl.when(i < n_used)
    def _():
        pltpu.make_async_copy(x_hbm.at[pl.ds(0, tme)], xbuf.at[slot], sem.at[slot]).wait()
        xb = _load_row_tiles(xbuf.at[slot]).astype(BF16)
        gp = jnp.dot(xb, wgb[...], preferred_element_type=F32)
        up = jnp.dot(xb, wub[...], preferred_element_type=F32)
        hidden = (gp * jax.nn.sigmoid(gp)) * up
        _store_row_tiles(y_ref, jnp.dot(hidden.astype(BF16), wdb[...], preferred_element_type=F32))

    @pl.when(i >= n_used)
    def _():
        y_ref[...] = jnp.zeros(y_ref.shape, F32)


def _moe_experts(tile_expert, n_used, row_token, h3, w_gate, w_up, w_down):
    n_tiles = tile_expert.shape[0]
    tme = EXPERT_TILE
    ct = D_MODEL // LANES
    wspec = lambda a, b: pl.BlockSpec((1, a, b), lambda i, te, nu, rt: (te[i], 0, 0))
    return pl.pallas_call(
        _moe_expert_kernel,
        out_shape=jax.ShapeDtypeStruct((n_tiles * tme, ct, LANES), F32),
        grid_spec=pltpu.PrefetchScalarGridSpec(
            num_scalar_prefetch=3,
            grid=(n_tiles,),
            in_specs=[pl.BlockSpec(memory_space=pl.ANY),
                      wspec(D_MODEL, MOE_FF), wspec(D_MODEL, MOE_FF), wspec(MOE_FF, D_MODEL)],
            out_specs=pl.BlockSpec((tme, ct, LANES), lambda i, te, nu, rt: (i, 0, 0)),
            scratch_shapes=[pltpu.VMEM((2, tme, ct, LANES), F32),
                            pltpu.SemaphoreType.DMA((2,)),
                            pltpu.VMEM((D_MODEL, MOE_FF), BF16),
                            pltpu.VMEM((D_MODEL, MOE_FF), BF16),
                            pltpu.VMEM((MOE_FF, D_MODEL), BF16)]),
        compiler_params=_params("arbitrary"),
        name="moe_experts",
    )(tile_expert, n_used, row_token, h3, w_gate, w_up, w_down)


def _moe_combine_kernel(slot_ref, h_ref, route_ref, y_hbm, g_ref, b_ref, o_ref, ybuf, sem):
    i = pl.program_id(0)
    n = pl.num_programs(0)
    slot = lax.rem(i, 2)
    tm = h_ref.shape[0]

    def start_gather(tile, buf):
        def body(r, carry):
            for k in range(2):
                s = slot_ref[(tile * tm + r) * 2 + k]
                pltpu.make_async_copy(y_hbm.at[s], ybuf.at[buf, k * tm + r], sem.at[buf]).start()
            return carry
        lax.fori_loop(0, tm, body, 0, unroll=4)

    @pl.when(i == 0)
    def _():
        start_gather(0, 0)

    @pl.when(i + 1 < n)
    def _():
        start_gather(i + 1, 1 - slot)

    pltpu.make_async_copy(y_hbm.at[pl.ds(0, 2 * tm)], ybuf.at[slot], sem.at[slot]).wait()
    y1 = _load_row_tiles(ybuf.at[slot, pl.ds(0, tm)])
    y2 = _load_row_tiles(ybuf.at[slot, pl.ds(tm, tm)])
    ffn = route_ref[:, 2:3] * y1 + route_ref[:, 3:4] * y2
    o_ref[...] = _layer_norm_rows(DN_ALPHA * h_ref[...] + ffn, g_ref[...], b_ref[...])


def _moe_combine(slots, h, route, y3, ln_g, ln_b, *, tm):
    t = h.shape[0]
    ct = D_MODEL // LANES
    row = lambda w: pl.BlockSpec((tm, w), lambda i, s: (i, 0))
    vec = pl.BlockSpec((1, D_MODEL), lambda i, s: (0, 0))
    return pl.pallas_call(
        _moe_combine_kernel,
        out_shape=jax.ShapeDtypeStruct((t, D_MODEL), F32),
        grid_spec=pltpu.PrefetchScalarGridSpec(
            num_scalar_prefetch=1,
            grid=(t // tm,),
            in_specs=[row(D_MODEL), row(LANES), pl.BlockSpec(memory_space=pl.ANY), vec, vec],
            out_specs=row(D_MODEL),
            scratch_shapes=[pltpu.VMEM((2, 2 * tm, ct, LANES), F32),
                            pltpu.SemaphoreType.DMA((2,))]),
        compiler_params=_params("arbitrary"),
        name="moe_combine",
    )(slots, h, route, y3, ln_g.reshape(1, D_MODEL), ln_b.reshape(1, D_MODEL))


def _moe_plan(route, counts, n_tiles):
    t = route.shape[0]
    tme = EXPERT_TILE
    cnt = counts[0, :MOE_EXPERTS].astype(jnp.int32)
    tiles = (cnt + tme - 1) // tme
    tile_end = jnp.cumsum(tiles)
    tile_start = tile_end - tiles
    n_used = tile_end[-1]
    choice = route[:, 0:2].astype(jnp.int32)
    rank = route[:, 4:6].astype(jnp.int32)
    slots = (tile_start[choice] * tme + rank).reshape(-1)
    tile_ids = jnp.arange(n_tiles, dtype=jnp.int32)
    te = jnp.minimum(jnp.searchsorted(tile_end, tile_ids, side="right"), MOE_EXPERTS - 1).astype(jnp.int32)
    te = jnp.where(tile_ids < n_used, te, te[n_used - 1])
    row_token = jnp.zeros((n_tiles * tme,), jnp.int32).at[slots].set(jnp.arange(2 * t, dtype=jnp.int32) // 2)
    return te, n_used.reshape(1).astype(jnp.int32), row_token, slots.astype(jnp.int32)


def _hier_moe(h, h3, w_group, b_group, w_expert, b_expert, w_gate, w_up, w_down, ln_g, ln_b):
    t = h.shape[0]
    pad = LANES - MOE_EXPERTS - MOE_GROUPS
    w_r = jnp.concatenate([w_expert, w_group, jnp.zeros((D_MODEL, pad), F32)], axis=1)
    b_r = jnp.concatenate([b_expert, b_group, jnp.zeros((pad,), F32)]).reshape(1, LANES)
    route, counts = _router(h, w_r, b_r)
    n_tiles = (2 * t) // EXPERT_TILE + MOE_EXPERTS
    te, n_used, row_token, slots = _moe_plan(route, counts, n_tiles)
    y3 = _moe_experts(te, n_used, row_token, h3, w_gate, w_up, w_down)
    return _moe_combine(slots, h, route, y3, ln_g, ln_b, tm=COMBINE_TILE)


def kernel(x, rel_bias, s5_w_in, s5_a_re, s5_a_im, s5_log_dt, s5_b_re, s5_b_im, s5_c_re, s5_c_im, s5_d, s5_w_glu, s5_w_out, dsa_w_in, dsa_q_norm, dsa_kv_norm, dsa_w_uq, dsa_w_qidx, dsa_w_uk, dsa_w_uv, dsa_w_out, moe_w_group, moe_b_group, moe_w_expert, moe_b_expert, moe_w_gate, moe_w_up, moe_w_down, ln_mix_g, ln_mix_b, ln_ffn_g, ln_ffn_b):
    batch, seq, dm = x.shape
    h = x.reshape(batch * seq, dm)
    for i in range(DEPTH):
        j = i // 2
        if i % 2 == 0:
            h, h3 = _s5_mixer(h, ln_mix_g[i], ln_mix_b[i], s5_w_in[j].astype(BF16), s5_a_re[j], s5_a_im[j],
                          s5_log_dt[j], s5_b_re[j], s5_b_im[j], s5_c_re[j], s5_c_im[j], s5_d[j],
                          s5_w_glu[j].astype(BF16), s5_w_out[j].astype(BF16), batch=batch, seq=seq)
        else:
            n_in = dsa_w_in.shape[-1]
            n_pad = Q_LORA + KV_LORA + IDX_DIM + LANES
            w_in_pad = jnp.pad(dsa_w_in[j], ((0, 0), (0, n_pad - n_in))).astype(BF16)
            w_q = jnp.concatenate([dsa_w_uq[j], dsa_w_qidx[j]], axis=1).astype(BF16)
            h, h3 = _dsa_mixer(h, ln_mix_g[i], ln_mix_b[i], rel_bias, w_in_pad, dsa_q_norm[j], dsa_kv_norm[j], w_q,
                           dsa_w_uk[j].astype(BF16), dsa_w_uv[j].astype(BF16), dsa_w_out[j].astype(BF16),
                           batch=batch, seq=seq)
        h = _hier_moe(h, h3, moe_w_group[i], moe_b_group[i], moe_w_expert[i], moe_b_expert[i],
                      moe_w_gate[i], moe_w_up[i], moe_w_down[i], ln_ffn_g[i], ln_ffn_b[i])
    return h.reshape(batch, seq, dm)
```

```python
import functools
import math

import numpy as np
import jax
import jax.numpy as jnp
from jax import lax
from jax.experimental import pallas as pl
from jax.experimental.pallas import tpu as pltpu

F32 = jnp.float32
BF16 = jnp.bfloat16

D_MODEL = 2048
DEPTH = 4
S5_GROUP = 16
S5_GROUPS = D_MODEL // S5_GROUP
S5_STATE = 64
S5_CHUNK = 64
ATT_HEADS = 16
ATT_HEAD_DIM = 128
Q_LORA = 512
KV_LORA = 512
IDX_HEADS = 16
IDX_DIM = 128
IDX_TOPK = 256
Q_BLOCK = 128
KEY_CHUNK = 512
REL_BUCKETS = 32
REL_MAX_DIST = 128
MOE_GROUPS = 4
MOE_PER_GROUP = 8
MOE_EXPERTS = 32
MOE_FF = 256
EXPERT_TILE = 256
COMBINE_TILE = 256
DN_ALPHA = (2 * DEPTH) ** 0.25
LN_EPS = 1e-5
RMS_EPS = 1e-6

LANES = 128
MASKED = -1e30
VMEM_LIMIT = 56 * 1024 * 1024


def _params(*sem):
    return pltpu.CompilerParams(dimension_semantics=sem, vmem_limit_bytes=VMEM_LIMIT)


def _layer_norm_rows(y, g, b):
    mean = jnp.mean(y, axis=-1, keepdims=True)
    yc = y - mean
    var = jnp.mean(yc * yc, axis=-1, keepdims=True)
    return yc * lax.rsqrt(var + LN_EPS) * g + b


def _mm_plain_kernel(x_ref, w_ref, o_ref, *, tn):
    xb = x_ref[...].astype(BF16)
    for n0 in range(0, o_ref.shape[1], tn):
        acc = jnp.dot(xb, w_ref[:, n0:n0 + tn], preferred_element_type=F32)
        o_ref[:, n0:n0 + tn] = acc.astype(o_ref.dtype)


def _mm_glu_kernel(z_ref, w_ref, o_ref, *, tn):
    zb = z_ref[...]
    for n0 in range(0, o_ref.shape[1], tn):
        acc = jnp.dot(zb, w_ref[:, n0:n0 + tn], preferred_element_type=F32)
        zc = z_ref[:, n0:n0 + tn].astype(F32)
        o_ref[:, n0:n0 + tn] = (zc * jax.nn.sigmoid(acc)).astype(o_ref.dtype)


def _mm_ln_kernel(x_ref, w_ref, res_ref, g_ref, b_ref, o_ref, *, tn):
    xb = x_ref[...].astype(BF16)
    for n0 in range(0, o_ref.shape[1], tn):
        acc = jnp.dot(xb, w_ref[:, n0:n0 + tn], preferred_element_type=F32)
        o_ref[:, n0:n0 + tn] = DN_ALPHA * res_ref[:, n0:n0 + tn] + acc
    o_ref[...] = _layer_norm_rows(o_ref[...], g_ref[...], b_ref[...])


def _layer_spec(w_all, layer):
    shape = w_all.shape[1:]
    return pl.BlockSpec((None,) + shape, lambda *_: (layer,) + (0,) * len(shape))


def _linear(x, w_all, layer, *, tm, out_dtype, tn=512):
    m, k = x.shape
    n = w_all.shape[-1]
    return pl.pallas_call(
        functools.partial(_mm_plain_kernel, tn=tn),
        out_shape=jax.ShapeDtypeStruct((m, n), out_dtype),
        grid=(m // tm,),
        in_specs=[pl.BlockSpec((tm, k), lambda i: (i, 0)), _layer_spec(w_all, layer)],
        out_specs=pl.BlockSpec((tm, n), lambda i: (i, 0)),
        compiler_params=_params("parallel"),
        name="linear",
    )(x, w_all)


def _linear_glu(z, w_all, layer, *, tm, tn=512):
    m, k = z.shape
    return pl.pallas_call(
        functools.partial(_mm_glu_kernel, tn=tn),
        out_shape=jax.ShapeDtypeStruct((m, k), BF16),
        grid=(m // tm,),
        in_specs=[pl.BlockSpec((tm, k), lambda i: (i, 0)), _layer_spec(w_all, layer)],
        out_specs=pl.BlockSpec((tm, k), lambda i: (i, 0)),
        compiler_params=_params("parallel"),
        name="linear_glu",
    )(z, w_all)


def _linear_ln(x, w_all, layer, res, g, b, *, tm, tn=512):
    m, k = x.shape
    n = w_all.shape[-1]
    return pl.pallas_call(
        functools.partial(_mm_ln_kernel, tn=tn),
        out_shape=jax.ShapeDtypeStruct((m, n), F32),
        grid=(m // tm,),
        in_specs=[pl.BlockSpec((tm, k), lambda i: (i, 0)),
                  _layer_spec(w_all, layer),
                  pl.BlockSpec((tm, n), lambda i: (i, 0)),
                  pl.BlockSpec((1, n), lambda i: (0, 0)),
                  pl.BlockSpec((1, n), lambda i: (0, 0))],
        out_specs=pl.BlockSpec((tm, n), lambda i: (i, 0)),
        compiler_params=_params("parallel"),
        name="linear_ln",
    )(x, w_all, res, g.reshape(1, n), b.reshape(1, n))


def _gelu_tanh(x):
    return 0.5 * x * (1.0 + jnp.tanh(math.sqrt(2.0 / math.pi) * (x + 0.044715 * x * x * x)))


def _s5_kernel(u_ref, ar_ref, ai_ref, ldt_ref, br_ref, bi_ref, cr_ref, ci_ref, d_ref, o_ref, w_scr,
               *, batch, n_chunks):
    cs, p, n = S5_CHUNK, S5_GROUP, S5_STATE
    width = cs * p
    n2 = 2 * n

    @pl.when(pl.program_id(0) == 0)
    def _():
        w_scr[...] = jnp.zeros(w_scr.shape, w_scr.dtype)

    dt = jnp.exp(ldt_ref[0])
    lam_r = jnp.minimum(ar_ref[0], -1e-4)
    lam_i = ai_ref[0]
    lo_half = lax.broadcasted_iota(jnp.int32, (1, n2), 1) < n

    def a_pow(tau):
        mag = jnp.exp(lam_r * dt * tau)
        ph = lam_i * dt * tau
        return mag * jnp.cos(ph), mag * jnp.sin(ph)

    tau = lax.broadcasted_iota(jnp.int32, (cs, 1), 0).astype(F32)
    a0r, a0i = a_pow(tau)
    a1r, a1i = a_pow(tau + 1.0)
    avr, avi = a_pow((cs - 1.0) - tau)
    acr, aci = a_pow(jnp.full((1, 1), float(cs), F32))

    abr, abi = a1r[0:1], a1i[0:1]
    den = lam_r * lam_r + lam_i * lam_i
    kr = ((abr - 1.0) * lam_r + abi * lam_i) / den
    ki = (abi * lam_r - (abr - 1.0) * lam_i) / den
    b_r, b_i = br_ref[0], bi_ref[0]
    bbr = kr * b_r - ki * b_i
    bbi = kr * b_i + ki * b_r
    c_r, c_i = cr_ref[0], ci_ref[0]

    def rep_rows(x):
        return jnp.concatenate([jnp.broadcast_to(x[t:t + 1, :], (p, n2)) for t in range(cs)], axis=0)

    def tile_rows(x):
        return jnp.concatenate([x] * cs, axis=0)

    def sel(lo, hi):
        return jnp.where(lo_half, lo, hi)

    crr, cii = tile_rows(c_r), tile_rows(c_i)
    brr, bii = tile_rows(bbr), tile_rows(bbi)
    c_taps = crr * rep_rows(sel(a0r, -a0i)) - cii * rep_rows(sel(a0i, a0r))
    c_out = crr * rep_rows(sel(a1r, -a1i)) - cii * rep_rows(sel(a1i, a1r))
    b_in = brr * rep_rows(sel(avr, avi)) + bii * rep_rows(sel(-avi, avr))
    b_in_sw = brr * rep_rows(sel(avi, avr)) + bii * rep_rows(sel(avr, -avi))

    b_mat = sel(bbr, bbi)
    taps = lax.dot_general(b_mat, c_taps, (((1,), (1,)), ((), ())),
                           precision=lax.Precision.HIGHEST, preferred_element_type=F32)

    lane = lax.broadcasted_iota(jnp.int32, (p, width), 1)
    band = [taps]
    for j in range(1, LANES // p):
        band.append(jnp.where(lane >= p * j, pltpu.roll(taps, p * j, axis=1), 0.0))
    band = jnp.concatenate(band, axis=0).astype(BF16)
    for q in range(width // LANES):
        w_scr[q * LANES:(q + 1) * LANES, q * LANES:width] = band[:, 0:width - q * LANES]
    w_scr[:, width:width + n2] = b_in.astype(BF16)
    w_scr[:, width + n2:width + 2 * n2] = b_in_sw.astype(BF16)

    u = u_ref[0]
    res = jnp.dot(u, w_scr[...], preferred_element_type=F32)
    y = res[:, 0:width]
    s_in = res[:, width:width + n2]
    s_sw = res[:, width + n2:width + 2 * n2]

    m_rr = acr
    m_ni = sel(-aci, aci)
    m_pi = sel(aci, -aci)
    h = jnp.zeros((batch, n2), F32)
    h_sw = jnp.zeros((batch, n2), F32)
    starts = []
    for c in range(n_chunks):
        starts.append(h)
        rows = slice(c * batch, (c + 1) * batch)
        h, h_sw = (h * m_rr + h_sw * m_ni + s_in[rows], h_sw * m_rr + h * m_pi + s_sw[rows])
    h_start = jnp.concatenate(starts, axis=0).astype(BF16)
    y = y + lax.dot_general(h_start, c_out.astype(BF16), (((1,), (1,)), ((), ())),
                            preferred_element_type=F32)
    y = y + d_ref[0] * u.astype(F32)
    o_ref[0] = _gelu_tanh(y).astype(o_ref.dtype)


def _s5_core(u_t, a_re, a_im, log_dt, b_re, b_im, c_re, c_im, d, *, batch, n_chunks):
    g, rows, width = u_t.shape
    dup = lambda x: jnp.concatenate([x, x], axis=-1)
    a_re2 = dup(a_re).reshape(g, 1, 2 * S5_STATE)
    a_im2 = dup(a_im).reshape(g, 1, 2 * S5_STATE)
    ldt = log_dt.reshape(g, 1, 1)
    bt_re2 = dup(jnp.swapaxes(b_re, 1, 2))
    bt_im2 = dup(jnp.swapaxes(b_im, 1, 2))
    c_re2 = dup(c_re)
    c_im2 = dup(c_im)
    d_t = jnp.tile(d, (1, S5_CHUNK)).reshape(g, 1, width)
    vec = lambda w: pl.BlockSpec((1, 1, w), lambda i: (i, 0, 0))
    mat = pl.BlockSpec((1, S5_GROUP, 2 * S5_STATE), lambda i: (i, 0, 0))
    return pl.pallas_call(
        functools.partial(_s5_kernel, batch=batch, n_chunks=n_chunks),
        out_shape=jax.ShapeDtypeStruct((g, rows, width), BF16),
        grid=(g,),
        in_specs=[pl.BlockSpec((1, rows, width), lambda i: (i, 0, 0)),
                  vec(2 * S5_STATE), vec(2 * S5_STATE), vec(1), mat, mat, mat, mat, vec(width)],
        out_specs=pl.BlockSpec((1, rows, width), lambda i: (i, 0, 0)),
        scratch_shapes=[pltpu.VMEM((width, width + 4 * S5_STATE), BF16)],
        compiler_params=_params("arbitrary"),
        name="s5_core",
    )(u_t, a_re2, a_im2, ldt, bt_re2, bt_im2, c_re2, c_im2, d_t)


def _s5_mixer(h, res_g, res_b, layer, w_in, a_re, a_im, log_dt, b_re, b_im, c_re, c_im, d, w_glu, w_out,
              *, batch, seq):
    t = batch * seq
    nc = seq // S5_CHUNK
    u = _linear(h, w_in, layer, tm=512, out_dtype=BF16)
    u_t = u.reshape(batch, nc, S5_CHUNK, S5_GROUPS, S5_GROUP).transpose(3, 1, 0, 2, 4)
    u_t = u_t.reshape(S5_GROUPS, nc * batch, S5_CHUNK * S5_GROUP)
    z_t = _s5_core(u_t, a_re, a_im, log_dt, b_re, b_im, c_re, c_im, d, batch=batch, n_chunks=nc)
    z = z_t.reshape(S5_GROUPS, nc, batch, S5_CHUNK, S5_GROUP).transpose(2, 1, 3, 0, 4)
    z = z.reshape(t, D_MODEL)
    zg = _linear_glu(z, w_glu, layer, tm=512)
    return _linear_ln(zg, w_out, layer, h, res_g, res_b, tm=256)


def _dsa_proj_kernel(x_ref, w_ref, qn_ref, kn_ref, cq_ref, ckv_ref, kidx_ref, widx_ref):
    xb = x_ref[...].astype(BF16)

    def rms(v, g):
        return v * lax.rsqrt(jnp.mean(v * v, axis=-1, keepdims=True) + RMS_EPS) * g

    o1 = Q_LORA + KV_LORA
    cq = jnp.dot(xb, w_ref[:, 0:Q_LORA], preferred_element_type=F32)
    cq_ref[...] = rms(cq, qn_ref[...]).astype(BF16)
    ckv = jnp.dot(xb, w_ref[:, Q_LORA:o1], preferred_element_type=F32)
    ckv_ref[...] = rms(ckv, kn_ref[...]).astype(BF16)
    kidx_ref[...] = jnp.dot(xb, w_ref[:, o1:o1 + IDX_DIM], preferred_element_type=F32).astype(BF16)
    widx = jnp.dot(xb, w_ref[:, o1 + IDX_DIM:o1 + IDX_DIM + LANES], preferred_element_type=F32)
    widx_ref[...] = widx * (IDX_HEADS ** -0.5)


def _dsa_proj(h, w_in_pad, layer, q_norm, kv_norm, *, tm=512):
    t = h.shape[0]
    row = lambda w: pl.BlockSpec((tm, w), lambda i: (i, 0))
    return pl.pallas_call(
        _dsa_proj_kernel,
        out_shape=(jax.ShapeDtypeStruct((t, Q_LORA), BF16), jax.ShapeDtypeStruct((t, KV_LORA), BF16),
                   jax.ShapeDtypeStruct((t, IDX_DIM), BF16), jax.ShapeDtypeStruct((t, LANES), F32)),
        grid=(t // tm,),
        in_specs=[row(D_MODEL), _layer_spec(w_in_pad, layer),
                  pl.BlockSpec((1, Q_LORA), lambda i: (0, 0)), pl.BlockSpec((1, KV_LORA), lambda i: (0, 0))],
        out_specs=(row(Q_LORA), row(KV_LORA), row(IDX_DIM), row(LANES)),
        compiler_params=_params("parallel"),
        name="dsa_proj",
    )(h, w_in_pad, q_norm.reshape(1, Q_LORA), kv_norm.reshape(1, KV_LORA))


def _indexer_kernel(q_ref, k_ref, w_ref, s_ref, thr_ref, key_scr, *, seq):
    jb = pl.program_id(1)
    n_sel = float(min(IDX_TOPK, seq // 4))
    int_min = jnp.int32(-2 ** 31)
    flip = jnp.int32(0x7FFFFFFF)
    hg = 4
    chains = 4
    rows_c = Q_BLOCK // chains

    def run(width):
        k = k_ref[0:width, :]
        wcol = w_ref[...] * (IDX_DIM ** -0.5)
        acc = jnp.zeros((Q_BLOCK, width), F32)
        for g in range(IDX_HEADS // hg):
            qg = jnp.concatenate([q_ref[:, h * IDX_DIM:(h + 1) * IDX_DIM]
                                  for h in range(g * hg, (g + 1) * hg)], axis=0)
            s = lax.dot_general(qg, k, (((1,), (1,)), ((), ())), preferred_element_type=F32)
            for i in range(hg):
                h = g * hg + i
                acc = acc + jnp.maximum(s[i * Q_BLOCK:(i + 1) * Q_BLOCK], 0.0) * wcol[:, h:h + 1]
        q_pos = jb * Q_BLOCK + lax.broadcasted_iota(jnp.int32, (Q_BLOCK, width), 0)
        k_pos = lax.broadcasted_iota(jnp.int32, (Q_BLOCK, width), 1)
        acc = jnp.where(k_pos <= q_pos, acc, -jnp.inf)
        s_ref[:, 0:width] = acc
        if width < seq:
            s_ref[:, width:seq] = jnp.full((Q_BLOCK, seq - width), -jnp.inf, F32)

        bits = lax.bitcast_convert_type(acc, jnp.int32)
        key_scr[:, 0:width] = jnp.where(bits < 0, bits ^ flip, bits)

        def count_ge(c, cand):
            keys = key_scr[c * rows_c:(c + 1) * rows_c, 0:width]
            return jnp.sum((keys >= cand).astype(F32), axis=1, keepdims=True)

        zero = jnp.zeros((rows_c, 1), jnp.int32)
        los = tuple(jnp.where(count_ge(c, zero) >= n_sel, jnp.int32(0), int_min) for c in range(chains))

        def body(i, los):
            step = jnp.left_shift(jnp.int32(1), jnp.int32(30) - i)
            return tuple(jnp.where(count_ge(c, lo + step) >= n_sel, lo + step, lo) for c, lo in enumerate(los))

        los = lax.fori_loop(0, 31, body, los)
        lo = jnp.concatenate(los, axis=0)
        thr = lax.bitcast_convert_type(jnp.where(lo < 0, lo ^ flip, lo), F32)
        thr_ref[...] = jnp.broadcast_to(thr, (Q_BLOCK, LANES))

    n_cls = seq // KEY_CHUNK
    per_cls = KEY_CHUNK // Q_BLOCK
    for cls in range(n_cls):
        pl.when(jb // per_cls == cls)(functools.partial(run, KEY_CHUNK * (cls + 1)))


def _indexer(qcat, kidx, widx, *, batch, seq):
    t = batch * seq
    n_blk = seq // Q_BLOCK
    return pl.pallas_call(
        functools.partial(_indexer_kernel, seq=seq),
        out_shape=(jax.ShapeDtypeStruct((t, seq), F32), jax.ShapeDtypeStruct((t, LANES), F32)),
        grid=(batch, n_blk),
        in_specs=[pl.BlockSpec((Q_BLOCK, IDX_HEADS * IDX_DIM), lambda b, j: (b * n_blk + j, 1)),
                  pl.BlockSpec((seq, IDX_DIM), lambda b, j: (b, 0)),
                  pl.BlockSpec((Q_BLOCK, LANES), lambda b, j: (b * n_blk + j, 0))],
        out_specs=(pl.BlockSpec((Q_BLOCK, seq), lambda b, j: (b * n_blk + j, 0)),
                   pl.BlockSpec((Q_BLOCK, LANES), lambda b, j: (b * n_blk + j, 0))),
        scratch_shapes=[pltpu.VMEM((Q_BLOCK, seq), jnp.int32)],
        compiler_params=_params("parallel", "parallel"),
        name="indexer",
    )(qcat, kidx, widx)


def _attn_kernel(q_ref, kv_ref, s_ref, thr_ref, wuk_ref, wuv_ref, bd_ref, bp_ref, o_ref,
                 ql_scr, m_scr, l_scr, acc_scr):
    jb = pl.program_id(1)
    kc = pl.program_id(2)
    last = (jb * Q_BLOCK) // KEY_CHUNK
    hd, qb = ATT_HEAD_DIM, Q_BLOCK
    sub = KEY_CHUNK // qb
    hg = 4
    rows_g = hg * qb

    @pl.when(kc == 0)
    def _():
        for h in range(ATT_HEADS):
            ql = jnp.dot(q_ref[:, h * hd:(h + 1) * hd], wuk_ref[h], preferred_element_type=F32)
            ql_scr[h * qb:(h + 1) * qb, :] = (ql * (hd ** -0.5)).astype(BF16)
        m_scr[...] = jnp.full(m_scr.shape, MASKED, F32)
        l_scr[...] = jnp.zeros(l_scr.shape, F32)
        acc_scr[...] = jnp.zeros(acc_scr.shape, F32)

    def attend(near):
        kv = kv_ref[...]
        q_pos = jb * qb + lax.broadcasted_iota(jnp.int32, (qb, KEY_CHUNK), 0)
        k_pos = kc * KEY_CHUNK + lax.broadcasted_iota(jnp.int32, (qb, KEY_CHUNK), 1)
        keep = (s_ref[...] >= thr_ref[:, 0:1]) & (k_pos <= q_pos)
        mask_add = jnp.where(keep, 0.0, MASKED)
        for g in range(ATT_HEADS // hg):
            rg = slice(g * rows_g, (g + 1) * rows_g)
            s = lax.dot_general(ql_scr[rg, :], kv, (((1,), (1,)), ((), ())), preferred_element_type=F32)
            probs, corrs = [], []
            for i in range(hg):
                h = g * hg + i
                rows = slice(h * qb, (h + 1) * qb)
                lg = s[i * qb:(i + 1) * qb] + mask_add
                if near:
                    tiles = []
                    for c in range(sub):
                        gblk = kc * sub + c
                        tiles.append(jnp.where(gblk == jb, bd_ref[h], jnp.where(gblk == jb - 1, bp_ref[h], 0.0)))
                    lg = lg + jnp.concatenate(tiles, axis=1)
                m_prev = m_scr[rows, 0:1]
                m_new = jnp.maximum(m_prev, jnp.max(lg, axis=1, keepdims=True))
                corr = jnp.exp(m_prev - m_new)
                pr = jnp.exp(lg - m_new)
                l_scr[rows, :] = jnp.broadcast_to(
                    corr * l_scr[rows, 0:1] + jnp.sum(pr, axis=1, keepdims=True), (qb, LANES))
                m_scr[rows, :] = jnp.broadcast_to(m_new, (qb, LANES))
                probs.append(pr.astype(BF16))
                corrs.append(corr)
            pv = jnp.dot(jnp.concatenate(probs, axis=0), kv, preferred_element_type=F32)
            acc_scr[rg, :] = acc_scr[rg, :] * jnp.concatenate(corrs, axis=0) + pv

    near = kc * sub + sub >= jb
    pl.when((kc <= last) & near)(functools.partial(attend, True))
    pl.when((kc <= last) & jnp.logical_not(near))(functools.partial(attend, False))

    @pl.when(kc == last)
    def _():
        for h in range(ATT_HEADS):
            rows = slice(h * qb, (h + 1) * qb)
            ol = (acc_scr[rows, :] / l_scr[rows, 0:1]).astype(BF16)
            o_ref[:, h * hd:(h + 1) * hd] = jnp.dot(ol, wuv_ref[h], preferred_element_type=F32).astype(o_ref.dtype)


def _rel_bucket_table():
    n = np.arange(2 * Q_BLOCK)
    max_exact = REL_BUCKETS // 2
    nf = np.maximum(n, 1).astype(np.float32)
    large = max_exact + (np.log(nf / np.float32(max_exact)) / np.float32(math.log(REL_MAX_DIST / max_exact))
                         * np.float32(REL_BUCKETS - max_exact)).astype(np.int32)
    large = np.minimum(large, REL_BUCKETS - 1)
    return np.where(n < max_exact, n, large)


def _attention(qcat, ckv, scores, thr, w_uk, w_uv, layer, rel_bias, *, batch, seq):
    t = batch * seq
    n_blk = seq // Q_BLOCK
    n_kc = seq // KEY_CHUNK
    sub = KEY_CHUNK // Q_BLOCK
    bucket = _rel_bucket_table()
    assert bucket[Q_BLOCK + 1:].min() == REL_BUCKETS - 1
    i = np.arange(Q_BLOCK)[:, None]
    j = np.arange(Q_BLOCK)[None, :]
    rb = rel_bias.astype(F32)
    rb = rb - rb[REL_BUCKETS - 1]
    diag = jnp.where(jnp.asarray(j <= i)[None], jnp.moveaxis(rb[bucket[np.maximum(i - j, 0)]], -1, 0), 0.0)
    prev = jnp.moveaxis(rb[bucket[Q_BLOCK + i - j]], -1, 0)
    hq = ATT_HEADS * Q_BLOCK

    def kv_map(b, jj, kc):
        return (b * n_kc + jnp.minimum(kc, (jj * Q_BLOCK) // KEY_CHUNK), 0)

    def s_map(b, jj, kc):
        return (b * n_blk + jj, jnp.minimum(kc, (jj * Q_BLOCK) // KEY_CHUNK))

    qrow = lambda b, jj, kc: (b * n_blk + jj, 0)
    full3 = lambda b, jj, kc: (0, 0, 0)
    return pl.pallas_call(
        _attn_kernel,
        out_shape=jax.ShapeDtypeStruct((t, ATT_HEADS * ATT_HEAD_DIM), BF16),
        grid=(batch, n_blk, n_kc),
        in_specs=[pl.BlockSpec((Q_BLOCK, ATT_HEADS * ATT_HEAD_DIM), qrow),
                  pl.BlockSpec((KEY_CHUNK, KV_LORA), kv_map),
                  pl.BlockSpec((Q_BLOCK, KEY_CHUNK), s_map),
                  pl.BlockSpec((Q_BLOCK, LANES), qrow),
                  _layer_spec(w_uk, layer),
                  _layer_spec(w_uv, layer),
                  pl.BlockSpec((ATT_HEADS, Q_BLOCK, Q_BLOCK), full3),
                  pl.BlockSpec((ATT_HEADS, Q_BLOCK, Q_BLOCK), full3)],
        out_specs=pl.BlockSpec((Q_BLOCK, ATT_HEADS * ATT_HEAD_DIM), qrow),
        scratch_shapes=[pltpu.VMEM((hq, KV_LORA), BF16),
                        pltpu.VMEM((hq, LANES), F32),
                        pltpu.VMEM((hq, LANES), F32),
                        pltpu.VMEM((hq, KV_LORA), F32)],
        compiler_params=_params("parallel", "parallel", "arbitrary"),
        name="latent_attention",
    )(qcat, ckv, scores, thr, w_uk, w_uv, diag, prev)


def _dsa_mixer(h, res_g, res_b, layer, rel_bias, w_in_pad, q_norm, kv_norm, w_q, w_uk, w_uv, w_out,
               *, batch, seq):
    cq, ckv, kidx, widx = _dsa_proj(h, w_in_pad, layer, q_norm, kv_norm)
    qcat = _linear(cq, w_q, layer, tm=512, out_dtype=BF16)
    scores, thr = _indexer(qcat, kidx, widx, batch=batch, seq=seq)
    o = _attention(qcat, ckv, scores, thr, w_uk, w_uv, layer, rel_bias, batch=batch, seq=seq)
    return _linear_ln(o, w_out, layer, h, res_g, res_b, tm=256)


def _router_kernel(x_ref, w_ref, b_ref, route_ref, cnt_ref, cnt_scr):
    logits = jnp.dot(x_ref[...], w_ref[...], precision=lax.Precision.HIGHEST,
                     preferred_element_type=F32) + b_ref[...]
    lane = lax.broadcasted_iota(jnp.int32, logits.shape, 1)
    big = jnp.int32(LANES)
    is_group = (lane >= MOE_EXPERTS) & (lane < MOE_EXPERTS + MOE_GROUPS)
    gl = jnp.where(is_group, logits, -jnp.inf)
    gm = jnp.max(gl, axis=1, keepdims=True)
    g_p = 1.0 / jnp.sum(jnp.exp(gl - gm), axis=1, keepdims=True)
    g_idx = jnp.min(jnp.where(gl == gm, lane, big), axis=1, keepdims=True) - MOE_EXPERTS
    in_group = (lane < MOE_EXPERTS) & (jnp.right_shift(lane, 3) == g_idx)
    el = jnp.where(in_group, logits, -jnp.inf)
    em = jnp.max(el, axis=1, keepdims=True)
    ee = jnp.exp(el - em)
    prob = ee / jnp.sum(ee, axis=1, keepdims=True)
    p1 = jnp.max(prob, axis=1, keepdims=True)
    i1 = jnp.min(jnp.where(in_group & (prob == p1), lane, big), axis=1, keepdims=True)
    rest = in_group & (lane != i1)
    p2 = jnp.max(jnp.where(rest, prob, -1.0), axis=1, keepdims=True)
    i2 = jnp.min(jnp.where(rest & (prob == p2), lane, big), axis=1, keepdims=True)
    den = p1 + p2
    g1 = g_p * (p1 / den)
    g2 = g_p * (p2 / den)

    @pl.when(pl.program_id(0) == 0)
    def _():
        cnt_scr[...] = jnp.zeros(cnt_scr.shape, F32)

    tm = logits.shape[0]
    oh1 = (lane == i1).astype(BF16)
    oh2 = (lane == i2).astype(BF16)
    r_i = lax.broadcasted_iota(jnp.int32, (tm, tm), 0)
    c_i = lax.broadcasted_iota(jnp.int32, (tm, tm), 1)
    tri = (c_i < r_i).astype(BF16)
    pre1 = jnp.dot(tri, oh1, preferred_element_type=F32)
    pre2 = jnp.dot(tri, oh2, preferred_element_type=F32)
    tot1 = jnp.sum(oh1.astype(F32), axis=0, keepdims=True)
    tot2 = jnp.sum(oh2.astype(F32), axis=0, keepdims=True)
    base = cnt_scr[...]
    rank1 = jnp.sum(jnp.where(lane == i1, base + pre1, 0.0), axis=1, keepdims=True)
    rank2 = jnp.sum(jnp.where(lane == i2, base + tot1 + pre2, 0.0), axis=1, keepdims=True)
    cnt_scr[...] = base + tot1 + tot2
    cnt_ref[...] = jnp.broadcast_to(cnt_scr[...], cnt_ref.shape)

    out = jnp.where(lane == 0, i1.astype(F32), jnp.where(lane == 1, i2.astype(F32), 0.0))
    out = jnp.where(lane == 2, g1, jnp.where(lane == 3, g2, out))
    out = jnp.where(lane == 4, rank1, jnp.where(lane == 5, rank2, out))
    route_ref[...] = out


def _router(h, w_r, b_r, *, tm=512):
    t = h.shape[0]
    return pl.pallas_call(
        _router_kernel,
        out_shape=(jax.ShapeDtypeStruct((t, LANES), F32), jax.ShapeDtypeStruct((8, LANES), F32)),
        grid=(t // tm,),
        in_specs=[pl.BlockSpec((tm, D_MODEL), lambda i: (i, 0)),
                  pl.BlockSpec((D_MODEL, LANES), lambda i: (0, 0)),
                  pl.BlockSpec((1, LANES), lambda i: (0, 0))],
        out_specs=(pl.BlockSpec((tm, LANES), lambda i: (i, 0)),
                   pl.BlockSpec((8, LANES), lambda i: (0, 0))),
        scratch_shapes=[pltpu.VMEM((1, LANES), F32)],
        compiler_params=_params("arbitrary"),
        name="moe_router",
    )(h, w_r, b_r)


def _moe_expert_kernel(te_ref, nu_ref, rt_ref, x_hbm, wg_ref, wu_ref, wd_ref, y_ref,
                       xbuf, sem, wgb, wub, wdb):
    i = pl.program_id(0)
    n_used = nu_ref[0]
    slot = lax.rem(i, 2)
    tme = EXPERT_TILE

    def start_gather(tile, buf):
        def body(r, carry):
            tok = rt_ref[tile * tme + r]
            pltpu.make_async_copy(x_hbm.at[pl.ds(tok, 1)], xbuf.at[buf, pl.ds(r, 1)], sem.at[buf]).start()
            return carry
        lax.fori_loop(0, tme, body, 0, unroll=8)

    @pl.when(i == 0)
    def _():
        start_gather(0, 0)

    @pl.when(i + 1 < n_used)
    def _():
        start_gather(i + 1, 1 - slot)

    te = te_ref[i]
    prev = te_ref[jnp.maximum(i - 1, 0)]

    @pl.when((i == 0) | (te != prev))
    def _():
        wgb[...] = wg_ref[...].astype(BF16)
        wub[...] = wu_ref[...].astype(BF16)
        wdb[...] = wd_ref[...].astype(BF16)

    @pl.when(i < n_used)
    def _():
        pltpu.make_async_copy(x_hbm.at[pl.ds(0, tme)], xbuf.at[slot], sem.at[slot]).wait()
        xb = xbuf[slot].astype(BF16)
        gp = jnp.dot(xb, wgb[...], preferred_element_type=F32)
        up = jnp.dot(xb, wub[...], preferred_element_type=F32)
        hidden = (gp * jax.nn.sigmoid(gp)) * up
        y_ref[...] = jnp.dot(hidden.astype(BF16), wdb[...], preferred_element_type=F32)

    @pl.when(i >= n_used)
    def _():
        y_ref[...] = jnp.zeros(y_ref.shape, F32)


def _moe_experts(tile_expert, n_used, row_token, h, w_gate, w_up, w_down, layer):
    n_tiles = tile_expert.shape[0]
    tme = EXPERT_TILE
    wspec = lambda a, b: pl.BlockSpec((None, None, a, b), lambda i, te, nu, rt: (layer, te[i], 0, 0))
    return pl.pallas_call(
        _moe_expert_kernel,
        out_shape=jax.ShapeDtypeStruct((n_tiles * tme, D_MODEL), F32),
        grid_spec=pltpu.PrefetchScalarGridSpec(
            num_scalar_prefetch=3,
            grid=(n_tiles,),
            in_specs=[pl.BlockSpec(memory_space=pl.ANY),
                      wspec(D_MODEL, MOE_FF), wspec(D_MODEL, MOE_FF), wspec(MOE_FF, D_MODEL)],
            out_specs=pl.BlockSpec((tme, D_MODEL), lambda i, te, nu, rt: (i, 0)),
            scratch_shapes=[pltpu.VMEM((2, tme, D_MODEL), F32),
                            pltpu.SemaphoreType.DMA((2,)),
                            pltpu.VMEM((D_MODEL, MOE_FF), BF16),
                            pltpu.VMEM((D_MODEL, MOE_FF), BF16),
                            pltpu.VMEM((MOE_FF, D_MODEL), BF16)]),
        compiler_params=_params("arbitrary"),
        name="moe_experts",
    )(tile_expert, n_used, row_token, h, w_gate, w_up, w_down)


def _moe_combine_kernel(slot_ref, h_ref, route_ref, y_hbm, g_ref, b_ref, o_ref, ybuf, sem):
    i = pl.program_id(0)
    n = pl.num_programs(0)
    slot = lax.rem(i, 2)
    tm = h_ref.shape[0]

    def start_gather(tile, buf):
        def body(r, carry):
            for k in range(2):
                s = slot_ref[(tile * tm + r) * 2 + k]
                pltpu.make_async_copy(y_hbm.at[pl.ds(s, 1)], ybuf.at[buf, pl.ds(k * tm + r, 1)],
                                      sem.at[buf]).start()
            return carry
        lax.fori_loop(0, tm, body, 0, unroll=4)

    @pl.when(i == 0)
    def _():
        start_gather(0, 0)

    @pl.when(i + 1 < n)
    def _():
        start_gather(i + 1, 1 - slot)

    pltpu.make_async_copy(y_hbm.at[pl.ds(0, 2 * tm)], ybuf.at[slot], sem.at[slot]).wait()
    ffn = route_ref[:, 2:3] * ybuf[slot, pl.ds(0, tm)] + route_ref[:, 3:4] * ybuf[slot, pl.ds(tm, tm)]
    o_ref[...] = _layer_norm_rows(DN_ALPHA * h_ref[...] + ffn, g_ref[...], b_ref[...])


def _moe_combine(slots, h, route, y, ln_g, ln_b, *, tm):
    t = h.shape[0]
    row = lambda w: pl.BlockSpec((tm, w), lambda i, s: (i, 0))
    vec = pl.BlockSpec((1, D_MODEL), lambda i, s: (0, 0))
    return pl.pallas_call(
        _moe_combine_kernel,
        out_shape=jax.ShapeDtypeStruct((t, D_MODEL), F32),
        grid_spec=pltpu.PrefetchScalarGridSpec(
            num_scalar_prefetch=1,
            grid=(t // tm,),
            in_specs=[row(D_MODEL), row(LANES), pl.BlockSpec(memory_space=pl.ANY), vec, vec],
            out_specs=row(D_MODEL),
            scratch_shapes=[pltpu.VMEM((2, 2 * tm, D_MODEL), F32),
                            pltpu.SemaphoreType.DMA((2,))]),
        compiler_params=_params("arbitrary"),
        name="moe_combine",
    )(slots, h, route, y, ln_g.reshape(1, D_MODEL), ln_b.reshape(1, D_MODEL))


def _moe_plan(route, counts, n_tiles):
    t = route.shape[0]
    tme = EXPERT_TILE
    cnt = counts[0, :MOE_EXPERTS].astype(jnp.int32)
    tiles = (cnt + tme - 1) // tme
    tile_end = jnp.cumsum(tiles)
    tile_start = tile_end - tiles
    n_used = tile_end[-1]
    choice = route[:, 0:2].astype(jnp.int32)
    rank = route[:, 4:6].astype(jnp.int32)
    slots = (tile_start[choice] * tme + rank).reshape(-1)
    tile_ids = jnp.arange(n_tiles, dtype=jnp.int32)
    te = jnp.sum((tile_ids[:, None] >= tile_end[None, :]).astype(jnp.int32), axis=1)
    last_e = jnp.sum((n_used - 1 >= tile_end).astype(jnp.int32))
    te = jnp.where(tile_ids < n_used, te, last_e)
    row_token = jnp.zeros((n_tiles * tme,), jnp.int32).at[slots].set(jnp.arange(2 * t, dtype=jnp.int32) // 2)
    return te, n_used.reshape(1).astype(jnp.int32), row_token, slots.astype(jnp.int32)


def _hier_moe(h, layer, w_group, b_group, w_expert, b_expert, w_gate, w_up, w_down, ln_g, ln_b):
    t = h.shape[0]
    pad = LANES - MOE_EXPERTS - MOE_GROUPS
    w_r = jnp.concatenate([w_expert, w_group, jnp.zeros((D_MODEL, pad), F32)], axis=1)
    b_r = jnp.concatenate([b_expert, b_group, jnp.zeros((pad,), F32)]).reshape(1, LANES)
    route, counts = _router(h, w_r, b_r)
    n_tiles = (2 * t) // EXPERT_TILE + MOE_EXPERTS
    te, n_used, row_token, slots = _moe_plan(route, counts, n_tiles)
    y = _moe_experts(te, n_used, row_token, h, w_gate, w_up, w_down, layer)
    return _moe_combine(slots, h, route, y, ln_g, ln_b, tm=COMBINE_TILE)


def kernel(x, rel_bias, s5_w_in, s5_a_re, s5_a_im, s5_log_dt, s5_b_re, s5_b_im, s5_c_re, s5_c_im, s5_d, s5_w_glu, s5_w_out, dsa_w_in, dsa_q_norm, dsa_kv_norm, dsa_w_uq, dsa_w_qidx, dsa_w_uk, dsa_w_uv, dsa_w_out, moe_w_group, moe_b_group, moe_w_expert, moe_b_expert, moe_w_gate, moe_w_up, moe_w_down, ln_mix_g, ln_mix_b, ln_ffn_g, ln_ffn_b):
    batch, seq, dm = x.shape
    h = x.reshape(batch * seq, dm)
    s5_w_in, s5_w_glu, s5_w_out = (w.astype(BF16) for w in (s5_w_in, s5_w_glu, s5_w_out))
    n_pad = Q_LORA + KV_LORA + IDX_DIM + LANES - dsa_w_in.shape[-1]
    dsa_w_in_pad = jnp.pad(dsa_w_in, ((0, 0), (0, 0), (0, n_pad))).astype(BF16)
    dsa_w_q = jnp.concatenate([dsa_w_uq, dsa_w_qidx], axis=-1).astype(BF16)
    dsa_w_uk, dsa_w_uv, dsa_w_out = (w.astype(BF16) for w in (dsa_w_uk, dsa_w_uv, dsa_w_out))
    for i in range(DEPTH):
        j = i // 2
        if i % 2 == 0:
            h = _s5_mixer(h, ln_mix_g[i], ln_mix_b[i], j, s5_w_in, s5_a_re[j], s5_a_im[j], s5_log_dt[j],
                          s5_b_re[j], s5_b_im[j], s5_c_re[j], s5_c_im[j], s5_d[j], s5_w_glu, s5_w_out,
                          batch=batch, seq=seq)
        else:
            h = _dsa_mixer(h, ln_mix_g[i], ln_mix_b[i], j, rel_bias, dsa_w_in_pad, dsa_q_norm[j],
                           dsa_kv_norm[j], dsa_w_q, dsa_w_uk, dsa_w_uv, dsa_w_out, batch=batch, seq=seq)
        h = _hier_moe(h, i, moe_w_group[i], moe_b_group[i], moe_w_expert[i], moe_b_expert[i],
                      moe_w_gate, moe_w_up, moe_w_down, ln_ffn_g[i], ln_ffn_b[i])
    return h.reshape(batch, seq, dm)
```

```python
import functools
import math

import numpy as np
import jax
import jax.numpy as jnp
from jax import lax
from jax.experimental import pallas as pl
from jax.experimental.pallas import tpu as pltpu

F32 = jnp.float32
BF16 = jnp.bfloat16

D_MODEL = 2048
DEPTH = 4
S5_GROUP = 16
S5_GROUPS = D_MODEL // S5_GROUP
S5_STATE = 64
S5_CHUNK = 64
ATT_HEADS = 16
ATT_HEAD_DIM = 128
Q_LORA = 512
KV_LORA = 512
IDX_HEADS = 16
IDX_DIM = 128
IDX_TOPK = 256
Q_BLOCK = 128
KEY_CHUNK = 512
REL_BUCKETS = 32
REL_MAX_DIST = 128
MOE_GROUPS = 4
MOE_PER_GROUP = 8
MOE_EXPERTS = 32
MOE_FF = 256
EXPERT_TILE = 256
COMBINE_TILE = 256
DN_ALPHA = (2 * DEPTH) ** 0.25
LN_EPS = 1e-5
RMS_EPS = 1e-6

LANES = 128
MASKED = -1e30
VMEM_LIMIT = 56 * 1024 * 1024


def _params(*sem):
    return pltpu.CompilerParams(dimension_semantics=sem, vmem_limit_bytes=VMEM_LIMIT)


def _layer_norm_rows(y, g, b):
    mean = jnp.mean(y, axis=-1, keepdims=True)
    yc = y - mean
    var = jnp.mean(yc * yc, axis=-1, keepdims=True)
    return yc * lax.rsqrt(var + LN_EPS) * g + b


def _mm_plain_kernel(x_ref, w_ref, o_ref, *, tn):
    xb = x_ref[...].astype(BF16)
    for n0 in range(0, o_ref.shape[1], tn):
        acc = jnp.dot(xb, w_ref[:, n0:n0 + tn], preferred_element_type=F32)
        o_ref[:, n0:n0 + tn] = acc.astype(o_ref.dtype)


def _mm_glu_kernel(z_ref, w_ref, o_ref, *, tn):
    zb = z_ref[...]
    for n0 in range(0, o_ref.shape[1], tn):
        acc = jnp.dot(zb, w_ref[:, n0:n0 + tn], preferred_element_type=F32)
        zc = z_ref[:, n0:n0 + tn].astype(F32)
        o_ref[:, n0:n0 + tn] = (zc * jax.nn.sigmoid(acc)).astype(o_ref.dtype)


def _mm_ln_kernel(x_ref, w_ref, res_ref, g_ref, b_ref, o_ref, *, tn):
    xb = x_ref[...].astype(BF16)
    for n0 in range(0, o_ref.shape[1], tn):
        acc = jnp.dot(xb, w_ref[:, n0:n0 + tn], preferred_element_type=F32)
        o_ref[:, n0:n0 + tn] = DN_ALPHA * res_ref[:, n0:n0 + tn] + acc
    o_ref[...] = _layer_norm_rows(o_ref[...], g_ref[...], b_ref[...])


def _layer_spec(w_all, layer):
    shape = w_all.shape[1:]
    return pl.BlockSpec((None,) + shape, lambda *_: (layer,) + (0,) * len(shape))


def _linear(x, w_all, layer, *, tm, out_dtype, tn=512):
    m, k = x.shape
    n = w_all.shape[-1]
    return pl.pallas_call(
        functools.partial(_mm_plain_kernel, tn=tn),
        out_shape=jax.ShapeDtypeStruct((m, n), out_dtype),
        grid=(m // tm,),
        in_specs=[pl.BlockSpec((tm, k), lambda i: (i, 0)), _layer_spec(w_all, layer)],
        out_specs=pl.BlockSpec((tm, n), lambda i: (i, 0)),
        compiler_params=_params("parallel"),
        name="linear",
    )(x, w_all)


def _linear_glu(z, w_all, layer, *, tm, tn=512):
    m, k = z.shape
    return pl.pallas_call(
        functools.partial(_mm_glu_kernel, tn=tn),
        out_shape=jax.ShapeDtypeStruct((m, k), BF16),
        grid=(m // tm,),
        in_specs=[pl.BlockSpec((tm, k), lambda i: (i, 0)), _layer_spec(w_all, layer)],
        out_specs=pl.BlockSpec((tm, k), lambda i: (i, 0)),
        compiler_params=_params("parallel"),
        name="linear_glu",
    )(z, w_all)


def _linear_ln(x, w_all, layer, res, g, b, *, tm, tn=512):
    m, k = x.shape
    n = w_all.shape[-1]
    return pl.pallas_call(
        functools.partial(_mm_ln_kernel, tn=tn),
        out_shape=jax.ShapeDtypeStruct((m, n), F32),
        grid=(m // tm,),
        in_specs=[pl.BlockSpec((tm, k), lambda i: (i, 0)),
                  _layer_spec(w_all, layer),
                  pl.BlockSpec((tm, n), lambda i: (i, 0)),
                  pl.BlockSpec((1, n), lambda i: (0, 0)),
                  pl.BlockSpec((1, n), lambda i: (0, 0))],
        out_specs=pl.BlockSpec((tm, n), lambda i: (i, 0)),
        compiler_params=_params("parallel"),
        name="linear_ln",
    )(x, w_all, res, g.reshape(1, n), b.reshape(1, n))


def _gelu_tanh(x):
    return 0.5 * x * (1.0 + jnp.tanh(math.sqrt(2.0 / math.pi) * (x + 0.044715 * x * x * x)))


def _s5_kernel(u_ref, ar_ref, ai_ref, ldt_ref, br_ref, bi_ref, cr_ref, ci_ref, d_ref, o_ref, w_scr,
               *, batch, n_chunks):
    cs, p, n = S5_CHUNK, S5_GROUP, S5_STATE
    width = cs * p
    n2 = 2 * n

    @pl.when(pl.program_id(0) == 0)
    def _():
        w_scr[...] = jnp.zeros(w_scr.shape, w_scr.dtype)

    dt = jnp.exp(ldt_ref[0])
    lam_r = jnp.minimum(ar_ref[0], -1e-4)
    lam_i = ai_ref[0]
    lo_half = lax.broadcasted_iota(jnp.int32, (1, n2), 1) < n

    def a_pow(tau):
        mag = jnp.exp(lam_r * dt * tau)
        ph = lam_i * dt * tau
        return mag * jnp.cos(ph), mag * jnp.sin(ph)

    tau = lax.broadcasted_iota(jnp.int32, (cs, 1), 0).astype(F32)
    a0r, a0i = a_pow(tau)
    a1r, a1i = a_pow(tau + 1.0)
    avr, avi = a_pow((cs - 1.0) - tau)
    acr, aci = a_pow(jnp.full((1, 1), float(cs), F32))

    abr, abi = a1r[0:1], a1i[0:1]
    den = lam_r * lam_r + lam_i * lam_i
    kr = ((abr - 1.0) * lam_r + abi * lam_i) / den
    ki = (abi * lam_r - (abr - 1.0) * lam_i) / den
    b_r, b_i = br_ref[0], bi_ref[0]
    bbr = kr * b_r - ki * b_i
    bbi = kr * b_i + ki * b_r
    c_r, c_i = cr_ref[0], ci_ref[0]

    def rep_rows(x):
        return jnp.concatenate([jnp.broadcast_to(x[t:t + 1, :], (p, n2)) for t in range(cs)], axis=0)

    def tile_rows(x):
        return jnp.concatenate([x] * cs, axis=0)

    def sel(lo, hi):
        return jnp.where(lo_half, lo, hi)

    crr, cii = tile_rows(c_r), tile_rows(c_i)
    brr, bii = tile_rows(bbr), tile_rows(bbi)
    c_taps = crr * rep_rows(sel(a0r, -a0i)) - cii * rep_rows(sel(a0i, a0r))
    c_out = crr * rep_rows(sel(a1r, -a1i)) - cii * rep_rows(sel(a1i, a1r))
    b_in = brr * rep_rows(sel(avr, avi)) + bii * rep_rows(sel(-avi, avr))
    b_in_sw = brr * rep_rows(sel(avi, avr)) + bii * rep_rows(sel(avr, -avi))

    b_mat = sel(bbr, bbi)
    taps = lax.dot_general(b_mat, c_taps, (((1,), (1,)), ((), ())),
                           precision=lax.Precision.HIGHEST, preferred_element_type=F32)

    lane = lax.broadcasted_iota(jnp.int32, (p, width), 1)
    band = [taps]
    for j in range(1, LANES // p):
        band.append(jnp.where(lane >= p * j, pltpu.roll(taps, p * j, axis=1), 0.0))
    band = jnp.concatenate(band, axis=0).astype(BF16)
    for q in range(width // LANES):
        w_scr[q * LANES:(q + 1) * LANES, q * LANES:width] = band[:, 0:width - q * LANES]
    w_scr[:, width:width + n2] = b_in.astype(BF16)
    w_scr[:, width + n2:width + 2 * n2] = b_in_sw.astype(BF16)

    u = u_ref[0]
    res = jnp.dot(u, w_scr[...], preferred_element_type=F32)
    y = res[:, 0:width]
    s_in = res[:, width:width + n2]
    s_sw = res[:, width + n2:width + 2 * n2]

    m_rr = acr
    m_ni = sel(-aci, aci)
    m_pi = sel(aci, -aci)
    h = jnp.zeros((batch, n2), F32)
    h_sw = jnp.zeros((batch, n2), F32)
    starts = []
    for c in range(n_chunks):
        starts.append(h)
        rows = slice(c * batch, (c + 1) * batch)
        h, h_sw = (h * m_rr + h_sw * m_ni + s_in[rows], h_sw * m_rr + h * m_pi + s_sw[rows])
    h_start = jnp.concatenate(starts, axis=0).astype(BF16)
    y = y + lax.dot_general(h_start, c_out.astype(BF16), (((1,), (1,)), ((), ())),
                            preferred_element_type=F32)
    y = y + d_ref[0] * u.astype(F32)
    o_ref[0] = _gelu_tanh(y).astype(o_ref.dtype)


def _s5_core(u_t, a_re, a_im, log_dt, b_re, b_im, c_re, c_im, d, *, batch, n_chunks):
    g, rows, width = u_t.shape
    dup = lambda x: jnp.concatenate([x, x], axis=-1)
    a_re2 = dup(a_re).reshape(g, 1, 2 * S5_STATE)
    a_im2 = dup(a_im).reshape(g, 1, 2 * S5_STATE)
    ldt = log_dt.reshape(g, 1, 1)
    bt_re2 = dup(jnp.swapaxes(b_re, 1, 2))
    bt_im2 = dup(jnp.swapaxes(b_im, 1, 2))
    c_re2 = dup(c_re)
    c_im2 = dup(c_im)
    d_t = jnp.tile(d, (1, S5_CHUNK)).reshape(g, 1, width)
    vec = lambda w: pl.BlockSpec((1, 1, w), lambda i: (i, 0, 0))
    mat = pl.BlockSpec((1, S5_GROUP, 2 * S5_STATE), lambda i: (i, 0, 0))
    return pl.pallas_call(
        functools.partial(_s5_kernel, batch=batch, n_chunks=n_chunks),
        out_shape=jax.ShapeDtypeStruct((g, rows, width), BF16),
        grid=(g,),
        in_specs=[pl.BlockSpec((1, rows, width), lambda i: (i, 0, 0)),
                  vec(2 * S5_STATE), vec(2 * S5_STATE), vec(1), mat, mat, mat, mat, vec(width)],
        out_specs=pl.BlockSpec((1, rows, width), lambda i: (i, 0, 0)),
        scratch_shapes=[pltpu.VMEM((width, width + 4 * S5_STATE), BF16)],
        compiler_params=_params("arbitrary"),
        name="s5_core",
    )(u_t, a_re2, a_im2, ldt, bt_re2, bt_im2, c_re2, c_im2, d_t)


def _s5_mixer(h, res_g, res_b, layer, w_in, a_re, a_im, log_dt, b_re, b_im, c_re, c_im, d, w_glu, w_out,
              *, batch, seq):
    t = batch * seq
    nc = seq // S5_CHUNK
    u = _linear(h, w_in, layer, tm=512, out_dtype=BF16)
    u_t = u.reshape(batch, nc, S5_CHUNK, S5_GROUPS, S5_GROUP).transpose(3, 1, 0, 2, 4)
    u_t = u_t.reshape(S5_GROUPS, nc * batch, S5_CHUNK * S5_GROUP)
    z_t = _s5_core(u_t, a_re, a_im, log_dt, b_re, b_im, c_re, c_im, d, batch=batch, n_chunks=nc)
    z = z_t.reshape(S5_GROUPS, nc, batch, S5_CHUNK, S5_GROUP).transpose(2, 1, 3, 0, 4)
    z = z.reshape(t, D_MODEL)
    zg = _linear_glu(z, w_glu, layer, tm=512)
    return _linear_ln(zg, w_out, layer, h, res_g, res_b, tm=256)


def _dsa_proj_kernel(x_ref, w_ref, qn_ref, kn_ref, cq_ref, ckv_ref, ckvt_ref, kidx_ref, widxt_ref):
    xb = x_ref[...].astype(BF16)

    def rms(v, g):
        return v * lax.rsqrt(jnp.mean(v * v, axis=-1, keepdims=True) + RMS_EPS) * g

    o1 = Q_LORA + KV_LORA
    cq = jnp.dot(xb, w_ref[:, 0:Q_LORA], preferred_element_type=F32)
    cq_ref[...] = rms(cq, qn_ref[...]).astype(BF16)
    ckv = rms(jnp.dot(xb, w_ref[:, Q_LORA:o1], preferred_element_type=F32), kn_ref[...])
    ckv_ref[...] = ckv.astype(BF16)
    ckvt_ref[...] = ckv.T.astype(BF16)
    kidx_ref[...] = jnp.dot(xb, w_ref[:, o1:o1 + IDX_DIM], preferred_element_type=F32).astype(BF16)
    widx = jnp.dot(xb, w_ref[:, o1 + IDX_DIM:o1 + IDX_DIM + LANES], preferred_element_type=F32)
    widxt_ref[...] = (widx * (IDX_HEADS ** -0.5)).T[0:IDX_HEADS, :]


def _dsa_proj(h, w_in_pad, layer, q_norm, kv_norm, *, batch, seq, tm=512):
    t = h.shape[0]
    per_b = seq // tm
    row = lambda w: pl.BlockSpec((tm, w), lambda i: (i, 0))
    return pl.pallas_call(
        _dsa_proj_kernel,
        out_shape=(jax.ShapeDtypeStruct((t, Q_LORA), BF16), jax.ShapeDtypeStruct((t, KV_LORA), BF16),
                   jax.ShapeDtypeStruct((batch, KV_LORA, seq), BF16),
                   jax.ShapeDtypeStruct((t, IDX_DIM), BF16), jax.ShapeDtypeStruct((IDX_HEADS, t), F32)),
        grid=(t // tm,),
        in_specs=[row(D_MODEL), _layer_spec(w_in_pad, layer),
                  pl.BlockSpec((1, Q_LORA), lambda i: (0, 0)), pl.BlockSpec((1, KV_LORA), lambda i: (0, 0))],
        out_specs=(row(Q_LORA), row(KV_LORA),
                   pl.BlockSpec((None, KV_LORA, tm), lambda i: (i // per_b, 0, i % per_b)),
                   row(IDX_DIM), pl.BlockSpec((IDX_HEADS, tm), lambda i: (0, i))),
        compiler_params=_params("parallel"),
        name="dsa_proj",
    )(h, w_in_pad, q_norm.reshape(1, Q_LORA), kv_norm.reshape(1, KV_LORA))


def _reduce_rows(op, x):
    rows, lanes = x.shape
    part = op(x.reshape(rows // 64, 8, 8, lanes), axis=0)
    return op(op(part, axis=0), axis=0, keepdims=True)


def _indexer_kernel(q_ref, k_ref, w_ref, s_ref, thr_ref, key_scr, *, seq):
    jb = pl.program_id(1)
    n_sel = float(min(IDX_TOPK, seq // 4))
    int_min = jnp.int32(-2 ** 31)
    flip = jnp.int32(0x7FFFFFFF)

    def run(width):
        k = k_ref[0:width, :]
        wrow = w_ref[...] * (IDX_DIM ** -0.5)
        acc = jnp.zeros((width, Q_BLOCK), F32)
        for g in range(IDX_HEADS // 2):
            qg = jnp.concatenate([q_ref[:, h * IDX_DIM:(h + 1) * IDX_DIM] for h in (2 * g, 2 * g + 1)], axis=0)
            s = lax.dot_general(k, qg, (((1,), (1,)), ((), ())), preferred_element_type=F32)
            acc = (acc + jnp.maximum(s[:, 0:Q_BLOCK], 0.0) * wrow[2 * g:2 * g + 1, :]
                   + jnp.maximum(s[:, Q_BLOCK:2 * Q_BLOCK], 0.0) * wrow[2 * g + 1:2 * g + 2, :])
        k_pos = lax.broadcasted_iota(jnp.int32, (width, Q_BLOCK), 0)
        q_pos = jb * Q_BLOCK + lax.broadcasted_iota(jnp.int32, (width, Q_BLOCK), 1)
        acc = jnp.where(k_pos <= q_pos, acc, -jnp.inf)
        s_ref[0:width, :] = acc
        if width < seq:
            s_ref[width:seq, :] = jnp.full((seq - width, Q_BLOCK), -jnp.inf, F32)

        bits = lax.bitcast_convert_type(acc, jnp.int32)
        key_scr[0:width, :] = jnp.where(bits < 0, bits ^ flip, bits)

        def count_ge(cand):
            return _reduce_rows(jnp.sum, (key_scr[0:width, :] >= cand).astype(F32))

        lo = jnp.where(count_ge(jnp.zeros((1, Q_BLOCK), jnp.int32)) >= n_sel, jnp.int32(0), int_min)

        def body(i, lo):
            cand = lo + jnp.left_shift(jnp.int32(1), jnp.int32(30) - i)
            return jnp.where(count_ge(cand) >= n_sel, cand, lo)

        lo = lax.fori_loop(0, 31, body, lo)
        thr = lax.bitcast_convert_type(jnp.where(lo < 0, lo ^ flip, lo), F32)
        thr_ref[...] = jnp.broadcast_to(thr, thr_ref.shape)

    n_cls = seq // KEY_CHUNK
    per_cls = KEY_CHUNK // Q_BLOCK
    for cls in range(n_cls):
        pl.when(jb // per_cls == cls)(functools.partial(run, KEY_CHUNK * (cls + 1)))


def _indexer(qcat, kidx, widx_t, *, batch, seq):
    n_blk = seq // Q_BLOCK
    return pl.pallas_call(
        functools.partial(_indexer_kernel, seq=seq),
        out_shape=(jax.ShapeDtypeStruct((batch, n_blk, seq, Q_BLOCK), F32),
                   jax.ShapeDtypeStruct((batch, n_blk, 8, Q_BLOCK), F32)),
        grid=(batch, n_blk),
        in_specs=[pl.BlockSpec((Q_BLOCK, IDX_HEADS * IDX_DIM), lambda b, j: (b * n_blk + j, 1)),
                  pl.BlockSpec((seq, IDX_DIM), lambda b, j: (b, 0)),
                  pl.BlockSpec((IDX_HEADS, Q_BLOCK), lambda b, j: (0, b * n_blk + j))],
        out_specs=(pl.BlockSpec((None, None, seq, Q_BLOCK), lambda b, j: (b, j, 0, 0)),
                   pl.BlockSpec((None, None, 8, Q_BLOCK), lambda b, j: (b, j, 0, 0))),
        scratch_shapes=[pltpu.VMEM((seq, Q_BLOCK), jnp.int32)],
        compiler_params=_params("parallel", "parallel"),
        name="indexer",
    )(qcat, kidx, widx_t)


def _attn_kernel(q_ref, kv_ref, kvt_ref, s_ref, thr_ref, wuk_ref, wuvt_ref, bd_ref, bp_ref, o_ref,
                 ql_scr, m_scr, l_scr, acc_scr):
    jb = pl.program_id(1)
    kc = pl.program_id(2)
    last = (jb * Q_BLOCK) // KEY_CHUNK
    hd, qb = ATT_HEAD_DIM, Q_BLOCK
    sub = KEY_CHUNK // qb
    pair = 2 * qb

    @pl.when(kc == 0)
    def _():
        for h in range(ATT_HEADS):
            ql = jnp.dot(q_ref[:, h * hd:(h + 1) * hd], wuk_ref[h], preferred_element_type=F32)
            ql_scr[h * qb:(h + 1) * qb, :] = (ql * (hd ** -0.5)).astype(BF16)
        m_scr[...] = jnp.full(m_scr.shape, MASKED, F32)
        l_scr[...] = jnp.zeros(l_scr.shape, F32)
        acc_scr[...] = jnp.zeros(acc_scr.shape, F32)

    def attend(near):
        kv = kv_ref[...]
        kvt = kvt_ref[...]
        k_pos = kc * KEY_CHUNK + lax.broadcasted_iota(jnp.int32, (KEY_CHUNK, qb), 0)
        q_pos = jb * qb + lax.broadcasted_iota(jnp.int32, (KEY_CHUNK, qb), 1)
        keep = (s_ref[...] >= thr_ref[0:1, :]) & (k_pos <= q_pos)
        mask_add = jnp.where(keep, 0.0, MASKED)
        mask_add = jnp.concatenate([mask_add, mask_add], axis=1)
        for g in range(ATT_HEADS // 2):
            s = lax.dot_general(kv, ql_scr[g * pair:(g + 1) * pair, :], (((1,), (1,)), ((), ())),
                                preferred_element_type=F32)
            lg = s + mask_add
            if near:
                cols = []
                for h in (2 * g, 2 * g + 1):
                    tiles = []
                    for c in range(sub):
                        gblk = kc * sub + c
                        tiles.append(jnp.where(gblk == jb, bd_ref[h], jnp.where(gblk == jb - 1, bp_ref[h], 0.0)))
                    cols.append(jnp.concatenate(tiles, axis=0))
                lg = lg + jnp.concatenate(cols, axis=1)
            m_prev = m_scr[g, 0:1, :]
            m_new = jnp.maximum(m_prev, _reduce_rows(jnp.max, lg))
            corr = jnp.exp(m_prev - m_new)
            pr = jnp.exp(lg - m_new)
            l_scr[g] = jnp.broadcast_to(corr * l_scr[g, 0:1, :] + _reduce_rows(jnp.sum, pr), (8, pair))
            m_scr[g] = jnp.broadcast_to(m_new, (8, pair))
            pv = jnp.dot(kvt, pr.astype(BF16), preferred_element_type=F32)
            acc_scr[g] = acc_scr[g] * corr + pv

    near = kc * sub + sub >= jb
    pl.when((kc <= last) & near)(functools.partial(attend, True))
    pl.when((kc <= last) & jnp.logical_not(near))(functools.partial(attend, False))

    @pl.when(kc == last)
    def _():
        for h in range(ATT_HEADS):
            g, half = h // 2, (h % 2) * qb
            ol = (acc_scr[g, :, half:half + qb] / l_scr[g, 0:1, half:half + qb]).astype(BF16)
            oh = jnp.dot(wuvt_ref[h], ol, preferred_element_type=F32)
            o_ref[:, h * hd:(h + 1) * hd] = oh.T.astype(o_ref.dtype)


def _rel_bucket_table():
    n = np.arange(2 * Q_BLOCK)
    max_exact = REL_BUCKETS // 2
    nf = np.maximum(n, 1).astype(np.float32)
    large = max_exact + (np.log(nf / np.float32(max_exact)) / np.float32(math.log(REL_MAX_DIST / max_exact))
                         * np.float32(REL_BUCKETS - max_exact)).astype(np.int32)
    large = np.minimum(large, REL_BUCKETS - 1)
    return np.where(n < max_exact, n, large)


def _attention(qcat, ckv, ckv_t, scores, thr, w_uk, w_uv_t, layer, rel_bias, *, batch, seq):
    t = batch * seq
    n_blk = seq // Q_BLOCK
    n_kc = seq // KEY_CHUNK
    bucket = _rel_bucket_table()
    assert bucket[Q_BLOCK + 1:].min() == REL_BUCKETS - 1
    kk = np.arange(Q_BLOCK)[:, None]
    qq = np.arange(Q_BLOCK)[None, :]
    rb = rel_bias.astype(F32)
    rb = rb - rb[REL_BUCKETS - 1]
    diag = jnp.where(jnp.asarray(kk <= qq)[None], jnp.moveaxis(rb[bucket[np.maximum(qq - kk, 0)]], -1, 0), 0.0)
    prev = jnp.moveaxis(rb[bucket[Q_BLOCK + qq - kk]], -1, 0)
    n_pair = ATT_HEADS // 2
    pair = 2 * Q_BLOCK

    def chunk(jj, kc):
        return jnp.minimum(kc, (jj * Q_BLOCK) // KEY_CHUNK)

    qrow = lambda b, jj, kc: (b * n_blk + jj, 0)
    full3 = lambda b, jj, kc: (0, 0, 0)
    return pl.pallas_call(
        _attn_kernel,
        out_shape=jax.ShapeDtypeStruct((t, ATT_HEADS * ATT_HEAD_DIM), BF16),
        grid=(batch, n_blk, n_kc),
        in_specs=[pl.BlockSpec((Q_BLOCK, ATT_HEADS * ATT_HEAD_DIM), qrow),
                  pl.BlockSpec((KEY_CHUNK, KV_LORA), lambda b, jj, kc: (b * n_kc + chunk(jj, kc), 0)),
                  pl.BlockSpec((None, KV_LORA, KEY_CHUNK), lambda b, jj, kc: (b, 0, chunk(jj, kc))),
                  pl.BlockSpec((None, None, KEY_CHUNK, Q_BLOCK), lambda b, jj, kc: (b, jj, chunk(jj, kc), 0)),
                  pl.BlockSpec((None, None, 8, Q_BLOCK), lambda b, jj, kc: (b, jj, 0, 0)),
                  _layer_spec(w_uk, layer),
                  _layer_spec(w_uv_t, layer),
                  pl.BlockSpec((ATT_HEADS, Q_BLOCK, Q_BLOCK), full3),
                  pl.BlockSpec((ATT_HEADS, Q_BLOCK, Q_BLOCK), full3)],
        out_specs=pl.BlockSpec((Q_BLOCK, ATT_HEADS * ATT_HEAD_DIM), qrow),
        scratch_shapes=[pltpu.VMEM((ATT_HEADS * Q_BLOCK, KV_LORA), BF16),
                        pltpu.VMEM((n_pair, 8, pair), F32),
                        pltpu.VMEM((n_pair, 8, pair), F32),
                        pltpu.VMEM((n_pair, KV_LORA, pair), F32)],
        compiler_params=_params("parallel", "parallel", "arbitrary"),
        name="latent_attention",
    )(qcat, ckv, ckv_t, scores, thr, w_uk, w_uv_t, diag, prev)


def _dsa_mixer(h, res_g, res_b, layer, rel_bias, w_in_pad, q_norm, kv_norm, w_q, w_uk, w_uv_t, w_out,
               *, batch, seq):
    cq, ckv, ckv_t, kidx, widx_t = _dsa_proj(h, w_in_pad, layer, q_norm, kv_norm, batch=batch, seq=seq)
    qcat = _linear(cq, w_q, layer, tm=512, out_dtype=BF16)
    scores, thr = _indexer(qcat, kidx, widx_t, batch=batch, seq=seq)
    o = _attention(qcat, ckv, ckv_t, scores, thr, w_uk, w_uv_t, layer, rel_bias, batch=batch, seq=seq)
    return _linear_ln(o, w_out, layer, h, res_g, res_b, tm=256)


def _router_kernel(x_ref, w_ref, b_ref, route_ref, cnt_ref, cnt_scr):
    logits = jnp.dot(x_ref[...], w_ref[...], precision=lax.Precision.HIGHEST,
                     preferred_element_type=F32) + b_ref[...]
    lane = lax.broadcasted_iota(jnp.int32, logits.shape, 1)
    big = jnp.int32(LANES)
    is_group = (lane >= MOE_EXPERTS) & (lane < MOE_EXPERTS + MOE_GROUPS)
    gl = jnp.where(is_group, logits, -jnp.inf)
    gm = jnp.max(gl, axis=1, keepdims=True)
    g_p = 1.0 / jnp.sum(jnp.exp(gl - gm), axis=1, keepdims=True)
    g_idx = jnp.min(jnp.where(gl == gm, lane, big), axis=1, keepdims=True) - MOE_EXPERTS
    in_group = (lane < MOE_EXPERTS) & (jnp.right_shift(lane, 3) == g_idx)
    el = jnp.where(in_group, logits, -jnp.inf)
    em = jnp.max(el, axis=1, keepdims=True)
    ee = jnp.exp(el - em)
    prob = ee / jnp.sum(ee, axis=1, keepdims=True)
    p1 = jnp.max(prob, axis=1, keepdims=True)
    i1 = jnp.min(jnp.where(in_group & (prob == p1), lane, big), axis=1, keepdims=True)
    rest = in_group & (lane != i1)
    p2 = jnp.max(jnp.where(rest, prob, -1.0), axis=1, keepdims=True)
    i2 = jnp.min(jnp.where(rest & (prob == p2), lane, big), axis=1, keepdims=True)
    den = p1 + p2
    g1 = g_p * (p1 / den)
    g2 = g_p * (p2 / den)

    @pl.when(pl.program_id(0) == 0)
    def _():
        cnt_scr[...] = jnp.zeros(cnt_scr.shape, F32)

    tm = logits.shape[0]
    oh1 = (lane == i1).astype(BF16)
    oh2 = (lane == i2).astype(BF16)
    r_i = lax.broadcasted_iota(jnp.int32, (tm, tm), 0)
    c_i = lax.broadcasted_iota(jnp.int32, (tm, tm), 1)
    tri = (c_i < r_i).astype(BF16)
    pre1 = jnp.dot(tri, oh1, preferred_element_type=F32)
    pre2 = jnp.dot(tri, oh2, preferred_element_type=F32)
    tot1 = jnp.sum(oh1.astype(F32), axis=0, keepdims=True)
    tot2 = jnp.sum(oh2.astype(F32), axis=0, keepdims=True)
    base = cnt_scr[...]
    rank1 = jnp.sum(jnp.where(lane == i1, base + pre1, 0.0), axis=1, keepdims=True)
    rank2 = jnp.sum(jnp.where(lane == i2, base + tot1 + pre2, 0.0), axis=1, keepdims=True)
    cnt_scr[...] = base + tot1 + tot2
    cnt_ref[...] = jnp.broadcast_to(cnt_scr[...], cnt_ref.shape)

    out = jnp.where(lane == 0, i1.astype(F32), jnp.where(lane == 1, i2.astype(F32), 0.0))
    out = jnp.where(lane == 2, g1, jnp.where(lane == 3, g2, out))
    out = jnp.where(lane == 4, rank1, jnp.where(lane == 5, rank2, out))
    route_ref[...] = out


def _router(h, w_r, b_r, *, tm=512):
    t = h.shape[0]
    return pl.pallas_call(
        _router_kernel,
        out_shape=(jax.ShapeDtypeStruct((t, LANES), F32), jax.ShapeDtypeStruct((8, LANES), F32)),
        grid=(t // tm,),
        in_specs=[pl.BlockSpec((tm, D_MODEL), lambda i: (i, 0)),
                  pl.BlockSpec((D_MODEL, LANES), lambda i: (0, 0)),
                  pl.BlockSpec((1, LANES), lambda i: (0, 0))],
        out_specs=(pl.BlockSpec((tm, LANES), lambda i: (i, 0)),
                   pl.BlockSpec((8, LANES), lambda i: (0, 0))),
        scratch_shapes=[pltpu.VMEM((1, LANES), F32)],
        compiler_params=_params("arbitrary"),
        name="moe_router",
    )(h, w_r, b_r)


def _moe_expert_kernel(te_ref, nu_ref, rt_ref, x_hbm, wg_ref, wu_ref, wd_ref, y_ref,
                       xbuf, sem, wgb, wub, wdb):
    i = pl.program_id(0)
    n_used = nu_ref[0]
    slot = lax.rem(i, 2)
    tme = EXPERT_TILE

    def start_gather(tile, buf):
        def body(r, carry):
            tok = rt_ref[tile * tme + r]
            pltpu.make_async_copy(x_hbm.at[pl.ds(tok, 1)], xbuf.at[buf, pl.ds(r, 1)], sem.at[buf]).start()
            return carry
        lax.fori_loop(0, tme, body, 0, unroll=8)

    @pl.when(i == 0)
    def _():
        start_gather(0, 0)

    @pl.when(i + 1 < n_used)
    def _():
        start_gather(i + 1, 1 - slot)

    te = te_ref[i]
    prev = te_ref[jnp.maximum(i - 1, 0)]

    @pl.when((i == 0) | (te != prev))
    def _():
        wgb[...] = wg_ref[...].astype(BF16)
        wub[...] = wu_ref[...].astype(BF16)
        wdb[...] = wd_ref[...].astype(BF16)

    @pl.when(i < n_used)
    def _():
        pltpu.make_async_copy(x_hbm.at[pl.ds(0, tme)], xbuf.at[slot], sem.at[slot]).wait()
        xb = xbuf[slot].astype(BF16)
        gp = jnp.dot(xb, wgb[...], preferred_element_type=F32)
        up = jnp.dot(xb, wub[...], preferred_element_type=F32)
        hidden = (gp * jax.nn.sigmoid(gp)) * up
        y_ref[...] = jnp.dot(hidden.astype(BF16), wdb[...], preferred_element_type=F32)

    @pl.when(i >= n_used)
    def _():
        y_ref[...] = jnp.zeros(y_ref.shape, F32)


def _moe_experts(tile_expert, n_used, row_token, h, w_gate, w_up, w_down, layer):
    n_tiles = tile_expert.shape[0]
    tme = EXPERT_TILE
    wspec = lambda a, b: pl.BlockSpec((None, None, a, b), lambda i, te, nu, rt: (layer, te[i], 0, 0))
    return pl.pallas_call(
        _moe_expert_kernel,
        out_shape=jax.ShapeDtypeStruct((n_tiles * tme, D_MODEL), F32),
        grid_spec=pltpu.PrefetchScalarGridSpec(
            num_scalar_prefetch=3,
            grid=(n_tiles,),
            in_specs=[pl.BlockSpec(memory_space=pl.ANY),
                      wspec(D_MODEL, MOE_FF), wspec(D_MODEL, MOE_FF), wspec(MOE_FF, D_MODEL)],
            out_specs=pl.BlockSpec((tme, D_MODEL), lambda i, te, nu, rt: (i, 0)),
            scratch_shapes=[pltpu.VMEM((2, tme, D_MODEL), F32),
                            pltpu.SemaphoreType.DMA((2,)),
                            pltpu.VMEM((D_MODEL, MOE_FF), BF16),
                            pltpu.VMEM((D_MODEL, MOE_FF), BF16),
                            pltpu.VMEM((MOE_FF, D_MODEL), BF16)]),
        compiler_params=_params("arbitrary"),
        name="moe_experts",
    )(tile_expert, n_used, row_token, h, w_gate, w_up, w_down)


def _moe_combine_kernel(slot_ref, h_ref, route_ref, y_hbm, g_ref, b_ref, o_ref, ybuf, sem):
    i = pl.program_id(0)
    n = pl.num_programs(0)
    slot = lax.rem(i, 2)
    tm = h_ref.shape[0]

    def start_gather(tile, buf):
        def body(r, carry):
            for k in range(2):
                s = slot_ref[(tile * tm + r) * 2 + k]
                pltpu.make_async_copy(y_hbm.at[pl.ds(s, 1)], ybuf.at[buf, pl.ds(k * tm + r, 1)],
                                      sem.at[buf]).start()
            return carry
        lax.fori_loop(0, tm, body, 0, unroll=4)

    @pl.when(i == 0)
    def _():
        start_gather(0, 0)

    @pl.when(i + 1 < n)
    def _():
        start_gather(i + 1, 1 - slot)

    pltpu.make_async_copy(y_hbm.at[pl.ds(0, 2 * tm)], ybuf.at[slot], sem.at[slot]).wait()
    ffn = route_ref[:, 2:3] * ybuf[slot, pl.ds(0, tm)] + route_ref[:, 3:4] * ybuf[slot, pl.ds(tm, tm)]
    o_ref[...] = _layer_norm_rows(DN_ALPHA * h_ref[...] + ffn, g_ref[...], b_ref[...])


def _moe_combine(slots, h, route, y, ln_g, ln_b, *, tm):
    t = h.shape[0]
    row = lambda w: pl.BlockSpec((tm, w), lambda i, s: (i, 0))
    vec = pl.BlockSpec((1, D_MODEL), lambda i, s: (0, 0))
    return pl.pallas_call(
        _moe_combine_kernel,
        out_shape=jax.ShapeDtypeStruct((t, D_MODEL), F32),
        grid_spec=pltpu.PrefetchScalarGridSpec(
            num_scalar_prefetch=1,
            grid=(t // tm,),
            in_specs=[row(D_MODEL), row(LANES), pl.BlockSpec(memory_space=pl.ANY), vec, vec],
            out_specs=row(D_MODEL),
            scratch_shapes=[pltpu.VMEM((2, 2 * tm, D_MODEL), F32),
                            pltpu.SemaphoreType.DMA((2,))]),
        compiler_params=_params("arbitrary"),
        name="moe_combine",
    )(slots, h, route, y, ln_g.reshape(1, D_MODEL), ln_b.reshape(1, D_MODEL))


def _moe_plan(route, counts, n_tiles):
    t = route.shape[0]
    tme = EXPERT_TILE
    cnt = counts[0, :MOE_EXPERTS].astype(jnp.int32)
    tiles = (cnt + tme - 1) // tme
    tile_end = jnp.cumsum(tiles)
    tile_start = tile_end - tiles
    n_used = tile_end[-1]
    choice = route[:, 0:2].astype(jnp.int32)
    rank = route[:, 4:6].astype(jnp.int32)
    slots = (tile_start[choice] * tme + rank).reshape(-1)
    tile_ids = jnp.arange(n_tiles, dtype=jnp.int32)
    te = jnp.sum((tile_ids[:, None] >= tile_end[None, :]).astype(jnp.int32), axis=1)
    last_e = jnp.sum((n_used - 1 >= tile_end).astype(jnp.int32))
    te = jnp.where(tile_ids < n_used, te, last_e)
    row_token = jnp.zeros((n_tiles * tme,), jnp.int32).at[slots].set(jnp.arange(2 * t, dtype=jnp.int32) // 2)
    return te, n_used.reshape(1).astype(jnp.int32), row_token, slots.astype(jnp.int32)


def _hier_moe(h, layer, w_group, b_group, w_expert, b_expert, w_gate, w_up, w_down, ln_g, ln_b):
    t = h.shape[0]
    pad = LANES - MOE_EXPERTS - MOE_GROUPS
    w_r = jnp.concatenate([w_expert, w_group, jnp.zeros((D_MODEL, pad), F32)], axis=1)
    b_r = jnp.concatenate([b_expert, b_group, jnp.zeros((pad,), F32)]).reshape(1, LANES)
    route, counts = _router(h, w_r, b_r)
    n_tiles = (2 * t) // EXPERT_TILE + MOE_EXPERTS
    te, n_used, row_token, slots = _moe_plan(route, counts, n_tiles)
    y = _moe_experts(te, n_used, row_token, h, w_gate, w_up, w_down, layer)
    return _moe_combine(slots, h, route, y, ln_g, ln_b, tm=COMBINE_TILE)


def kernel(x, rel_bias, s5_w_in, s5_a_re, s5_a_im, s5_log_dt, s5_b_re, s5_b_im, s5_c_re, s5_c_im, s5_d, s5_w_glu, s5_w_out, dsa_w_in, dsa_q_norm, dsa_kv_norm, dsa_w_uq, dsa_w_qidx, dsa_w_uk, dsa_w_uv, dsa_w_out, moe_w_group, moe_b_group, moe_w_expert, moe_b_expert, moe_w_gate, moe_w_up, moe_w_down, ln_mix_g, ln_mix_b, ln_ffn_g, ln_ffn_b):
    batch, seq, dm = x.shape
    h = x.reshape(batch * seq, dm)
    s5_w_in, s5_w_glu, s5_w_out = (w.astype(BF16) for w in (s5_w_in, s5_w_glu, s5_w_out))
    n_pad = Q_LORA + KV_LORA + IDX_DIM + LANES - dsa_w_in.shape[-1]
    dsa_w_in_pad = jnp.pad(dsa_w_in, ((0, 0), (0, 0), (0, n_pad))).astype(BF16)
    dsa_w_q = jnp.concatenate([dsa_w_uq, dsa_w_qidx], axis=-1).astype(BF16)
    dsa_w_uk, dsa_w_out = dsa_w_uk.astype(BF16), dsa_w_out.astype(BF16)
    dsa_w_uv_t = jnp.swapaxes(dsa_w_uv, -1, -2).astype(BF16)
    for i in range(DEPTH):
        j = i // 2
        if i % 2 == 0:
            h = _s5_mixer(h, ln_mix_g[i], ln_mix_b[i], j, s5_w_in, s5_a_re[j], s5_a_im[j], s5_log_dt[j],
                          s5_b_re[j], s5_b_im[j], s5_c_re[j], s5_c_im[j], s5_d[j], s5_w_glu, s5_w_out,
                          batch=batch, seq=seq)
        else:
            h = _dsa_mixer(h, ln_mix_g[i], ln_mix_b[i], j, rel_bias, dsa_w_in_pad, dsa_q_norm[j],
                           dsa_kv_norm[j], dsa_w_q, dsa_w_uk, dsa_w_uv_t, dsa_w_out, batch=batch, seq=seq)
        h = _hier_moe(h, i, moe_w_group[i], moe_b_group[i], moe_w_expert[i], moe_b_expert[i],
                      moe_w_gate, moe_w_up, moe_w_down, ln_ffn_g[i], ln_ffn_b[i])
    return h.reshape(batch, seq, dm)
```

```python
import functools
import math

import numpy as np
import jax
import jax.numpy as jnp
from jax import lax
from jax.experimental import pallas as pl
from jax.experimental.pallas import tpu as pltpu

F32 = jnp.float32
BF16 = jnp.bfloat16

D_MODEL = 2048
DEPTH = 4
S5_GROUP = 16
S5_GROUPS = D_MODEL // S5_GROUP
S5_STATE = 64
S5_CHUNK = 64
ATT_HEADS = 16
ATT_HEAD_DIM = 128
Q_LORA = 512
KV_LORA = 512
IDX_HEADS = 16
IDX_DIM = 128
IDX_TOPK = 256
Q_BLOCK = 128
KEY_CHUNK = 512
REL_BUCKETS = 32
REL_MAX_DIST = 128
MOE_GROUPS = 4
MOE_PER_GROUP = 8
MOE_EXPERTS = 32
MOE_FF = 256
EXPERT_TILE = 256
COMBINE_TILE = 256
DN_ALPHA = (2 * DEPTH) ** 0.25
LN_EPS = 1e-5
RMS_EPS = 1e-6

LANES = 128
MASKED = -1e30
VMEM_LIMIT = 56 * 1024 * 1024


def _params(*sem):
    return pltpu.CompilerParams(dimension_semantics=sem, vmem_limit_bytes=VMEM_LIMIT)


def _layer_norm_rows(y, g, b):
    mean = jnp.mean(y, axis=-1, keepdims=True)
    yc = y - mean
    var = jnp.mean(yc * yc, axis=-1, keepdims=True)
    return yc * lax.rsqrt(var + LN_EPS) * g + b


def _mm_plain_kernel(x_ref, w_ref, o_ref, *, tn):
    xb = x_ref[...].astype(BF16)
    for n0 in range(0, o_ref.shape[1], tn):
        acc = jnp.dot(xb, w_ref[:, n0:n0 + tn], preferred_element_type=F32)
        o_ref[:, n0:n0 + tn] = acc.astype(o_ref.dtype)


def _mm_glu_kernel(z_ref, w_ref, o_ref, *, tn):
    zb = z_ref[...]
    for n0 in range(0, o_ref.shape[1], tn):
        acc = jnp.dot(zb, w_ref[:, n0:n0 + tn], preferred_element_type=F32)
        zc = z_ref[:, n0:n0 + tn].astype(F32)
        o_ref[:, n0:n0 + tn] = (zc * jax.nn.sigmoid(acc)).astype(o_ref.dtype)


PACK_ROWS = D_MODEL // (2 * LANES)


def _store_packed(o_ref, y):
    half = y.shape[1] // 2
    bits = lambda v: lax.bitcast_convert_type(v.astype(BF16).astype(F32), jnp.uint32)
    word = (bits(y[:, half:]) & jnp.uint32(0xFFFF0000)) | (bits(y[:, :half]) >> 16)
    rows = y.shape[0]
    for c in range(PACK_ROWS):
        o_ref[pl.ds(c, rows, stride=PACK_ROWS), :] = word[:, c * LANES:(c + 1) * LANES]


def _load_packed(x_ref, rows):
    word = jnp.concatenate([x_ref[pl.ds(c, rows, stride=PACK_ROWS), :] for c in range(PACK_ROWS)], axis=1)
    lo = lax.bitcast_convert_type(word << 16, F32)
    hi = lax.bitcast_convert_type(word & jnp.uint32(0xFFFF0000), F32)
    return jnp.concatenate([lo, hi], axis=1)


def _mm_ln_kernel(x_ref, w_ref, res_ref, g_ref, b_ref, o_ref, op_ref, *, tn):
    xb = x_ref[...].astype(BF16)
    for n0 in range(0, o_ref.shape[1], tn):
        acc = jnp.dot(xb, w_ref[:, n0:n0 + tn], preferred_element_type=F32)
        o_ref[:, n0:n0 + tn] = DN_ALPHA * res_ref[:, n0:n0 + tn] + acc
    y = _layer_norm_rows(o_ref[...], g_ref[...], b_ref[...])
    o_ref[...] = y
    _store_packed(op_ref, y)


def _layer_spec(w_all, layer):
    shape = w_all.shape[1:]
    return pl.BlockSpec((None,) + shape, lambda *_: (layer,) + (0,) * len(shape))


def _linear(x, w_all, layer, *, tm, out_dtype, tn=512):
    m, k = x.shape
    n = w_all.shape[-1]
    return pl.pallas_call(
        functools.partial(_mm_plain_kernel, tn=tn),
        out_shape=jax.ShapeDtypeStruct((m, n), out_dtype),
        grid=(m // tm,),
        in_specs=[pl.BlockSpec((tm, k), lambda i: (i, 0)), _layer_spec(w_all, layer)],
        out_specs=pl.BlockSpec((tm, n), lambda i: (i, 0)),
        compiler_params=_params("parallel"),
        name="linear",
    )(x, w_all)


def _linear_glu(z, w_all, layer, *, tm, tn=512):
    m, k = z.shape
    return pl.pallas_call(
        functools.partial(_mm_glu_kernel, tn=tn),
        out_shape=jax.ShapeDtypeStruct((m, k), BF16),
        grid=(m // tm,),
        in_specs=[pl.BlockSpec((tm, k), lambda i: (i, 0)), _layer_spec(w_all, layer)],
        out_specs=pl.BlockSpec((tm, k), lambda i: (i, 0)),
        compiler_params=_params("parallel"),
        name="linear_glu",
    )(z, w_all)


def _linear_ln(x, w_all, layer, res, g, b, *, tm, tn=512):
    m, k = x.shape
    n = w_all.shape[-1]
    return pl.pallas_call(
        functools.partial(_mm_ln_kernel, tn=tn),
        out_shape=(jax.ShapeDtypeStruct((m, n), F32), jax.ShapeDtypeStruct((m * PACK_ROWS, LANES), jnp.uint32)),
        grid=(m // tm,),
        in_specs=[pl.BlockSpec((tm, k), lambda i: (i, 0)),
                  _layer_spec(w_all, layer),
                  pl.BlockSpec((tm, n), lambda i: (i, 0)),
                  pl.BlockSpec((1, n), lambda i: (0, 0)),
                  pl.BlockSpec((1, n), lambda i: (0, 0))],
        out_specs=(pl.BlockSpec((tm, n), lambda i: (i, 0)),
                   pl.BlockSpec((tm * PACK_ROWS, LANES), lambda i: (i, 0))),
        compiler_params=_params("parallel"),
        name="linear_ln",
    )(x, w_all, res, g.reshape(1, n), b.reshape(1, n))


def _gelu_tanh(x):
    return 0.5 * x * (1.0 + jnp.tanh(math.sqrt(2.0 / math.pi) * (x + 0.044715 * x * x * x)))


def _s5_kernel(u_ref, ar_ref, ai_ref, ldt_ref, br_ref, bi_ref, cr_ref, ci_ref, d_ref, o_ref, w_scr,
               *, batch, n_chunks):
    cs, p, n = S5_CHUNK, S5_GROUP, S5_STATE
    width = cs * p
    n2 = 2 * n

    @pl.when(pl.program_id(0) == 0)
    def _():
        w_scr[...] = jnp.zeros(w_scr.shape, w_scr.dtype)

    dt = jnp.exp(ldt_ref[0])
    lam_r = jnp.minimum(ar_ref[0], -1e-4)
    lam_i = ai_ref[0]
    lo_half = lax.broadcasted_iota(jnp.int32, (1, n2), 1) < n

    def a_pow(tau):
        mag = jnp.exp(lam_r * dt * tau)
        ph = lam_i * dt * tau
        return mag * jnp.cos(ph), mag * jnp.sin(ph)

    tau = lax.broadcasted_iota(jnp.int32, (cs, 1), 0).astype(F32)
    a0r, a0i = a_pow(tau)
    a1r, a1i = a_pow(tau + 1.0)
    avr, avi = a_pow((cs - 1.0) - tau)
    acr, aci = a_pow(jnp.full((1, 1), float(cs), F32))

    abr, abi = a1r[0:1], a1i[0:1]
    den = lam_r * lam_r + lam_i * lam_i
    kr = ((abr - 1.0) * lam_r + abi * lam_i) / den
    ki = (abi * lam_r - (abr - 1.0) * lam_i) / den
    b_r, b_i = br_ref[0], bi_ref[0]
    bbr = kr * b_r - ki * b_i
    bbi = kr * b_i + ki * b_r
    c_r, c_i = cr_ref[0], ci_ref[0]

    def rep_rows(x):
        return jnp.concatenate([jnp.broadcast_to(x[t:t + 1, :], (p, n2)) for t in range(cs)], axis=0)

    def tile_rows(x):
        return jnp.concatenate([x] * cs, axis=0)

    def sel(lo, hi):
        return jnp.where(lo_half, lo, hi)

    crr, cii = tile_rows(c_r), tile_rows(c_i)
    brr, bii = tile_rows(bbr), tile_rows(bbi)
    c_taps = crr * rep_rows(sel(a0r, -a0i)) - cii * rep_rows(sel(a0i, a0r))
    c_out = crr * rep_rows(sel(a1r, -a1i)) - cii * rep_rows(sel(a1i, a1r))
    b_in = brr * rep_rows(sel(avr, avi)) + bii * rep_rows(sel(-avi, avr))
    b_in_sw = brr * rep_rows(sel(avi, avr)) + bii * rep_rows(sel(avr, -avi))

    b_mat = sel(bbr, bbi)
    taps = lax.dot_general(b_mat, c_taps, (((1,), (1,)), ((), ())),
                           precision=lax.Precision.HIGHEST, preferred_element_type=F32)

    lane = lax.broadcasted_iota(jnp.int32, (p, width), 1)
    band = [taps]
    for j in range(1, LANES // p):
        band.append(jnp.where(lane >= p * j, pltpu.roll(taps, p * j, axis=1), 0.0))
    band = jnp.concatenate(band, axis=0).astype(BF16)
    for q in range(width // LANES):
        w_scr[q * LANES:(q + 1) * LANES, q * LANES:width] = band[:, 0:width - q * LANES]
    w_scr[:, width:width + n2] = b_in.astype(BF16)
    w_scr[:, width + n2:width + 2 * n2] = b_in_sw.astype(BF16)

    u = u_ref[0]
    res = jnp.dot(u, w_scr[...], preferred_element_type=F32)
    y = res[:, 0:width]
    s_in = res[:, width:width + n2]
    s_sw = res[:, width + n2:width + 2 * n2]

    m_rr = acr
    m_ni = sel(-aci, aci)
    m_pi = sel(aci, -aci)
    h = jnp.zeros((batch, n2), F32)
    h_sw = jnp.zeros((batch, n2), F32)
    starts = []
    for c in range(n_chunks):
        starts.append(h)
        rows = slice(c * batch, (c + 1) * batch)
        h, h_sw = (h * m_rr + h_sw * m_ni + s_in[rows], h_sw * m_rr + h * m_pi + s_sw[rows])
    h_start = jnp.concatenate(starts, axis=0).astype(BF16)
    y = y + lax.dot_general(h_start, c_out.astype(BF16), (((1,), (1,)), ((), ())),
                            preferred_element_type=F32)
    y = y + d_ref[0] * u.astype(F32)
    o_ref[0] = _gelu_tanh(y).astype(o_ref.dtype)


def _s5_core(u_t, a_re, a_im, log_dt, b_re, b_im, c_re, c_im, d, *, batch, n_chunks):
    g, rows, width = u_t.shape
    dup = lambda x: jnp.concatenate([x, x], axis=-1)
    a_re2 = dup(a_re).reshape(g, 1, 2 * S5_STATE)
    a_im2 = dup(a_im).reshape(g, 1, 2 * S5_STATE)
    ldt = log_dt.reshape(g, 1, 1)
    bt_re2 = dup(jnp.swapaxes(b_re, 1, 2))
    bt_im2 = dup(jnp.swapaxes(b_im, 1, 2))
    c_re2 = dup(c_re)
    c_im2 = dup(c_im)
    d_t = jnp.tile(d, (1, S5_CHUNK)).reshape(g, 1, width)
    vec = lambda w: pl.BlockSpec((1, 1, w), lambda i: (i, 0, 0))
    mat = pl.BlockSpec((1, S5_GROUP, 2 * S5_STATE), lambda i: (i, 0, 0))
    return pl.pallas_call(
        functools.partial(_s5_kernel, batch=batch, n_chunks=n_chunks),
        out_shape=jax.ShapeDtypeStruct((g, rows, width), BF16),
        grid=(g,),
        in_specs=[pl.BlockSpec((1, rows, width), lambda i: (i, 0, 0)),
                  vec(2 * S5_STATE), vec(2 * S5_STATE), vec(1), mat, mat, mat, mat, vec(width)],
        out_specs=pl.BlockSpec((1, rows, width), lambda i: (i, 0, 0)),
        scratch_shapes=[pltpu.VMEM((width, width + 4 * S5_STATE), BF16)],
        compiler_params=_params("arbitrary"),
        name="s5_core",
    )(u_t, a_re2, a_im2, ldt, bt_re2, bt_im2, c_re2, c_im2, d_t)


def _s5_mixer(h, res_g, res_b, layer, w_in, a_re, a_im, log_dt, b_re, b_im, c_re, c_im, d, w_glu, w_out,
              *, batch, seq):
    t = batch * seq
    nc = seq // S5_CHUNK
    u = _linear(h, w_in, layer, tm=512, out_dtype=BF16)
    u_t = u.reshape(batch, nc, S5_CHUNK, S5_GROUPS, S5_GROUP).transpose(3, 1, 0, 2, 4)
    u_t = u_t.reshape(S5_GROUPS, nc * batch, S5_CHUNK * S5_GROUP)
    z_t = _s5_core(u_t, a_re, a_im, log_dt, b_re, b_im, c_re, c_im, d, batch=batch, n_chunks=nc)
    z = z_t.reshape(S5_GROUPS, nc, batch, S5_CHUNK, S5_GROUP).transpose(2, 1, 3, 0, 4)
    z = z.reshape(t, D_MODEL)
    zg = _linear_glu(z, w_glu, layer, tm=512)
    return _linear_ln(zg, w_out, layer, h, res_g, res_b, tm=256)


def _dsa_proj_kernel(x_ref, w_ref, qn_ref, kn_ref, cq_ref, ckv_ref, ckvt_ref, kidx_ref, widxt_ref):
    xb = x_ref[...].astype(BF16)

    def rms(v, g):
        return v * lax.rsqrt(jnp.mean(v * v, axis=-1, keepdims=True) + RMS_EPS) * g

    o1 = Q_LORA + KV_LORA
    cq = jnp.dot(xb, w_ref[:, 0:Q_LORA], preferred_element_type=F32)
    cq_ref[...] = rms(cq, qn_ref[...]).astype(BF16)
    ckv = rms(jnp.dot(xb, w_ref[:, Q_LORA:o1], preferred_element_type=F32), kn_ref[...])
    ckv_ref[...] = ckv.astype(BF16)
    ckvt_ref[...] = ckv.T.astype(BF16)
    kidx_ref[...] = jnp.dot(xb, w_ref[:, o1:o1 + IDX_DIM], preferred_element_type=F32).astype(BF16)
    widx = jnp.dot(xb, w_ref[:, o1 + IDX_DIM:o1 + IDX_DIM + LANES], preferred_element_type=F32)
    widxt_ref[...] = (widx * (IDX_HEADS ** -0.5)).T[0:IDX_HEADS, :]


def _dsa_proj(h, w_in_pad, layer, q_norm, kv_norm, *, batch, seq, tm=512):
    t = h.shape[0]
    per_b = seq // tm
    row = lambda w: pl.BlockSpec((tm, w), lambda i: (i, 0))
    return pl.pallas_call(
        _dsa_proj_kernel,
        out_shape=(jax.ShapeDtypeStruct((t, Q_LORA), BF16), jax.ShapeDtypeStruct((t, KV_LORA), BF16),
                   jax.ShapeDtypeStruct((batch, KV_LORA, seq), BF16),
                   jax.ShapeDtypeStruct((t, IDX_DIM), BF16), jax.ShapeDtypeStruct((IDX_HEADS, t), F32)),
        grid=(t // tm,),
        in_specs=[row(D_MODEL), _layer_spec(w_in_pad, layer),
                  pl.BlockSpec((1, Q_LORA), lambda i: (0, 0)), pl.BlockSpec((1, KV_LORA), lambda i: (0, 0))],
        out_specs=(row(Q_LORA), row(KV_LORA),
                   pl.BlockSpec((None, KV_LORA, tm), lambda i: (i // per_b, 0, i % per_b)),
                   row(IDX_DIM), pl.BlockSpec((IDX_HEADS, tm), lambda i: (0, i))),
        compiler_params=_params("parallel"),
        name="dsa_proj",
    )(h, w_in_pad, q_norm.reshape(1, Q_LORA), kv_norm.reshape(1, KV_LORA))


def _reduce_rows(op, x):
    rows, lanes = x.shape
    part = op(x.reshape(rows // 64, 8, 8, lanes), axis=0)
    return op(op(part, axis=0), axis=0, keepdims=True)


def _indexer_kernel(q_ref, k_ref, w_ref, s_ref, thr_ref, key_scr, *, seq):
    jb = pl.program_id(1)
    n_sel = float(min(IDX_TOPK, seq // 4))
    int_min = jnp.int32(-2 ** 31)
    flip = jnp.int32(0x7FFFFFFF)

    def run(width):
        k = k_ref[0:width, :]
        wrow = w_ref[...] * (IDX_DIM ** -0.5)
        acc = jnp.zeros((width, Q_BLOCK), F32)
        for g in range(IDX_HEADS // 2):
            qg = jnp.concatenate([q_ref[:, h * IDX_DIM:(h + 1) * IDX_DIM] for h in (2 * g, 2 * g + 1)], axis=0)
            s = lax.dot_general(k, qg, (((1,), (1,)), ((), ())), preferred_element_type=F32)
            acc = (acc + jnp.maximum(s[:, 0:Q_BLOCK], 0.0) * wrow[2 * g:2 * g + 1, :]
                   + jnp.maximum(s[:, Q_BLOCK:2 * Q_BLOCK], 0.0) * wrow[2 * g + 1:2 * g + 2, :])
        k_pos = lax.broadcasted_iota(jnp.int32, (width, Q_BLOCK), 0)
        q_pos = jb * Q_BLOCK + lax.broadcasted_iota(jnp.int32, (width, Q_BLOCK), 1)
        acc = jnp.where(k_pos <= q_pos, acc, -jnp.inf)
        s_ref[0:width, :] = acc
        if width < seq:
            s_ref[width:seq, :] = jnp.full((seq - width, Q_BLOCK), -jnp.inf, F32)

        bits = lax.bitcast_convert_type(acc, jnp.int32)
        key_scr[0:width, :] = jnp.where(bits < 0, bits ^ flip, bits)

        def count_ge(cand):
            return _reduce_rows(jnp.sum, (key_scr[0:width, :] >= cand).astype(F32))

        lo = jnp.where(count_ge(jnp.zeros((1, Q_BLOCK), jnp.int32)) >= n_sel, jnp.int32(0), int_min)

        def body(i, lo):
            cand = lo + jnp.left_shift(jnp.int32(1), jnp.int32(30) - i)
            return jnp.where(count_ge(cand) >= n_sel, cand, lo)

        lo = lax.fori_loop(0, 31, body, lo)
        thr = lax.bitcast_convert_type(jnp.where(lo < 0, lo ^ flip, lo), F32)
        thr_ref[...] = jnp.broadcast_to(thr, thr_ref.shape)

    n_cls = seq // KEY_CHUNK
    per_cls = KEY_CHUNK // Q_BLOCK
    for cls in range(n_cls):
        pl.when(jb // per_cls == cls)(functools.partial(run, KEY_CHUNK * (cls + 1)))


def _indexer(qcat, kidx, widx_t, *, batch, seq):
    n_blk = seq // Q_BLOCK
    return pl.pallas_call(
        functools.partial(_indexer_kernel, seq=seq),
        out_shape=(jax.ShapeDtypeStruct((batch, n_blk, seq, Q_BLOCK), F32),
                   jax.ShapeDtypeStruct((batch, n_blk, 8, Q_BLOCK), F32)),
        grid=(batch, n_blk),
        in_specs=[pl.BlockSpec((Q_BLOCK, IDX_HEADS * IDX_DIM), lambda b, j: (b * n_blk + j, 1)),
                  pl.BlockSpec((seq, IDX_DIM), lambda b, j: (b, 0)),
                  pl.BlockSpec((IDX_HEADS, Q_BLOCK), lambda b, j: (0, b * n_blk + j))],
        out_specs=(pl.BlockSpec((None, None, seq, Q_BLOCK), lambda b, j: (b, j, 0, 0)),
                   pl.BlockSpec((None, None, 8, Q_BLOCK), lambda b, j: (b, j, 0, 0))),
        scratch_shapes=[pltpu.VMEM((seq, Q_BLOCK), jnp.int32)],
        compiler_params=_params("parallel", "parallel"),
        name="indexer",
    )(qcat, kidx, widx_t)


def _attn_kernel(q_ref, kv_ref, kvt_ref, s_ref, thr_ref, wuk_ref, wuvt_ref, bd_ref, bp_ref, o_ref,
                 ql_scr, m_scr, l_scr, acc_scr):
    jb = pl.program_id(1)
    kc = pl.program_id(2)
    last = (jb * Q_BLOCK) // KEY_CHUNK
    hd, qb = ATT_HEAD_DIM, Q_BLOCK
    sub = KEY_CHUNK // qb
    pair = 2 * qb

    @pl.when(kc == 0)
    def _():
        for h in range(ATT_HEADS):
            ql = jnp.dot(q_ref[:, h * hd:(h + 1) * hd], wuk_ref[h], preferred_element_type=F32)
            ql_scr[h * qb:(h + 1) * qb, :] = (ql * (hd ** -0.5)).astype(BF16)
        m_scr[...] = jnp.full(m_scr.shape, MASKED, F32)
        l_scr[...] = jnp.zeros(l_scr.shape, F32)
        acc_scr[...] = jnp.zeros(acc_scr.shape, F32)

    def attend(near):
        kv = kv_ref[...]
        kvt = kvt_ref[...]
        k_pos = kc * KEY_CHUNK + lax.broadcasted_iota(jnp.int32, (KEY_CHUNK, qb), 0)
        q_pos = jb * qb + lax.broadcasted_iota(jnp.int32, (KEY_CHUNK, qb), 1)
        keep = (s_ref[...] >= thr_ref[0:1, :]) & (k_pos <= q_pos)
        mask_add = jnp.where(keep, 0.0, MASKED)
        mask_add = jnp.concatenate([mask_add, mask_add], axis=1)
        for g in range(ATT_HEADS // 2):
            s = lax.dot_general(kv, ql_scr[g * pair:(g + 1) * pair, :], (((1,), (1,)), ((), ())),
                                preferred_element_type=F32)
            lg = s + mask_add
            if near:
                cols = []
                for h in (2 * g, 2 * g + 1):
                    tiles = []
                    for c in range(sub):
                        gblk = kc * sub + c
                        tiles.append(jnp.where(gblk == jb, bd_ref[h], jnp.where(gblk == jb - 1, bp_ref[h], 0.0)))
                    cols.append(jnp.concatenate(tiles, axis=0))
                lg = lg + jnp.concatenate(cols, axis=1)
            m_prev = m_scr[g, 0:1, :]
            m_new = jnp.maximum(m_prev, _reduce_rows(jnp.max, lg))
            corr = jnp.exp(m_prev - m_new)
            pr = jnp.exp(lg - m_new)
            l_scr[g] = jnp.broadcast_to(corr * l_scr[g, 0:1, :] + _reduce_rows(jnp.sum, pr), (8, pair))
            m_scr[g] = jnp.broadcast_to(m_new, (8, pair))
            pv = jnp.dot(kvt, pr.astype(BF16), preferred_element_type=F32)
            acc_scr[g] = acc_scr[g] * corr + pv

    near = kc * sub + sub >= jb
    pl.when((kc <= last) & near)(functools.partial(attend, True))
    pl.when((kc <= last) & jnp.logical_not(near))(functools.partial(attend, False))

    @pl.when(kc == last)
    def _():
        for h in range(ATT_HEADS):
            g, half = h // 2, (h % 2) * qb
            ol = (acc_scr[g, :, half:half + qb] / l_scr[g, 0:1, half:half + qb]).astype(BF16)
            oh = jnp.dot(wuvt_ref[h], ol, preferred_element_type=F32)
            o_ref[:, h * hd:(h + 1) * hd] = oh.T.astype(o_ref.dtype)


def _rel_bucket_table():
    n = np.arange(2 * Q_BLOCK)
    max_exact = REL_BUCKETS // 2
    nf = np.maximum(n, 1).astype(np.float32)
    large = max_exact + (np.log(nf / np.float32(max_exact)) / np.float32(math.log(REL_MAX_DIST / max_exact))
                         * np.float32(REL_BUCKETS - max_exact)).astype(np.int32)
    large = np.minimum(large, REL_BUCKETS - 1)
    return np.where(n < max_exact, n, large)


def _attention(qcat, ckv, ckv_t, scores, thr, w_uk, w_uv_t, layer, rel_bias, *, batch, seq):
    t = batch * seq
    n_blk = seq // Q_BLOCK
    n_kc = seq // KEY_CHUNK
    bucket = _rel_bucket_table()
    assert bucket[Q_BLOCK + 1:].min() == REL_BUCKETS - 1
    kk = np.arange(Q_BLOCK)[:, None]
    qq = np.arange(Q_BLOCK)[None, :]
    rb = rel_bias.astype(F32)
    rb = rb - rb[REL_BUCKETS - 1]
    diag = jnp.where(jnp.asarray(kk <= qq)[None], jnp.moveaxis(rb[bucket[np.maximum(qq - kk, 0)]], -1, 0), 0.0)
    prev = jnp.moveaxis(rb[bucket[Q_BLOCK + qq - kk]], -1, 0)
    n_pair = ATT_HEADS // 2
    pair = 2 * Q_BLOCK

    def chunk(jj, kc):
        return jnp.minimum(kc, (jj * Q_BLOCK) // KEY_CHUNK)

    qrow = lambda b, jj, kc: (b * n_blk + jj, 0)
    full3 = lambda b, jj, kc: (0, 0, 0)
    return pl.pallas_call(
        _attn_kernel,
        out_shape=jax.ShapeDtypeStruct((t, ATT_HEADS * ATT_HEAD_DIM), BF16),
        grid=(batch, n_blk, n_kc),
        in_specs=[pl.BlockSpec((Q_BLOCK, ATT_HEADS * ATT_HEAD_DIM), qrow),
                  pl.BlockSpec((KEY_CHUNK, KV_LORA), lambda b, jj, kc: (b * n_kc + chunk(jj, kc), 0)),
                  pl.BlockSpec((None, KV_LORA, KEY_CHUNK), lambda b, jj, kc: (b, 0, chunk(jj, kc))),
                  pl.BlockSpec((None, None, KEY_CHUNK, Q_BLOCK), lambda b, jj, kc: (b, jj, chunk(jj, kc), 0)),
                  pl.BlockSpec((None, None, 8, Q_BLOCK), lambda b, jj, kc: (b, jj, 0, 0)),
                  _layer_spec(w_uk, layer),
                  _layer_spec(w_uv_t, layer),
                  pl.BlockSpec((ATT_HEADS, Q_BLOCK, Q_BLOCK), full3),
                  pl.BlockSpec((ATT_HEADS, Q_BLOCK, Q_BLOCK), full3)],
        out_specs=pl.BlockSpec((Q_BLOCK, ATT_HEADS * ATT_HEAD_DIM), qrow),
        scratch_shapes=[pltpu.VMEM((ATT_HEADS * Q_BLOCK, KV_LORA), BF16),
                        pltpu.VMEM((n_pair, 8, pair), F32),
                        pltpu.VMEM((n_pair, 8, pair), F32),
                        pltpu.VMEM((n_pair, KV_LORA, pair), F32)],
        compiler_params=_params("parallel", "parallel", "arbitrary"),
        name="latent_attention",
    )(qcat, ckv, ckv_t, scores, thr, w_uk, w_uv_t, diag, prev)


def _dsa_mixer(h, res_g, res_b, layer, rel_bias, w_in_pad, q_norm, kv_norm, w_q, w_uk, w_uv_t, w_out,
               *, batch, seq):
    cq, ckv, ckv_t, kidx, widx_t = _dsa_proj(h, w_in_pad, layer, q_norm, kv_norm, batch=batch, seq=seq)
    qcat = _linear(cq, w_q, layer, tm=512, out_dtype=BF16)
    scores, thr = _indexer(qcat, kidx, widx_t, batch=batch, seq=seq)
    o = _attention(qcat, ckv, ckv_t, scores, thr, w_uk, w_uv_t, layer, rel_bias, batch=batch, seq=seq)
    return _linear_ln(o, w_out, layer, h, res_g, res_b, tm=256)


def _router_kernel(x_ref, w_ref, b_ref, route_ref, cnt_ref, cnt_scr):
    x = x_ref[...]
    x_hi = x.astype(BF16)
    x_lo = (x - x_hi.astype(F32)).astype(BF16)
    logits = (jnp.dot(x_hi, w_ref[0], preferred_element_type=F32)
              + jnp.dot(x_lo, w_ref[0], preferred_element_type=F32)
              + jnp.dot(x_hi, w_ref[1], preferred_element_type=F32)) + b_ref[...]
    lane = lax.broadcasted_iota(jnp.int32, logits.shape, 1)
    big = jnp.int32(LANES)
    is_group = (lane >= MOE_EXPERTS) & (lane < MOE_EXPERTS + MOE_GROUPS)
    gl = jnp.where(is_group, logits, -jnp.inf)
    gm = jnp.max(gl, axis=1, keepdims=True)
    g_p = 1.0 / jnp.sum(jnp.exp(gl - gm), axis=1, keepdims=True)
    g_idx = jnp.min(jnp.where(gl == gm, lane, big), axis=1, keepdims=True) - MOE_EXPERTS
    in_group = (lane < MOE_EXPERTS) & (jnp.right_shift(lane, 3) == g_idx)
    el = jnp.where(in_group, logits, -jnp.inf)
    em = jnp.max(el, axis=1, keepdims=True)
    ee = jnp.exp(el - em)
    prob = ee / jnp.sum(ee, axis=1, keepdims=True)
    p1 = jnp.max(prob, axis=1, keepdims=True)
    i1 = jnp.min(jnp.where(in_group & (prob == p1), lane, big), axis=1, keepdims=True)
    rest = in_group & (lane != i1)
    p2 = jnp.max(jnp.where(rest, prob, -1.0), axis=1, keepdims=True)
    i2 = jnp.min(jnp.where(rest & (prob == p2), lane, big), axis=1, keepdims=True)
    den = p1 + p2
    g1 = g_p * (p1 / den)
    g2 = g_p * (p2 / den)

    @pl.when(pl.program_id(0) == 0)
    def _():
        cnt_scr[...] = jnp.zeros(cnt_scr.shape, F32)

    tm = logits.shape[0]
    oh1 = (lane == i1).astype(BF16)
    oh2 = (lane == i2).astype(BF16)
    r_i = lax.broadcasted_iota(jnp.int32, (tm, tm), 0)
    c_i = lax.broadcasted_iota(jnp.int32, (tm, tm), 1)
    tri = (c_i < r_i).astype(BF16)
    pre1 = jnp.dot(tri, oh1, preferred_element_type=F32)
    pre2 = jnp.dot(tri, oh2, preferred_element_type=F32)
    tot1 = jnp.sum(oh1.astype(F32), axis=0, keepdims=True)
    tot2 = jnp.sum(oh2.astype(F32), axis=0, keepdims=True)
    base = cnt_scr[...]
    rank1 = jnp.sum(jnp.where(lane == i1, base + pre1, 0.0), axis=1, keepdims=True)
    rank2 = jnp.sum(jnp.where(lane == i2, base + tot1 + pre2, 0.0), axis=1, keepdims=True)
    cnt_scr[...] = base + tot1 + tot2
    cnt_ref[...] = jnp.broadcast_to(cnt_scr[...], cnt_ref.shape)

    out = jnp.where(lane == 0, i1.astype(F32), jnp.where(lane == 1, i2.astype(F32), 0.0))
    out = jnp.where(lane == 2, g1, jnp.where(lane == 3, g2, out))
    out = jnp.where(lane == 4, rank1, jnp.where(lane == 5, rank2, out))
    route_ref[...] = out


def _router(h, w_r, b_r, *, tm=512):
    t = h.shape[0]
    return pl.pallas_call(
        _router_kernel,
        out_shape=(jax.ShapeDtypeStruct((t, LANES), F32), jax.ShapeDtypeStruct((8, LANES), F32)),
        grid=(t // tm,),
        in_specs=[pl.BlockSpec((tm, D_MODEL), lambda i: (i, 0)),
                  pl.BlockSpec((2, D_MODEL, LANES), lambda i: (0, 0, 0)),
                  pl.BlockSpec((1, LANES), lambda i: (0, 0))],
        out_specs=(pl.BlockSpec((tm, LANES), lambda i: (i, 0)),
                   pl.BlockSpec((8, LANES), lambda i: (0, 0))),
        scratch_shapes=[pltpu.VMEM((1, LANES), F32)],
        compiler_params=_params("arbitrary"),
        name="moe_router",
    )(h, w_r, b_r)


def _moe_expert_kernel(te_ref, nu_ref, rt_ref, x_hbm, wg_ref, wu_ref, wd_ref, y_ref,
                       xbuf, sem, wgb, wub, wdb):
    i = pl.program_id(0)
    n_used = nu_ref[0]
    slot = lax.rem(i, 2)
    tme = EXPERT_TILE
    pr = PACK_ROWS

    def start_gather(tile, buf):
        def body(r, carry):
            src = pl.multiple_of(rt_ref[tile * tme + r] * pr, pr)
            dst = pl.multiple_of(r * pr, pr)
            pltpu.make_async_copy(x_hbm.at[pl.ds(src, pr)], xbuf.at[buf, pl.ds(dst, pr)], sem.at[buf]).start()
            return carry
        lax.fori_loop(0, tme, body, 0, unroll=8)

    @pl.when(i == 0)
    def _():
        start_gather(0, 0)

    @pl.when(i + 1 < n_used)
    def _():
        start_gather(i + 1, 1 - slot)

    te = te_ref[i]
    prev = te_ref[jnp.maximum(i - 1, 0)]

    @pl.when((i == 0) | (te != prev))
    def _():
        wgb[...] = wg_ref[...].astype(BF16)
        wub[...] = wu_ref[...].astype(BF16)
        wdb[...] = wd_ref[...].astype(BF16)

    @pl.when(i < n_used)
    def _():
        pltpu.make_async_copy(x_hbm.at[pl.ds(0, tme * pr)], xbuf.at[slot], sem.at[slot]).wait()
        xb = _load_packed(xbuf.at[slot], tme).astype(BF16)
        gp = jnp.dot(xb, wgb[...], preferred_element_type=F32)
        up = jnp.dot(xb, wub[...], preferred_element_type=F32)
        hidden = (gp * jax.nn.sigmoid(gp)) * up
        _store_packed(y_ref, jnp.dot(hidden.astype(BF16), wdb[...], preferred_element_type=F32))

    @pl.when(i >= n_used)
    def _():
        y_ref[...] = jnp.zeros(y_ref.shape, y_ref.dtype)


def _moe_experts(tile_expert, n_used, row_token, h_packed, w_gate, w_up, w_down, layer):
    n_tiles = tile_expert.shape[0]
    rows = EXPERT_TILE * PACK_ROWS
    wspec = lambda a, b: pl.BlockSpec((None, None, a, b), lambda i, te, nu, rt: (layer, te[i], 0, 0))
    return pl.pallas_call(
        _moe_expert_kernel,
        out_shape=jax.ShapeDtypeStruct((n_tiles * rows, LANES), jnp.uint32),
        grid_spec=pltpu.PrefetchScalarGridSpec(
            num_scalar_prefetch=3,
            grid=(n_tiles,),
            in_specs=[pl.BlockSpec(memory_space=pl.ANY),
                      wspec(D_MODEL, MOE_FF), wspec(D_MODEL, MOE_FF), wspec(MOE_FF, D_MODEL)],
            out_specs=pl.BlockSpec((rows, LANES), lambda i, te, nu, rt: (i, 0)),
            scratch_shapes=[pltpu.VMEM((2, rows, LANES), jnp.uint32),
                            pltpu.SemaphoreType.DMA((2,)),
                            pltpu.VMEM((D_MODEL, MOE_FF), BF16),
                            pltpu.VMEM((D_MODEL, MOE_FF), BF16),
                            pltpu.VMEM((MOE_FF, D_MODEL), BF16)]),
        compiler_params=_params("arbitrary"),
        name="moe_experts",
    )(tile_expert, n_used, row_token, h_packed, w_gate, w_up, w_down)


def _moe_combine_kernel(slot_ref, h_ref, route_ref, y_hbm, g_ref, b_ref, o_ref, ybuf, sem):
    i = pl.program_id(0)
    n = pl.num_programs(0)
    slot = lax.rem(i, 2)
    tm = h_ref.shape[0]
    pr = PACK_ROWS

    def start_gather(tile, buf):
        def body(r, carry):
            for k in range(2):
                src = pl.multiple_of(slot_ref[(tile * tm + r) * 2 + k] * pr, pr)
                dst = pl.multiple_of((k * tm + r) * pr, pr)
                pltpu.make_async_copy(y_hbm.at[pl.ds(src, pr)], ybuf.at[buf, pl.ds(dst, pr)], sem.at[buf]).start()
            return carry
        lax.fori_loop(0, tm, body, 0, unroll=4)

    @pl.when(i == 0)
    def _():
        start_gather(0, 0)

    @pl.when(i + 1 < n)
    def _():
        start_gather(i + 1, 1 - slot)

    pltpu.make_async_copy(y_hbm.at[pl.ds(0, 2 * tm * pr)], ybuf.at[slot], sem.at[slot]).wait()
    y1 = _load_packed(ybuf.at[slot, pl.ds(0, tm * pr)], tm)
    y2 = _load_packed(ybuf.at[slot, pl.ds(tm * pr, tm * pr)], tm)
    ffn = route_ref[:, 2:3] * y1 + route_ref[:, 3:4] * y2
    o_ref[...] = _layer_norm_rows(DN_ALPHA * h_ref[...] + ffn, g_ref[...], b_ref[...])


def _moe_combine(slots, h, route, y, ln_g, ln_b, *, tm):
    t = h.shape[0]
    row = lambda w: pl.BlockSpec((tm, w), lambda i, s: (i, 0))
    vec = pl.BlockSpec((1, D_MODEL), lambda i, s: (0, 0))
    return pl.pallas_call(
        _moe_combine_kernel,
        out_shape=jax.ShapeDtypeStruct((t, D_MODEL), F32),
        grid_spec=pltpu.PrefetchScalarGridSpec(
            num_scalar_prefetch=1,
            grid=(t // tm,),
            in_specs=[row(D_MODEL), row(LANES), pl.BlockSpec(memory_space=pl.ANY), vec, vec],
            out_specs=row(D_MODEL),
            scratch_shapes=[pltpu.VMEM((2, 2 * tm * PACK_ROWS, LANES), jnp.uint32),
                            pltpu.SemaphoreType.DMA((2,))]),
        compiler_params=_params("arbitrary"),
        name="moe_combine",
    )(slots, h, route, y, ln_g.reshape(1, D_MODEL), ln_b.reshape(1, D_MODEL))


def _moe_plan(route, counts, n_tiles):
    t = route.shape[0]
    tme = EXPERT_TILE
    cnt = counts[0, :MOE_EXPERTS].astype(jnp.int32)
    tiles = (cnt + tme - 1) // tme
    tile_end = jnp.cumsum(tiles)
    tile_start = tile_end - tiles
    n_used = tile_end[-1]
    choice = route[:, 0:2].astype(jnp.int32)
    rank = route[:, 4:6].astype(jnp.int32)
    slots = (tile_start[choice] * tme + rank).reshape(-1)
    tile_ids = jnp.arange(n_tiles, dtype=jnp.int32)
    te = jnp.sum((tile_ids[:, None] >= tile_end[None, :]).astype(jnp.int32), axis=1)
    last_e = jnp.sum((n_used - 1 >= tile_end).astype(jnp.int32))
    te = jnp.where(tile_ids < n_used, te, last_e)
    row_token = jnp.zeros((n_tiles * tme,), jnp.int32).at[slots].set(jnp.arange(2 * t, dtype=jnp.int32) // 2)
    return te, n_used.reshape(1).astype(jnp.int32), row_token, slots.astype(jnp.int32)


def _hier_moe(h, h_packed, layer, w_group, b_group, w_expert, b_expert, w_gate, w_up, w_down, ln_g, ln_b):
    t = h.shape[0]
    pad = LANES - MOE_EXPERTS - MOE_GROUPS
    w_r = jnp.concatenate([w_expert, w_group, jnp.zeros((D_MODEL, pad), F32)], axis=1)
    w_hi = w_r.astype(BF16)
    w_r = jnp.stack([w_hi, (w_r - w_hi.astype(F32)).astype(BF16)])
    b_r = jnp.concatenate([b_expert, b_group, jnp.zeros((pad,), F32)]).reshape(1, LANES)
    route, counts = _router(h, w_r, b_r)
    n_tiles = (2 * t) // EXPERT_TILE + MOE_EXPERTS
    te, n_used, row_token, slots = _moe_plan(route, counts, n_tiles)
    y = _moe_experts(te, n_used, row_token, h_packed, w_gate, w_up, w_down, layer)
    return _moe_combine(slots, h, route, y, ln_g, ln_b, tm=COMBINE_TILE)


def kernel(x, rel_bias, s5_w_in, s5_a_re, s5_a_im, s5_log_dt, s5_b_re, s5_b_im, s5_c_re, s5_c_im, s5_d, s5_w_glu, s5_w_out, dsa_w_in, dsa_q_norm, dsa_kv_norm, dsa_w_uq, dsa_w_qidx, dsa_w_uk, dsa_w_uv, dsa_w_out, moe_w_group, moe_b_group, moe_w_expert, moe_b_expert, moe_w_gate, moe_w_up, moe_w_down, ln_mix_g, ln_mix_b, ln_ffn_g, ln_ffn_b):
    batch, seq, dm = x.shape
    h = x.reshape(batch * seq, dm)
    s5_w_in, s5_w_glu, s5_w_out = (w.astype(BF16) for w in (s5_w_in, s5_w_glu, s5_w_out))
    n_pad = Q_LORA + KV_LORA + IDX_DIM + LANES - dsa_w_in.shape[-1]
    dsa_w_in_pad = jnp.pad(dsa_w_in, ((0, 0), (0, 0), (0, n_pad))).astype(BF16)
    dsa_w_q = jnp.concatenate([dsa_w_uq, dsa_w_qidx], axis=-1).astype(BF16)
    dsa_w_uk, dsa_w_out = dsa_w_uk.astype(BF16), dsa_w_out.astype(BF16)
    dsa_w_uv_t = jnp.swapaxes(dsa_w_uv, -1, -2).astype(BF16)
    for i in range(DEPTH):
        j = i // 2
        if i % 2 == 0:
            h, hp = _s5_mixer(h, ln_mix_g[i], ln_mix_b[i], j, s5_w_in, s5_a_re[j], s5_a_im[j], s5_log_dt[j],
                              s5_b_re[j], s5_b_im[j], s5_c_re[j], s5_c_im[j], s5_d[j], s5_w_glu, s5_w_out,
                              batch=batch, seq=seq)
        else:
            h, hp = _dsa_mixer(h, ln_mix_g[i], ln_mix_b[i], j, rel_bias, dsa_w_in_pad, dsa_q_norm[j],
                               dsa_kv_norm[j], dsa_w_q, dsa_w_uk, dsa_w_uv_t, dsa_w_out, batch=batch, seq=seq)
        h = _hier_moe(h, hp, i, moe_w_group[i], moe_b_group[i], moe_w_expert[i], moe_b_expert[i],
                      moe_w_gate, moe_w_up, moe_w_down, ln_ffn_g[i], ln_ffn_b[i])
    return h.reshape(batch, seq, dm)
```

```python
import functools
import math

import numpy as np
import jax
import jax.numpy as jnp
from jax import lax
from jax.experimental import pallas as pl
from jax.experimental.pallas import tpu as pltpu

F32 = jnp.float32
BF16 = jnp.bfloat16

D_MODEL = 2048
DEPTH = 4
S5_GROUP = 16
S5_GROUPS = D_MODEL // S5_GROUP
S5_STATE = 64
S5_CHUNK = 64
ATT_HEADS = 16
ATT_HEAD_DIM = 128
Q_LORA = 512
KV_LORA = 512
IDX_HEADS = 16
IDX_DIM = 128
IDX_TOPK = 256
Q_BLOCK = 128
KEY_CHUNK = 512
REL_BUCKETS = 32
REL_MAX_DIST = 128
MOE_GROUPS = 4
MOE_PER_GROUP = 8
MOE_EXPERTS = 32
MOE_FF = 256
EXPERT_TILE = 256
COMBINE_TILE = 256
DN_ALPHA = (2 * DEPTH) ** 0.25
LN_EPS = 1e-5
RMS_EPS = 1e-6

LANES = 128
MASKED = -1e30
VMEM_LIMIT = 56 * 1024 * 1024


def _params(*sem):
    return pltpu.CompilerParams(dimension_semantics=sem, vmem_limit_bytes=VMEM_LIMIT)


def _layer_norm_rows(y, g, b):
    mean = jnp.mean(y, axis=-1, keepdims=True)
    yc = y - mean
    var = jnp.mean(yc * yc, axis=-1, keepdims=True)
    return yc * lax.rsqrt(var + LN_EPS) * g + b


def _mm_plain_kernel(x_ref, w_ref, o_ref, *, tn):
    xb = x_ref[...].astype(BF16)
    for n0 in range(0, o_ref.shape[1], tn):
        acc = jnp.dot(xb, w_ref[:, n0:n0 + tn], preferred_element_type=F32)
        o_ref[:, n0:n0 + tn] = acc.astype(o_ref.dtype)


def _mm_glu_kernel(z_ref, w_ref, o_ref, *, tn):
    zb = z_ref[...]
    for n0 in range(0, o_ref.shape[1], tn):
        acc = jnp.dot(zb, w_ref[:, n0:n0 + tn], preferred_element_type=F32)
        zc = z_ref[:, n0:n0 + tn].astype(F32)
        o_ref[:, n0:n0 + tn] = (zc * jax.nn.sigmoid(acc)).astype(o_ref.dtype)


PACK_ROWS = D_MODEL // (2 * LANES)


def _store_packed(o_ref, y):
    half = y.shape[1] // 2
    bits = lambda v: lax.bitcast_convert_type(v.astype(BF16).astype(F32), jnp.uint32)
    word = (bits(y[:, half:]) & jnp.uint32(0xFFFF0000)) | (bits(y[:, :half]) >> 16)
    rows = y.shape[0]
    for c in range(PACK_ROWS):
        o_ref[pl.ds(c, rows, stride=PACK_ROWS), :] = word[:, c * LANES:(c + 1) * LANES]


def _load_packed(x_ref, rows):
    word = jnp.concatenate([x_ref[pl.ds(c, rows, stride=PACK_ROWS), :] for c in range(PACK_ROWS)], axis=1)
    lo = lax.bitcast_convert_type(word << 16, F32)
    hi = lax.bitcast_convert_type(word & jnp.uint32(0xFFFF0000), F32)
    return jnp.concatenate([lo, hi], axis=1)


def _mm_ln_kernel(x_ref, w_ref, res_ref, g_ref, b_ref, o_ref, op_ref, *, tn):
    xb = x_ref[...].astype(BF16)
    for n0 in range(0, o_ref.shape[1], tn):
        acc = jnp.dot(xb, w_ref[:, n0:n0 + tn], preferred_element_type=F32)
        o_ref[:, n0:n0 + tn] = DN_ALPHA * res_ref[:, n0:n0 + tn] + acc
    y = _layer_norm_rows(o_ref[...], g_ref[...], b_ref[...])
    o_ref[...] = y
    _store_packed(op_ref, y)


def _layer_spec(w_all, layer):
    shape = w_all.shape[1:]
    return pl.BlockSpec((None,) + shape, lambda *_: (layer,) + (0,) * len(shape))


def _linear(x, w_all, layer, *, tm, out_dtype, tn=512):
    m, k = x.shape
    n = w_all.shape[-1]
    return pl.pallas_call(
        functools.partial(_mm_plain_kernel, tn=tn),
        out_shape=jax.ShapeDtypeStruct((m, n), out_dtype),
        grid=(m // tm,),
        in_specs=[pl.BlockSpec((tm, k), lambda i: (i, 0)), _layer_spec(w_all, layer)],
        out_specs=pl.BlockSpec((tm, n), lambda i: (i, 0)),
        compiler_params=_params("parallel"),
        name="linear",
    )(x, w_all)


def _linear_glu(z, w_all, layer, *, tm, tn=512):
    m, k = z.shape
    return pl.pallas_call(
        functools.partial(_mm_glu_kernel, tn=tn),
        out_shape=jax.ShapeDtypeStruct((m, k), BF16),
        grid=(m // tm,),
        in_specs=[pl.BlockSpec((tm, k), lambda i: (i, 0)), _layer_spec(w_all, layer)],
        out_specs=pl.BlockSpec((tm, k), lambda i: (i, 0)),
        compiler_params=_params("parallel"),
        name="linear_glu",
    )(z, w_all)


def _linear_ln(x, w_all, layer, res, g, b, *, tm, tn=512):
    m, k = x.shape
    n = w_all.shape[-1]
    return pl.pallas_call(
        functools.partial(_mm_ln_kernel, tn=tn),
        out_shape=(jax.ShapeDtypeStruct((m, n), F32), jax.ShapeDtypeStruct((m * PACK_ROWS, LANES), jnp.uint32)),
        grid=(m // tm,),
        in_specs=[pl.BlockSpec((tm, k), lambda i: (i, 0)),
                  _layer_spec(w_all, layer),
                  pl.BlockSpec((tm, n), lambda i: (i, 0)),
                  pl.BlockSpec((1, n), lambda i: (0, 0)),
                  pl.BlockSpec((1, n), lambda i: (0, 0))],
        out_specs=(pl.BlockSpec((tm, n), lambda i: (i, 0)),
                   pl.BlockSpec((tm * PACK_ROWS, LANES), lambda i: (i, 0))),
        compiler_params=_params("parallel"),
        name="linear_ln",
    )(x, w_all, res, g.reshape(1, n), b.reshape(1, n))


def _gelu_tanh(x):
    return 0.5 * x * (1.0 + jnp.tanh(math.sqrt(2.0 / math.pi) * (x + 0.044715 * x * x * x)))


def _s5_kernel(u_ref, ar_ref, ai_ref, ldt_ref, br_ref, bi_ref, cr_ref, ci_ref, d_ref, o_ref, a_scr,
               *, n_chunks):
    cs, p, n = S5_CHUNK, S5_GROUP, S5_STATE
    width = cs * p
    n2 = 2 * n

    @pl.when(pl.program_id(0) == 0)
    def _():
        a_scr[...] = jnp.zeros(a_scr.shape, a_scr.dtype)

    dt = jnp.exp(ldt_ref[0])
    lam_r = jnp.minimum(ar_ref[0], -1e-4)
    lam_i = ai_ref[0]
    lo_half = lax.broadcasted_iota(jnp.int32, (1, n2), 1) < n

    def a_pow(tau):
        mag = jnp.exp(lam_r * dt * tau)
        ph = lam_i * dt * tau
        return mag * jnp.cos(ph), mag * jnp.sin(ph)

    tau = lax.broadcasted_iota(jnp.int32, (cs, 1), 0).astype(F32)
    a0r, a0i = a_pow(tau)
    a1r, a1i = a_pow(tau + 1.0)
    avr, avi = a_pow((cs - 1.0) - tau)

    abr, abi = a1r[0:1], a1i[0:1]
    den = lam_r * lam_r + lam_i * lam_i
    kr = ((abr - 1.0) * lam_r + abi * lam_i) / den
    ki = (abi * lam_r - (abr - 1.0) * lam_i) / den
    b_r, b_i = br_ref[0], bi_ref[0]
    bbr = kr * b_r - ki * b_i
    bbi = kr * b_i + ki * b_r
    c_r, c_i = cr_ref[0], ci_ref[0]

    def rep_rows(x):
        return jnp.concatenate([jnp.broadcast_to(x[t:t + 1, :], (p, n2)) for t in range(cs)], axis=0)

    def tile_rows(x):
        return jnp.concatenate([x] * cs, axis=0)

    def sel(lo, hi):
        return jnp.where(lo_half, lo, hi)

    crr, cii = tile_rows(c_r), tile_rows(c_i)
    brr, bii = tile_rows(bbr), tile_rows(bbi)
    c_taps = crr * rep_rows(sel(a0r, -a0i)) - cii * rep_rows(sel(a0i, a0r))
    c_out = crr * rep_rows(sel(a1r, -a1i)) - cii * rep_rows(sel(a1i, a1r))
    b_in = brr * rep_rows(sel(avr, avi)) + bii * rep_rows(sel(-avi, avr))
    b_in_sw = brr * rep_rows(sel(avi, avr)) + bii * rep_rows(sel(avr, -avi))

    b_mat = sel(bbr, bbi)
    taps = lax.dot_general(b_mat, c_taps, (((1,), (1,)), ((), ())),
                           precision=lax.Precision.HIGHEST, preferred_element_type=F32)

    lane = lax.broadcasted_iota(jnp.int32, (p, width), 1)
    band = [taps]
    for j in range(1, LANES // p):
        band.append(jnp.where(lane >= p * j, pltpu.roll(taps, p * j, axis=1), 0.0))
    band = jnp.concatenate(band, axis=0)
    n_blk = width // LANES
    band_t = jnp.concatenate([band[:, d * LANES:(d + 1) * LANES].T for d in reversed(range(n_blk))],
                             axis=1).astype(BF16)
    for q in range(n_blk):
        a_scr[q * LANES:(q + 1) * LANES, 0:(q + 1) * LANES] = band_t[:, (n_blk - 1 - q) * LANES:width]
    a_scr[width:width + n2, :] = b_in.T.astype(BF16)
    a_scr[width + n2:width + 2 * n2, :] = b_in_sw.T.astype(BF16)

    u = u_ref[...].reshape(width, u_ref.shape[2])
    res = jnp.dot(a_scr[...], u, preferred_element_type=F32)
    y = res[0:width]
    x_in = res[width:width + n2]
    x_sw = res[width + n2:width + 2 * n2]

    n_lanes = x_in.shape[1]
    levels = n_chunks.bit_length() - 1
    mults = []
    for k in range(levels):
        mr, mi = a_pow(jnp.full((1, 1), float(cs * 2 ** k), F32))
        mults += [mr, sel(-mi, mi), sel(mi, -mi)]
    mults = jnp.concatenate(mults + [jnp.zeros((n2 - len(mults), n2), F32)], axis=0)
    mcol = mults.T
    chunk_id = lax.rem(lax.broadcasted_iota(jnp.int32, (n2, n_lanes), 1), n_chunks)
    for k in range(levels):
        sh = 2 ** k
        ok = chunk_id >= sh
        xs = jnp.where(ok, pltpu.roll(x_in, sh, axis=1), 0.0)
        xs_sw = jnp.where(ok, pltpu.roll(x_sw, sh, axis=1), 0.0)
        m_rr, m_ni, m_pi = (mcol[:, 3 * k + j:3 * k + j + 1] for j in range(3))
        x_in, x_sw = (x_in + xs * m_rr + xs_sw * m_ni, x_sw + xs_sw * m_rr + xs * m_pi)
    h_start = jnp.where(chunk_id >= 1, pltpu.roll(x_in, 1, axis=1), 0.0).astype(BF16)

    y = y + jnp.dot(c_out.astype(BF16), h_start, preferred_element_type=F32)
    y3 = y.reshape(cs, p, n_lanes) + d_ref[0] * u_ref[...].astype(F32)
    o_ref[...] = _gelu_tanh(y3).astype(o_ref.dtype)


def _s5_core(u_t, a_re, a_im, log_dt, b_re, b_im, c_re, c_im, d, *, n_chunks):
    cs, dm, n_lanes = u_t.shape
    g = dm // S5_GROUP
    assert n_chunks & (n_chunks - 1) == 0
    dup = lambda x: jnp.concatenate([x, x], axis=-1)
    a_re2 = dup(a_re).reshape(g, 1, 2 * S5_STATE)
    a_im2 = dup(a_im).reshape(g, 1, 2 * S5_STATE)
    ldt = log_dt.reshape(g, 1, 1)
    bt_re2 = dup(jnp.swapaxes(b_re, 1, 2))
    bt_im2 = dup(jnp.swapaxes(b_im, 1, 2))
    c_re2 = dup(c_re)
    c_im2 = dup(c_im)
    d_c = d.reshape(g, S5_GROUP, 1)
    width = cs * S5_GROUP
    vec = lambda w: pl.BlockSpec((1, 1, w), lambda i: (i, 0, 0))
    mat = pl.BlockSpec((1, S5_GROUP, 2 * S5_STATE), lambda i: (i, 0, 0))
    blk = pl.BlockSpec((cs, S5_GROUP, n_lanes), lambda i: (0, i, 0))
    return pl.pallas_call(
        functools.partial(_s5_kernel, n_chunks=n_chunks),
        out_shape=jax.ShapeDtypeStruct((cs, dm, n_lanes), BF16),
        grid=(g,),
        in_specs=[blk, vec(2 * S5_STATE), vec(2 * S5_STATE), vec(1), mat, mat, mat, mat,
                  pl.BlockSpec((1, S5_GROUP, 1), lambda i: (i, 0, 0))],
        out_specs=blk,
        scratch_shapes=[pltpu.VMEM((width + 4 * S5_STATE, width), BF16)],
        compiler_params=_params("arbitrary"),
        name="s5_core",
    )(u_t, a_re2, a_im2, ldt, bt_re2, bt_im2, c_re2, c_im2, d_c)


S5_STEP = 8
LANE_ROWS = 128


def _s5_in_kernel(x_ref, w_ref, o_ref, u_scr):
    rows = LANE_ROWS * S5_STEP
    dm = x_ref.shape[2]
    xb = x_ref[...].reshape(rows, dm).astype(BF16)
    for n0 in range(0, dm, 512):
        acc = jnp.dot(xb, w_ref[:, n0:n0 + 512], preferred_element_type=F32)
        for c in range(512 // LANES):
            u_scr[n0 // LANES + c] = acc[:, c * LANES:(c + 1) * LANES]
    for j in range(S5_STEP):
        for cb in range(dm // LANES):
            tile = u_scr[cb, pl.ds(j, LANE_ROWS, stride=S5_STEP), :]
            o_ref[j, cb * LANES:(cb + 1) * LANES, :] = tile.T.astype(o_ref.dtype)


def _s5_in_proj(h3, w_all, layer):
    n_bc, cs, dm = h3.shape
    return pl.pallas_call(
        _s5_in_kernel,
        out_shape=jax.ShapeDtypeStruct((cs, dm, n_bc), BF16),
        grid=(cs // S5_STEP, n_bc // LANE_ROWS),
        in_specs=[pl.BlockSpec((LANE_ROWS, S5_STEP, dm), lambda s, r: (r, s, 0)), _layer_spec(w_all, layer)],
        out_specs=pl.BlockSpec((S5_STEP, dm, LANE_ROWS), lambda s, r: (s, 0, r)),
        scratch_shapes=[pltpu.VMEM((dm // LANES, LANE_ROWS * S5_STEP, LANES), F32)],
        compiler_params=_params("parallel", "parallel"),
        name="s5_in_proj",
    )(h3, w_all)


def _s5_glu_kernel(z_ref, w_ref, o_ref, z_scr, o_scr):
    nh = pl.program_id(2)
    half = o_ref.shape[2]

    @pl.when(nh == 0)
    def _():
        for j in range(S5_STEP):
            z_scr[j * LANE_ROWS:(j + 1) * LANE_ROWS, :] = z_ref[j].astype(F32).T

    zb = z_scr[...].astype(BF16)
    col0 = pl.multiple_of(nh * half, half)
    for n0 in range(0, half, 512):
        acc = jnp.dot(zb, w_ref[:, n0:n0 + 512], preferred_element_type=F32)
        zg = z_scr[:, pl.ds(col0 + n0, 512)] * jax.nn.sigmoid(acc)
        for j in range(S5_STEP):
            for c in range(512 // LANES):
                o_scr[n0 // LANES + c, pl.ds(j, LANE_ROWS, stride=S5_STEP), :] = (
                    zg[j * LANE_ROWS:(j + 1) * LANE_ROWS, c * LANES:(c + 1) * LANES])
    out = jnp.concatenate([o_scr[cb] for cb in range(half // LANES)], axis=1)
    o_ref[...] = out.reshape(o_ref.shape)


def _s5_glu(z_t, w_all, layer):
    cs, dm, n_bc = z_t.shape
    rows = LANE_ROWS * S5_STEP
    half = dm // 2
    return pl.pallas_call(
        _s5_glu_kernel,
        out_shape=jax.ShapeDtypeStruct((n_bc, cs, dm), F32),
        grid=(cs // S5_STEP, n_bc // LANE_ROWS, 2),
        in_specs=[pl.BlockSpec((S5_STEP, dm, LANE_ROWS), lambda s, r, nh: (s, 0, r)),
                  pl.BlockSpec((None, dm, half), lambda s, r, nh: (layer, 0, nh))],
        out_specs=pl.BlockSpec((LANE_ROWS, S5_STEP, half), lambda s, r, nh: (r, s, nh)),
        scratch_shapes=[pltpu.VMEM((rows, dm), F32), pltpu.VMEM((half // LANES, rows, LANES), F32)],
        compiler_params=_params("parallel", "parallel", "arbitrary"),
        name="s5_glu",
    )(z_t, w_all)


def _s5_mixer(h, res_g, res_b, layer, w_in, a_re, a_im, log_dt, b_re, b_im, c_re, c_im, d, w_glu, w_out,
              *, batch, seq):
    t = batch * seq
    nc = seq // S5_CHUNK
    u_t = _s5_in_proj(h.reshape(batch * nc, S5_CHUNK, D_MODEL), w_in, layer)
    z_t = _s5_core(u_t, a_re, a_im, log_dt, b_re, b_im, c_re, c_im, d, n_chunks=nc)
    zg = _s5_glu(z_t, w_glu, layer).reshape(t, D_MODEL)
    return _linear_ln(zg, w_out, layer, h, res_g, res_b, tm=256)


def _dsa_proj_kernel(x_ref, w_ref, qn_ref, kn_ref, cq_ref, ckv_ref, ckvt_ref, kidx_ref, widxt_ref):
    xb = x_ref[...].astype(BF16)

    def rms(v, g):
        return v * lax.rsqrt(jnp.mean(v * v, axis=-1, keepdims=True) + RMS_EPS) * g

    o1 = Q_LORA + KV_LORA
    cq = jnp.dot(xb, w_ref[:, 0:Q_LORA], preferred_element_type=F32)
    cq_ref[...] = rms(cq, qn_ref[...]).astype(BF16)
    ckv = rms(jnp.dot(xb, w_ref[:, Q_LORA:o1], preferred_element_type=F32), kn_ref[...])
    ckv_ref[...] = ckv.astype(BF16)
    ckvt_ref[...] = ckv.T.astype(BF16)
    kidx_ref[...] = jnp.dot(xb, w_ref[:, o1:o1 + IDX_DIM], preferred_element_type=F32).astype(BF16)
    widx = jnp.dot(xb, w_ref[:, o1 + IDX_DIM:o1 + IDX_DIM + LANES], preferred_element_type=F32)
    widxt_ref[...] = (widx * (IDX_HEADS ** -0.5)).T[0:IDX_HEADS, :]


def _dsa_proj(h, w_in_pad, layer, q_norm, kv_norm, *, batch, seq, tm=512):
    t = h.shape[0]
    per_b = seq // tm
    row = lambda w: pl.BlockSpec((tm, w), lambda i: (i, 0))
    return pl.pallas_call(
        _dsa_proj_kernel,
        out_shape=(jax.ShapeDtypeStruct((t, Q_LORA), BF16), jax.ShapeDtypeStruct((t, KV_LORA), BF16),
                   jax.ShapeDtypeStruct((batch, KV_LORA, seq), BF16),
                   jax.ShapeDtypeStruct((t, IDX_DIM), BF16), jax.ShapeDtypeStruct((IDX_HEADS, t), F32)),
        grid=(t // tm,),
        in_specs=[row(D_MODEL), _layer_spec(w_in_pad, layer),
                  pl.BlockSpec((1, Q_LORA), lambda i: (0, 0)), pl.BlockSpec((1, KV_LORA), lambda i: (0, 0))],
        out_specs=(row(Q_LORA), row(KV_LORA),
                   pl.BlockSpec((None, KV_LORA, tm), lambda i: (i // per_b, 0, i % per_b)),
                   row(IDX_DIM), pl.BlockSpec((IDX_HEADS, tm), lambda i: (0, i))),
        compiler_params=_params("parallel"),
        name="dsa_proj",
    )(h, w_in_pad, q_norm.reshape(1, Q_LORA), kv_norm.reshape(1, KV_LORA))


def _reduce_rows(op, x):
    rows, lanes = x.shape
    part = op(x.reshape(rows // 64, 8, 8, lanes), axis=0)
    return op(op(part, axis=0), axis=0, keepdims=True)


def _indexer_kernel(q_ref, k_ref, w_ref, s_ref, thr_ref, key_scr, *, seq):
    jb = pl.program_id(1)
    n_sel = float(min(IDX_TOPK, seq // 4))
    int_min = jnp.int32(-2 ** 31)
    flip = jnp.int32(0x7FFFFFFF)

    def run(width):
        k = k_ref[0:width, :]
        wrow = w_ref[...] * (IDX_DIM ** -0.5)
        acc = jnp.zeros((width, Q_BLOCK), F32)
        for g in range(IDX_HEADS // 2):
            qg = jnp.concatenate([q_ref[:, h * IDX_DIM:(h + 1) * IDX_DIM] for h in (2 * g, 2 * g + 1)], axis=0)
            s = lax.dot_general(k, qg, (((1,), (1,)), ((), ())), preferred_element_type=F32)
            acc = (acc + jnp.maximum(s[:, 0:Q_BLOCK], 0.0) * wrow[2 * g:2 * g + 1, :]
                   + jnp.maximum(s[:, Q_BLOCK:2 * Q_BLOCK], 0.0) * wrow[2 * g + 1:2 * g + 2, :])
        k_pos = lax.broadcasted_iota(jnp.int32, (width, Q_BLOCK), 0)
        q_pos = jb * Q_BLOCK + lax.broadcasted_iota(jnp.int32, (width, Q_BLOCK), 1)
        acc = jnp.where(k_pos <= q_pos, acc, -jnp.inf)
        s_ref[0:width, :] = acc
        if width < seq:
            s_ref[width:seq, :] = jnp.full((seq - width, Q_BLOCK), -jnp.inf, F32)

        bits = lax.bitcast_convert_type(acc, jnp.int32)
        key_scr[0:width, :] = jnp.where(bits < 0, bits ^ flip, bits)

        def count_ge(cand):
            return _reduce_rows(jnp.sum, (key_scr[0:width, :] >= cand).astype(F32))

        lo = jnp.where(count_ge(jnp.zeros((1, Q_BLOCK), jnp.int32)) >= n_sel, jnp.int32(0), int_min)

        def body(i, lo):
            cand = lo + jnp.left_shift(jnp.int32(1), jnp.int32(30) - i)
            return jnp.where(count_ge(cand) >= n_sel, cand, lo)

        lo = lax.fori_loop(0, 31, body, lo)
        thr = lax.bitcast_convert_type(jnp.where(lo < 0, lo ^ flip, lo), F32)
        thr_ref[...] = jnp.broadcast_to(thr, thr_ref.shape)

    n_cls = seq // KEY_CHUNK
    per_cls = KEY_CHUNK // Q_BLOCK
    for cls in range(n_cls):
        pl.when(jb // per_cls == cls)(functools.partial(run, KEY_CHUNK * (cls + 1)))


def _indexer(qcat, kidx, widx_t, *, batch, seq):
    n_blk = seq // Q_BLOCK
    return pl.pallas_call(
        functools.partial(_indexer_kernel, seq=seq),
        out_shape=(jax.ShapeDtypeStruct((batch, n_blk, seq, Q_BLOCK), F32),
                   jax.ShapeDtypeStruct((batch, n_blk, 8, Q_BLOCK), F32)),
        grid=(batch, n_blk),
        in_specs=[pl.BlockSpec((Q_BLOCK, IDX_HEADS * IDX_DIM), lambda b, j: (b * n_blk + j, 1)),
                  pl.BlockSpec((seq, IDX_DIM), lambda b, j: (b, 0)),
                  pl.BlockSpec((IDX_HEADS, Q_BLOCK), lambda b, j: (0, b * n_blk + j))],
        out_specs=(pl.BlockSpec((None, None, seq, Q_BLOCK), lambda b, j: (b, j, 0, 0)),
                   pl.BlockSpec((None, None, 8, Q_BLOCK), lambda b, j: (b, j, 0, 0))),
        scratch_shapes=[pltpu.VMEM((seq, Q_BLOCK), jnp.int32)],
        compiler_params=_params("parallel", "parallel"),
        name="indexer",
    )(qcat, kidx, widx_t)


def _attn_kernel(q_ref, kv_ref, kvt_ref, s_ref, thr_ref, wuk_ref, wuvt_ref, bd_ref, bp_ref, o_ref,
                 ql_scr, m_scr, l_scr, acc_scr):
    jb = pl.program_id(1)
    kc = pl.program_id(2)
    last = (jb * Q_BLOCK) // KEY_CHUNK
    hd, qb = ATT_HEAD_DIM, Q_BLOCK
    sub = KEY_CHUNK // qb
    pair = 2 * qb

    @pl.when(kc == 0)
    def _():
        for h in range(ATT_HEADS):
            ql = jnp.dot(q_ref[:, h * hd:(h + 1) * hd], wuk_ref[h], preferred_element_type=F32)
            ql_scr[h * qb:(h + 1) * qb, :] = (ql * (hd ** -0.5)).astype(BF16)
        m_scr[...] = jnp.full(m_scr.shape, MASKED, F32)
        l_scr[...] = jnp.zeros(l_scr.shape, F32)
        acc_scr[...] = jnp.zeros(acc_scr.shape, F32)

    def attend(near):
        kv = kv_ref[...]
        kvt = kvt_ref[...]
        k_pos = kc * KEY_CHUNK + lax.broadcasted_iota(jnp.int32, (KEY_CHUNK, qb), 0)
        q_pos = jb * qb + lax.broadcasted_iota(jnp.int32, (KEY_CHUNK, qb), 1)
        keep = (s_ref[...] >= thr_ref[0:1, :]) & (k_pos <= q_pos)
        mask_add = jnp.where(keep, 0.0, MASKED)
        mask_add = jnp.concatenate([mask_add, mask_add], axis=1)
        for g in range(ATT_HEADS // 2):
            s = lax.dot_general(kv, ql_scr[g * pair:(g + 1) * pair, :], (((1,), (1,)), ((), ())),
                                preferred_element_type=F32)
            lg = s + mask_add
            if near:
                cols = []
                for h in (2 * g, 2 * g + 1):
                    tiles = []
                    for c in range(sub):
                        gblk = kc * sub + c
                        tiles.append(jnp.where(gblk == jb, bd_ref[h], jnp.where(gblk == jb - 1, bp_ref[h], 0.0)))
                    cols.append(jnp.concatenate(tiles, axis=0))
                lg = lg + jnp.concatenate(cols, axis=1)
            m_prev = m_scr[g, 0:1, :]
            m_new = jnp.maximum(m_prev, _reduce_rows(jnp.max, lg))
            corr = jnp.exp(m_prev - m_new)
            pr = jnp.exp(lg - m_new)
            l_scr[g] = jnp.broadcast_to(corr * l_scr[g, 0:1, :] + _reduce_rows(jnp.sum, pr), (8, pair))
            m_scr[g] = jnp.broadcast_to(m_new, (8, pair))
            pv = jnp.dot(kvt, pr.astype(BF16), preferred_element_type=F32)
            acc_scr[g] = acc_scr[g] * corr + pv

    near = kc * sub + sub >= jb
    pl.when((kc <= last) & near)(functools.partial(attend, True))
    pl.when((kc <= last) & jnp.logical_not(near))(functools.partial(attend, False))

    @pl.when(kc == last)
    def _():
        for h in range(ATT_HEADS):
            g, half = h // 2, (h % 2) * qb
            ol = (acc_scr[g, :, half:half + qb] / l_scr[g, 0:1, half:half + qb]).astype(BF16)
            oh = jnp.dot(wuvt_ref[h], ol, preferred_element_type=F32)
            o_ref[:, h * hd:(h + 1) * hd] = oh.T.astype(o_ref.dtype)


def _rel_bucket_table():
    n = np.arange(2 * Q_BLOCK)
    max_exact = REL_BUCKETS // 2
    nf = np.maximum(n, 1).astype(np.float32)
    large = max_exact + (np.log(nf / np.float32(max_exact)) / np.float32(math.log(REL_MAX_DIST / max_exact))
                         * np.float32(REL_BUCKETS - max_exact)).astype(np.int32)
    large = np.minimum(large, REL_BUCKETS - 1)
    return np.where(n < max_exact, n, large)


def _attention(qcat, ckv, ckv_t, scores, thr, w_uk, w_uv_t, layer, rel_bias, *, batch, seq):
    t = batch * seq
    n_blk = seq // Q_BLOCK
    n_kc = seq // KEY_CHUNK
    bucket = _rel_bucket_table()
    assert bucket[Q_BLOCK + 1:].min() == REL_BUCKETS - 1
    kk = np.arange(Q_BLOCK)[:, None]
    qq = np.arange(Q_BLOCK)[None, :]
    rb = rel_bias.astype(F32)
    rb = rb - rb[REL_BUCKETS - 1]
    diag = jnp.where(jnp.asarray(kk <= qq)[None], jnp.moveaxis(rb[bucket[np.maximum(qq - kk, 0)]], -1, 0), 0.0)
    prev = jnp.moveaxis(rb[bucket[Q_BLOCK + qq - kk]], -1, 0)
    n_pair = ATT_HEADS // 2
    pair = 2 * Q_BLOCK

    def chunk(jj, kc):
        return jnp.minimum(kc, (jj * Q_BLOCK) // KEY_CHUNK)

    qrow = lambda b, jj, kc: (b * n_blk + jj, 0)
    full3 = lambda b, jj, kc: (0, 0, 0)
    return pl.pallas_call(
        _attn_kernel,
        out_shape=jax.ShapeDtypeStruct((t, ATT_HEADS * ATT_HEAD_DIM), BF16),
        grid=(batch, n_blk, n_kc),
        in_specs=[pl.BlockSpec((Q_BLOCK, ATT_HEADS * ATT_HEAD_DIM), qrow),
                  pl.BlockSpec((KEY_CHUNK, KV_LORA), lambda b, jj, kc: (b * n_kc + chunk(jj, kc), 0)),
                  pl.BlockSpec((None, KV_LORA, KEY_CHUNK), lambda b, jj, kc: (b, 0, chunk(jj, kc))),
                  pl.BlockSpec((None, None, KEY_CHUNK, Q_BLOCK), lambda b, jj, kc: (b, jj, chunk(jj, kc), 0)),
                  pl.BlockSpec((None, None, 8, Q_BLOCK), lambda b, jj, kc: (b, jj, 0, 0)),
                  _layer_spec(w_uk, layer),
                  _layer_spec(w_uv_t, layer),
                  pl.BlockSpec((ATT_HEADS, Q_BLOCK, Q_BLOCK), full3),
                  pl.BlockSpec((ATT_HEADS, Q_BLOCK, Q_BLOCK), full3)],
        out_specs=pl.BlockSpec((Q_BLOCK, ATT_HEADS * ATT_HEAD_DIM), qrow),
        scratch_shapes=[pltpu.VMEM((ATT_HEADS * Q_BLOCK, KV_LORA), BF16),
                        pltpu.VMEM((n_pair, 8, pair), F32),
                        pltpu.VMEM((n_pair, 8, pair), F32),
                        pltpu.VMEM((n_pair, KV_LORA, pair), F32)],
        compiler_params=_params("parallel", "parallel", "arbitrary"),
        name="latent_attention",
    )(qcat, ckv, ckv_t, scores, thr, w_uk, w_uv_t, diag, prev)


def _dsa_mixer(h, res_g, res_b, layer, rel_bias, w_in_pad, q_norm, kv_norm, w_q, w_uk, w_uv_t, w_out,
               *, batch, seq):
    cq, ckv, ckv_t, kidx, widx_t = _dsa_proj(h, w_in_pad, layer, q_norm, kv_norm, batch=batch, seq=seq)
    qcat = _linear(cq, w_q, layer, tm=512, out_dtype=BF16)
    scores, thr = _indexer(qcat, kidx, widx_t, batch=batch, seq=seq)
    o = _attention(qcat, ckv, ckv_t, scores, thr, w_uk, w_uv_t, layer, rel_bias, batch=batch, seq=seq)
    return _linear_ln(o, w_out, layer, h, res_g, res_b, tm=256)


def _router_kernel(x_ref, w_ref, b_ref, route_ref, cnt_ref, cnt_scr):
    x = x_ref[...]
    x_hi = x.astype(BF16)
    x_lo = (x - x_hi.astype(F32)).astype(BF16)
    logits = (jnp.dot(x_hi, w_ref[0], preferred_element_type=F32)
              + jnp.dot(x_lo, w_ref[0], preferred_element_type=F32)
              + jnp.dot(x_hi, w_ref[1], preferred_element_type=F32)) + b_ref[...]
    lane = lax.broadcasted_iota(jnp.int32, logits.shape, 1)
    big = jnp.int32(LANES)
    is_group = (lane >= MOE_EXPERTS) & (lane < MOE_EXPERTS + MOE_GROUPS)
    gl = jnp.where(is_group, logits, -jnp.inf)
    gm = jnp.max(gl, axis=1, keepdims=True)
    g_p = 1.0 / jnp.sum(jnp.exp(gl - gm), axis=1, keepdims=True)
    g_idx = jnp.min(jnp.where(gl == gm, lane, big), axis=1, keepdims=True) - MOE_EXPERTS
    in_group = (lane < MOE_EXPERTS) & (jnp.right_shift(lane, 3) == g_idx)
    el = jnp.where(in_group, logits, -jnp.inf)
    em = jnp.max(el, axis=1, keepdims=True)
    ee = jnp.exp(el - em)
    prob = ee / jnp.sum(ee, axis=1, keepdims=True)
    p1 = jnp.max(prob, axis=1, keepdims=True)
    i1 = jnp.min(jnp.where(in_group & (prob == p1), lane, big), axis=1, keepdims=True)
    rest = in_group & (lane != i1)
    p2 = jnp.max(jnp.where(rest, prob, -1.0), axis=1, keepdims=True)
    i2 = jnp.min(jnp.where(rest & (prob == p2), lane, big), axis=1, keepdims=True)
    den = p1 + p2
    g1 = g_p * (p1 / den)
    g2 = g_p * (p2 / den)

    @pl.when(pl.program_id(0) == 0)
    def _():
        cnt_scr[...] = jnp.zeros(cnt_scr.shape, F32)

    tm = logits.shape[0]
    oh1 = (lane == i1).astype(BF16)
    oh2 = (lane == i2).astype(BF16)
    r_i = lax.broadcasted_iota(jnp.int32, (tm, tm), 0)
    c_i = lax.broadcasted_iota(jnp.int32, (tm, tm), 1)
    tri = (c_i < r_i).astype(BF16)
    pre1 = jnp.dot(tri, oh1, preferred_element_type=F32)
    pre2 = jnp.dot(tri, oh2, preferred_element_type=F32)
    tot1 = jnp.sum(oh1.astype(F32), axis=0, keepdims=True)
    tot2 = jnp.sum(oh2.astype(F32), axis=0, keepdims=True)
    base = cnt_scr[...]
    rank1 = jnp.sum(jnp.where(lane == i1, base + pre1, 0.0), axis=1, keepdims=True)
    rank2 = jnp.sum(jnp.where(lane == i2, base + tot1 + pre2, 0.0), axis=1, keepdims=True)
    cnt_scr[...] = base + tot1 + tot2
    cnt_ref[...] = jnp.broadcast_to(cnt_scr[...], cnt_ref.shape)

    out = jnp.where(lane == 0, i1.astype(F32), jnp.where(lane == 1, i2.astype(F32), 0.0))
    out = jnp.where(lane == 2, g1, jnp.where(lane == 3, g2, out))
    out = jnp.where(lane == 4, rank1, jnp.where(lane == 5, rank2, out))
    route_ref[...] = out


def _router(h, w_r, b_r, *, tm=512):
    t = h.shape[0]
    return pl.pallas_call(
        _router_kernel,
        out_shape=(jax.ShapeDtypeStruct((t, LANES), F32), jax.ShapeDtypeStruct((8, LANES), F32)),
        grid=(t // tm,),
        in_specs=[pl.BlockSpec((tm, D_MODEL), lambda i: (i, 0)),
                  pl.BlockSpec((2, D_MODEL, LANES), lambda i: (0, 0, 0)),
                  pl.BlockSpec((1, LANES), lambda i: (0, 0))],
        out_specs=(pl.BlockSpec((tm, LANES), lambda i: (i, 0)),
                   pl.BlockSpec((8, LANES), lambda i: (0, 0))),
        scratch_shapes=[pltpu.VMEM((1, LANES), F32)],
        compiler_params=_params("arbitrary"),
        name="moe_router",
    )(h, w_r, b_r)


def _moe_expert_kernel(te_ref, nu_ref, rt_ref, x_hbm, wg_ref, wu_ref, wd_ref, y_ref,
                       xbuf, sem, wgb, wub, wdb):
    i = pl.program_id(0)
    n_used = nu_ref[0]
    slot = lax.rem(i, 2)
    tme = EXPERT_TILE
    pr = PACK_ROWS

    def start_gather(tile, buf):
        def body(r, carry):
            src = pl.multiple_of(rt_ref[tile * tme + r] * pr, pr)
            dst = pl.multiple_of(r * pr, pr)
            pltpu.make_async_copy(x_hbm.at[pl.ds(src, pr)], xbuf.at[buf, pl.ds(dst, pr)], sem.at[buf]).start()
            return carry
        lax.fori_loop(0, tme, body, 0, unroll=8)

    @pl.when(i == 0)
    def _():
        start_gather(0, 0)

    @pl.when(i + 1 < n_used)
    def _():
        start_gather(i + 1, 1 - slot)

    te = te_ref[i]
    prev = te_ref[jnp.maximum(i - 1, 0)]

    @pl.when((i == 0) | (te != prev))
    def _():
        wgb[...] = wg_ref[...].astype(BF16)
        wub[...] = wu_ref[...].astype(BF16)
        wdb[...] = wd_ref[...].astype(BF16)

    @pl.when(i < n_used)
    def _():
        pltpu.make_async_copy(x_hbm.at[pl.ds(0, tme * pr)], xbuf.at[slot], sem.at[slot]).wait()
        xb = _load_packed(xbuf.at[slot], tme).astype(BF16)
        gp = jnp.dot(xb, wgb[...], preferred_element_type=F32)
        up = jnp.dot(xb, wub[...], preferred_element_type=F32)
        hidden = (gp * jax.nn.sigmoid(gp)) * up
        _store_packed(y_ref, jnp.dot(hidden.astype(BF16), wdb[...], preferred_element_type=F32))

    @pl.when(i >= n_used)
    def _():
        y_ref[...] = jnp.zeros(y_ref.shape, y_ref.dtype)


def _moe_experts(tile_expert, n_used, row_token, h_packed, w_gate, w_up, w_down, layer):
    n_tiles = tile_expert.shape[0]
    rows = EXPERT_TILE * PACK_ROWS
    wspec = lambda a, b: pl.BlockSpec((None, None, a, b), lambda i, te, nu, rt: (layer, te[i], 0, 0))
    return pl.pallas_call(
        _moe_expert_kernel,
        out_shape=jax.ShapeDtypeStruct((n_tiles * rows, LANES), jnp.uint32),
        grid_spec=pltpu.PrefetchScalarGridSpec(
            num_scalar_prefetch=3,
            grid=(n_tiles,),
            in_specs=[pl.BlockSpec(memory_space=pl.ANY),
                      wspec(D_MODEL, MOE_FF), wspec(D_MODEL, MOE_FF), wspec(MOE_FF, D_MODEL)],
            out_specs=pl.BlockSpec((rows, LANES), lambda i, te, nu, rt: (i, 0)),
            scratch_shapes=[pltpu.VMEM((2, rows, LANES), jnp.uint32),
                            pltpu.SemaphoreType.DMA((2,)),
                            pltpu.VMEM((D_MODEL, MOE_FF), BF16),
                            pltpu.VMEM((D_MODEL, MOE_FF), BF16),
                            pltpu.VMEM((MOE_FF, D_MODEL), BF16)]),
        compiler_params=_params("arbitrary"),
        name="moe_experts",
    )(tile_expert, n_used, row_token, h_packed, w_gate, w_up, w_down)


def _moe_combine_kernel(slot_ref, h_ref, route_ref, y_hbm, g_ref, b_ref, o_ref, ybuf, sem):
    i = pl.program_id(0)
    n = pl.num_programs(0)
    slot = lax.rem(i, 2)
    tm = h_ref.shape[0]
    pr = PACK_ROWS

    def start_gather(tile, buf):
        def body(r, carry):
            for k in range(2):
                src = pl.multiple_of(slot_ref[(tile * tm + r) * 2 + k] * pr, pr)
                dst = pl.multiple_of((k * tm + r) * pr, pr)
                pltpu.make_async_copy(y_hbm.at[pl.ds(src, pr)], ybuf.at[buf, pl.ds(dst, pr)], sem.at[buf]).start()
            return carry
        lax.fori_loop(0, tm, body, 0, unroll=4)

    @pl.when(i == 0)
    def _():
        start_gather(0, 0)

    @pl.when(i + 1 < n)
    def _():
        start_gather(i + 1, 1 - slot)

    pltpu.make_async_copy(y_hbm.at[pl.ds(0, 2 * tm * pr)], ybuf.at[slot], sem.at[slot]).wait()
    y1 = _load_packed(ybuf.at[slot, pl.ds(0, tm * pr)], tm)
    y2 = _load_packed(ybuf.at[slot, pl.ds(tm * pr, tm * pr)], tm)
    ffn = route_ref[:, 2:3] * y1 + route_ref[:, 3:4] * y2
    o_ref[...] = _layer_norm_rows(DN_ALPHA * h_ref[...] + ffn, g_ref[...], b_ref[...])


def _moe_combine(slots, h, route, y, ln_g, ln_b, *, tm):
    t = h.shape[0]
    row = lambda w: pl.BlockSpec((tm, w), lambda i, s: (i, 0))
    vec = pl.BlockSpec((1, D_MODEL), lambda i, s: (0, 0))
    return pl.pallas_call(
        _moe_combine_kernel,
        out_shape=jax.ShapeDtypeStruct((t, D_MODEL), F32),
        grid_spec=pltpu.PrefetchScalarGridSpec(
            num_scalar_prefetch=1,
            grid=(t // tm,),
            in_specs=[row(D_MODEL), row(LANES), pl.BlockSpec(memory_space=pl.ANY), vec, vec],
            out_specs=row(D_MODEL),
            scratch_shapes=[pltpu.VMEM((2, 2 * tm * PACK_ROWS, LANES), jnp.uint32),
                            pltpu.SemaphoreType.DMA((2,))]),
        compiler_params=_params("arbitrary"),
        name="moe_combine",
    )(slots, h, route, y, ln_g.reshape(1, D_MODEL), ln_b.reshape(1, D_MODEL))


def _moe_plan(route, counts, n_tiles):
    t = route.shape[0]
    tme = EXPERT_TILE
    cnt = counts[0, :MOE_EXPERTS].astype(jnp.int32)
    tiles = (cnt + tme - 1) // tme
    tile_end = jnp.cumsum(tiles)
    tile_start = tile_end - tiles
    n_used = tile_end[-1]
    choice = route[:, 0:2].astype(jnp.int32)
    rank = route[:, 4:6].astype(jnp.int32)
    slots = (tile_start[choice] * tme + rank).reshape(-1)
    tile_ids = jnp.arange(n_tiles, dtype=jnp.int32)
    te = jnp.sum((tile_ids[:, None] >= tile_end[None, :]).astype(jnp.int32), axis=1)
    last_e = jnp.sum((n_used - 1 >= tile_end).astype(jnp.int32))
    te = jnp.where(tile_ids < n_used, te, last_e)
    row_token = jnp.zeros((n_tiles * tme,), jnp.int32).at[slots].set(jnp.arange(2 * t, dtype=jnp.int32) // 2)
    return te, n_used.reshape(1).astype(jnp.int32), row_token, slots.astype(jnp.int32)


def _hier_moe(h, h_packed, layer, w_group, b_group, w_expert, b_expert, w_gate, w_up, w_down, ln_g, ln_b):
    t = h.shape[0]
    pad = LANES - MOE_EXPERTS - MOE_GROUPS
    w_r = jnp.concatenate([w_expert, w_group, jnp.zeros((D_MODEL, pad), F32)], axis=1)
    w_hi = w_r.astype(BF16)
    w_r = jnp.stack([w_hi, (w_r - w_hi.astype(F32)).astype(BF16)])
    b_r = jnp.concatenate([b_expert, b_group, jnp.zeros((pad,), F32)]).reshape(1, LANES)
    route, counts = _router(h, w_r, b_r)
    n_tiles = (2 * t) // EXPERT_TILE + MOE_EXPERTS
    te, n_used, row_token, slots = _moe_plan(route, counts, n_tiles)
    y = _moe_experts(te, n_used, row_token, h_packed, w_gate, w_up, w_down, layer)
    return _moe_combine(slots, h, route, y, ln_g, ln_b, tm=COMBINE_TILE)


def kernel(x, rel_bias, s5_w_in, s5_a_re, s5_a_im, s5_log_dt, s5_b_re, s5_b_im, s5_c_re, s5_c_im, s5_d, s5_w_glu, s5_w_out, dsa_w_in, dsa_q_norm, dsa_kv_norm, dsa_w_uq, dsa_w_qidx, dsa_w_uk, dsa_w_uv, dsa_w_out, moe_w_group, moe_b_group, moe_w_expert, moe_b_expert, moe_w_gate, moe_w_up, moe_w_down, ln_mix_g, ln_mix_b, ln_ffn_g, ln_ffn_b):
    batch, seq, dm = x.shape
    h = x.reshape(batch * seq, dm)
    s5_w_in, s5_w_glu, s5_w_out = (w.astype(BF16) for w in (s5_w_in, s5_w_glu, s5_w_out))
    n_pad = Q_LORA + KV_LORA + IDX_DIM + LANES - dsa_w_in.shape[-1]
    dsa_w_in_pad = jnp.pad(dsa_w_in, ((0, 0), (0, 0), (0, n_pad))).astype(BF16)
    dsa_w_q = jnp.concatenate([dsa_w_uq, dsa_w_qidx], axis=-1).astype(BF16)
    dsa_w_uk, dsa_w_out = dsa_w_uk.astype(BF16), dsa_w_out.astype(BF16)
    dsa_w_uv_t = jnp.swapaxes(dsa_w_uv, -1, -2).astype(BF16)
    for i in range(DEPTH):
        j = i // 2
        if i % 2 == 0:
            h, hp = _s5_mixer(h, ln_mix_g[i], ln_mix_b[i], j, s5_w_in, s5_a_re[j], s5_a_im[j], s5_log_dt[j],
                              s5_b_re[j], s5_b_im[j], s5_c_re[j], s5_c_im[j], s5_d[j], s5_w_glu, s5_w_out,
                              batch=batch, seq=seq)
        else:
            h, hp = _dsa_mixer(h, ln_mix_g[i], ln_mix_b[i], j, rel_bias, dsa_w_in_pad, dsa_q_norm[j],
                               dsa_kv_norm[j], dsa_w_q, dsa_w_uk, dsa_w_uv_t, dsa_w_out, batch=batch, seq=seq)
        h = _hier_moe(h, hp, i, moe_w_group[i], moe_b_group[i], moe_w_expert[i], moe_b_expert[i],
                      moe_w_gate, moe_w_up, moe_w_down, ln_ffn_g[i], ln_ffn_b[i])
    return h.reshape(batch, seq, dm)
```

```python
import functools
import math

import numpy as np
import jax
import jax.numpy as jnp
from jax import lax
from jax.experimental import pallas as pl
from jax.experimental.pallas import tpu as pltpu

F32 = jnp.float32
BF16 = jnp.bfloat16

D_MODEL = 2048
DEPTH = 4
S5_GROUP = 16
S5_GROUPS = D_MODEL // S5_GROUP
S5_STATE = 64
S5_CHUNK = 64
ATT_HEADS = 16
ATT_HEAD_DIM = 128
Q_LORA = 512
KV_LORA = 512
IDX_HEADS = 16
IDX_DIM = 128
IDX_TOPK = 256
Q_BLOCK = 128
KEY_CHUNK = 512
REL_BUCKETS = 32
REL_MAX_DIST = 128
MOE_GROUPS = 4
MOE_PER_GROUP = 8
MOE_EXPERTS = 32
MOE_FF = 256
EXPERT_TILE = 256
COMBINE_TILE = 256
GATHER_AHEAD = 2
DN_ALPHA = (2 * DEPTH) ** 0.25
LN_EPS = 1e-5
RMS_EPS = 1e-6

LANES = 128
MASKED = -1e30
VMEM_LIMIT = 56 * 1024 * 1024


def _params(*sem):
    return pltpu.CompilerParams(dimension_semantics=sem, vmem_limit_bytes=VMEM_LIMIT)


def _layer_norm_rows(y, g, b):
    mean = jnp.mean(y, axis=-1, keepdims=True)
    yc = y - mean
    var = jnp.mean(yc * yc, axis=-1, keepdims=True)
    return yc * lax.rsqrt(var + LN_EPS) * g + b


def _mm_plain_kernel(x_ref, w_ref, o_ref, *, tn):
    xb = x_ref[...].astype(BF16)
    for n0 in range(0, o_ref.shape[1], tn):
        acc = jnp.dot(xb, w_ref[:, n0:n0 + tn], preferred_element_type=F32)
        o_ref[:, n0:n0 + tn] = acc.astype(o_ref.dtype)


def _mm_glu_kernel(z_ref, w_ref, o_ref, *, tn):
    zb = z_ref[...]
    for n0 in range(0, o_ref.shape[1], tn):
        acc = jnp.dot(zb, w_ref[:, n0:n0 + tn], preferred_element_type=F32)
        zc = z_ref[:, n0:n0 + tn].astype(F32)
        o_ref[:, n0:n0 + tn] = (zc * jax.nn.sigmoid(acc)).astype(o_ref.dtype)


PACK_ROWS = D_MODEL // (2 * LANES)


def _store_packed(o_ref, y):
    half = y.shape[1] // 2
    bits = lambda v: lax.bitcast_convert_type(v.astype(BF16).astype(F32), jnp.uint32)
    word = (bits(y[:, half:]) & jnp.uint32(0xFFFF0000)) | (bits(y[:, :half]) >> 16)
    rows = y.shape[0]
    for c in range(PACK_ROWS):
        o_ref[pl.ds(c, rows, stride=PACK_ROWS), :] = word[:, c * LANES:(c + 1) * LANES]


def _load_packed(x_ref, rows):
    word = jnp.concatenate([x_ref[pl.ds(c, rows, stride=PACK_ROWS), :] for c in range(PACK_ROWS)], axis=1)
    lo = lax.bitcast_convert_type(word << 16, F32)
    hi = lax.bitcast_convert_type(word & jnp.uint32(0xFFFF0000), F32)
    return jnp.concatenate([lo, hi], axis=1)


def _mm_ln_kernel(x_ref, w_ref, res_ref, g_ref, b_ref, o_ref, op_ref, *, tn):
    xb = x_ref[...].astype(BF16)
    for n0 in range(0, o_ref.shape[1], tn):
        acc = jnp.dot(xb, w_ref[:, n0:n0 + tn], preferred_element_type=F32)
        o_ref[:, n0:n0 + tn] = DN_ALPHA * res_ref[:, n0:n0 + tn] + acc
    y = _layer_norm_rows(o_ref[...], g_ref[...], b_ref[...])
    o_ref[...] = y
    _store_packed(op_ref, y)


def _layer_spec(w_all, layer):
    shape = w_all.shape[1:]
    return pl.BlockSpec((None,) + shape, lambda *_: (layer,) + (0,) * len(shape))


def _linear(x, w_all, layer, *, tm, out_dtype, tn=512):
    m, k = x.shape
    n = w_all.shape[-1]
    return pl.pallas_call(
        functools.partial(_mm_plain_kernel, tn=tn),
        out_shape=jax.ShapeDtypeStruct((m, n), out_dtype),
        grid=(m // tm,),
        in_specs=[pl.BlockSpec((tm, k), lambda i: (i, 0)), _layer_spec(w_all, layer)],
        out_specs=pl.BlockSpec((tm, n), lambda i: (i, 0)),
        compiler_params=_params("parallel"),
        name="linear",
    )(x, w_all)


def _linear_glu(z, w_all, layer, *, tm, tn=512):
    m, k = z.shape
    return pl.pallas_call(
        functools.partial(_mm_glu_kernel, tn=tn),
        out_shape=jax.ShapeDtypeStruct((m, k), BF16),
        grid=(m // tm,),
        in_specs=[pl.BlockSpec((tm, k), lambda i: (i, 0)), _layer_spec(w_all, layer)],
        out_specs=pl.BlockSpec((tm, k), lambda i: (i, 0)),
        compiler_params=_params("parallel"),
        name="linear_glu",
    )(z, w_all)


def _linear_ln(x, w_all, layer, res, g, b, *, tm, tn=512):
    m, k = x.shape
    n = w_all.shape[-1]
    return pl.pallas_call(
        functools.partial(_mm_ln_kernel, tn=tn),
        out_shape=(jax.ShapeDtypeStruct((m, n), F32), jax.ShapeDtypeStruct((m * PACK_ROWS, LANES), jnp.uint32)),
        grid=(m // tm,),
        in_specs=[pl.BlockSpec((tm, k), lambda i: (i, 0)),
                  _layer_spec(w_all, layer),
                  pl.BlockSpec((tm, n), lambda i: (i, 0)),
                  pl.BlockSpec((1, n), lambda i: (0, 0)),
                  pl.BlockSpec((1, n), lambda i: (0, 0))],
        out_specs=(pl.BlockSpec((tm, n), lambda i: (i, 0)),
                   pl.BlockSpec((tm * PACK_ROWS, LANES), lambda i: (i, 0))),
        compiler_params=_params("parallel"),
        name="linear_ln",
    )(x, w_all, res, g.reshape(1, n), b.reshape(1, n))


def _gelu_tanh(x):
    return 0.5 * x * (1.0 + jnp.tanh(math.sqrt(2.0 / math.pi) * (x + 0.044715 * x * x * x)))


def _s5_kernel(u_ref, ar_ref, ai_ref, ldt_ref, br_ref, bi_ref, cr_ref, ci_ref, d_ref, o_ref, a_scr,
               *, n_chunks):
    cs, p, n = S5_CHUNK, S5_GROUP, S5_STATE
    width = cs * p
    n2 = 2 * n

    @pl.when(pl.program_id(0) == 0)
    def _():
        a_scr[...] = jnp.zeros(a_scr.shape, a_scr.dtype)

    dt = jnp.exp(ldt_ref[0])
    lam_r = jnp.minimum(ar_ref[0], -1e-4)
    lam_i = ai_ref[0]
    lo_half = lax.broadcasted_iota(jnp.int32, (1, n2), 1) < n

    def a_pow(tau):
        mag = jnp.exp(lam_r * dt * tau)
        ph = lam_i * dt * tau
        return mag * jnp.cos(ph), mag * jnp.sin(ph)

    tau = lax.broadcasted_iota(jnp.int32, (cs, 1), 0).astype(F32)
    a0r, a0i = a_pow(tau)
    a1r, a1i = a_pow(tau + 1.0)
    avr, avi = a_pow((cs - 1.0) - tau)

    abr, abi = a1r[0:1], a1i[0:1]
    den = lam_r * lam_r + lam_i * lam_i
    kr = ((abr - 1.0) * lam_r + abi * lam_i) / den
    ki = (abi * lam_r - (abr - 1.0) * lam_i) / den
    b_r, b_i = br_ref[0], bi_ref[0]
    bbr = kr * b_r - ki * b_i
    bbi = kr * b_i + ki * b_r
    c_r, c_i = cr_ref[0], ci_ref[0]

    def rep_rows(x):
        return jnp.concatenate([jnp.broadcast_to(x[t:t + 1, :], (p, n2)) for t in range(cs)], axis=0)

    def tile_rows(x):
        return jnp.concatenate([x] * cs, axis=0)

    def sel(lo, hi):
        return jnp.where(lo_half, lo, hi)

    crr, cii = tile_rows(c_r), tile_rows(c_i)
    brr, bii = tile_rows(bbr), tile_rows(bbi)
    c_taps = crr * rep_rows(sel(a0r, -a0i)) - cii * rep_rows(sel(a0i, a0r))
    c_out = crr * rep_rows(sel(a1r, -a1i)) - cii * rep_rows(sel(a1i, a1r))
    b_in = brr * rep_rows(sel(avr, avi)) + bii * rep_rows(sel(-avi, avr))
    b_in_sw = brr * rep_rows(sel(avi, avr)) + bii * rep_rows(sel(avr, -avi))

    b_mat = sel(bbr, bbi)
    taps = lax.dot_general(b_mat, c_taps, (((1,), (1,)), ((), ())),
                           precision=lax.Precision.HIGHEST, preferred_element_type=F32)

    lane = lax.broadcasted_iota(jnp.int32, (p, width), 1)
    band = [taps]
    for j in range(1, LANES // p):
        band.append(jnp.where(lane >= p * j, pltpu.roll(taps, p * j, axis=1), 0.0))
    band = jnp.concatenate(band, axis=0)
    n_blk = width // LANES
    band_t = jnp.concatenate([band[:, d * LANES:(d + 1) * LANES].T for d in reversed(range(n_blk))],
                             axis=1).astype(BF16)
    for q in range(n_blk):
        a_scr[q * LANES:(q + 1) * LANES, 0:(q + 1) * LANES] = band_t[:, (n_blk - 1 - q) * LANES:width]
    a_scr[width:width + n2, :] = b_in.T.astype(BF16)
    a_scr[width + n2:width + 2 * n2, :] = b_in_sw.T.astype(BF16)

    u = u_ref[...].reshape(width, u_ref.shape[2])
    res = jnp.dot(a_scr[...], u, preferred_element_type=F32)
    y = res[0:width]
    x_in = res[width:width + n2]
    x_sw = res[width + n2:width + 2 * n2]

    n_lanes = x_in.shape[1]
    levels = n_chunks.bit_length() - 1
    mults = []
    for k in range(levels):
        mr, mi = a_pow(jnp.full((1, 1), float(cs * 2 ** k), F32))
        mults += [mr, sel(-mi, mi), sel(mi, -mi)]
    mults = jnp.concatenate(mults + [jnp.zeros((n2 - len(mults), n2), F32)], axis=0)
    mcol = mults.T
    chunk_id = lax.rem(lax.broadcasted_iota(jnp.int32, (n2, n_lanes), 1), n_chunks)
    for k in range(levels):
        sh = 2 ** k
        ok = chunk_id >= sh
        xs = jnp.where(ok, pltpu.roll(x_in, sh, axis=1), 0.0)
        xs_sw = jnp.where(ok, pltpu.roll(x_sw, sh, axis=1), 0.0)
        m_rr, m_ni, m_pi = (mcol[:, 3 * k + j:3 * k + j + 1] for j in range(3))
        x_in, x_sw = (x_in + xs * m_rr + xs_sw * m_ni, x_sw + xs_sw * m_rr + xs * m_pi)
    h_start = jnp.where(chunk_id >= 1, pltpu.roll(x_in, 1, axis=1), 0.0).astype(BF16)

    y = y + jnp.dot(c_out.astype(BF16), h_start, preferred_element_type=F32)
    y3 = y.reshape(cs, p, n_lanes) + d_ref[0] * u_ref[...].astype(F32)
    o_ref[...] = _gelu_tanh(y3).astype(o_ref.dtype)


def _s5_core(u_t, a_re, a_im, log_dt, b_re, b_im, c_re, c_im, d, *, n_chunks):
    cs, dm, n_lanes = u_t.shape
    g = dm // S5_GROUP
    assert n_chunks & (n_chunks - 1) == 0
    dup = lambda x: jnp.concatenate([x, x], axis=-1)
    a_re2 = dup(a_re).reshape(g, 1, 2 * S5_STATE)
    a_im2 = dup(a_im).reshape(g, 1, 2 * S5_STATE)
    ldt = log_dt.reshape(g, 1, 1)
    bt_re2 = dup(jnp.swapaxes(b_re, 1, 2))
    bt_im2 = dup(jnp.swapaxes(b_im, 1, 2))
    c_re2 = dup(c_re)
    c_im2 = dup(c_im)
    d_c = d.reshape(g, S5_GROUP, 1)
    width = cs * S5_GROUP
    vec = lambda w: pl.BlockSpec((1, 1, w), lambda i: (i, 0, 0))
    mat = pl.BlockSpec((1, S5_GROUP, 2 * S5_STATE), lambda i: (i, 0, 0))
    blk = pl.BlockSpec((cs, S5_GROUP, n_lanes), lambda i: (0, i, 0))
    return pl.pallas_call(
        functools.partial(_s5_kernel, n_chunks=n_chunks),
        out_shape=jax.ShapeDtypeStruct((cs, dm, n_lanes), BF16),
        grid=(g,),
        in_specs=[blk, vec(2 * S5_STATE), vec(2 * S5_STATE), vec(1), mat, mat, mat, mat,
                  pl.BlockSpec((1, S5_GROUP, 1), lambda i: (i, 0, 0))],
        out_specs=blk,
        scratch_shapes=[pltpu.VMEM((width + 4 * S5_STATE, width), BF16)],
        compiler_params=_params("arbitrary"),
        name="s5_core",
    )(u_t, a_re2, a_im2, ldt, bt_re2, bt_im2, c_re2, c_im2, d_c)


S5_STEP = 8
LANE_ROWS = 128


def _s5_in_kernel(x_ref, w_ref, o_ref, u_scr):
    rows = LANE_ROWS * S5_STEP
    dm = x_ref.shape[2]
    xb = x_ref[...].reshape(rows, dm).astype(BF16)
    for n0 in range(0, dm, 512):
        acc = jnp.dot(xb, w_ref[:, n0:n0 + 512], preferred_element_type=F32)
        for c in range(512 // LANES):
            u_scr[n0 // LANES + c] = acc[:, c * LANES:(c + 1) * LANES]
    for j in range(S5_STEP):
        for cb in range(dm // LANES):
            tile = u_scr[cb, pl.ds(j, LANE_ROWS, stride=S5_STEP), :]
            o_ref[j, cb * LANES:(cb + 1) * LANES, :] = tile.T.astype(o_ref.dtype)


def _s5_in_proj(h3, w_all, layer):
    n_bc, cs, dm = h3.shape
    return pl.pallas_call(
        _s5_in_kernel,
        out_shape=jax.ShapeDtypeStruct((cs, dm, n_bc), BF16),
        grid=(cs // S5_STEP, n_bc // LANE_ROWS),
        in_specs=[pl.BlockSpec((LANE_ROWS, S5_STEP, dm), lambda s, r: (r, s, 0)), _layer_spec(w_all, layer)],
        out_specs=pl.BlockSpec((S5_STEP, dm, LANE_ROWS), lambda s, r: (s, 0, r)),
        scratch_shapes=[pltpu.VMEM((dm // LANES, LANE_ROWS * S5_STEP, LANES), F32)],
        compiler_params=_params("parallel", "parallel"),
        name="s5_in_proj",
    )(h3, w_all)


def _s5_glu_kernel(z_ref, w_ref, o_ref, z_scr, o_scr):
    nh = pl.program_id(2)
    half = o_ref.shape[2]

    @pl.when(nh == 0)
    def _():
        for j in range(S5_STEP):
            z_scr[j * LANE_ROWS:(j + 1) * LANE_ROWS, :] = z_ref[j].astype(F32).T

    zb = z_scr[...].astype(BF16)
    col0 = pl.multiple_of(nh * half, half)
    for n0 in range(0, half, 512):
        acc = jnp.dot(zb, w_ref[:, n0:n0 + 512], preferred_element_type=F32)
        zg = z_scr[:, pl.ds(col0 + n0, 512)] * jax.nn.sigmoid(acc)
        for j in range(S5_STEP):
            for c in range(512 // LANES):
                o_scr[n0 // LANES + c, pl.ds(j, LANE_ROWS, stride=S5_STEP), :] = (
                    zg[j * LANE_ROWS:(j + 1) * LANE_ROWS, c * LANES:(c + 1) * LANES])
    out = jnp.concatenate([o_scr[cb] for cb in range(half // LANES)], axis=1)
    o_ref[...] = out.reshape(o_ref.shape)


def _s5_glu(z_t, w_all, layer):
    cs, dm, n_bc = z_t.shape
    rows = LANE_ROWS * S5_STEP
    half = dm // 2
    return pl.pallas_call(
        _s5_glu_kernel,
        out_shape=jax.ShapeDtypeStruct((n_bc, cs, dm), F32),
        grid=(cs // S5_STEP, n_bc // LANE_ROWS, 2),
        in_specs=[pl.BlockSpec((S5_STEP, dm, LANE_ROWS), lambda s, r, nh: (s, 0, r)),
                  pl.BlockSpec((None, dm, half), lambda s, r, nh: (layer, 0, nh))],
        out_specs=pl.BlockSpec((LANE_ROWS, S5_STEP, half), lambda s, r, nh: (r, s, nh)),
        scratch_shapes=[pltpu.VMEM((rows, dm), F32), pltpu.VMEM((half // LANES, rows, LANES), F32)],
        compiler_params=_params("parallel", "parallel", "arbitrary"),
        name="s5_glu",
    )(z_t, w_all)


def _s5_mixer(h, res_g, res_b, layer, w_in, a_re, a_im, log_dt, b_re, b_im, c_re, c_im, d, w_glu, w_out,
              *, batch, seq):
    t = batch * seq
    nc = seq // S5_CHUNK
    u_t = _s5_in_proj(h.reshape(batch * nc, S5_CHUNK, D_MODEL), w_in, layer)
    z_t = _s5_core(u_t, a_re, a_im, log_dt, b_re, b_im, c_re, c_im, d, n_chunks=nc)
    zg = _s5_glu(z_t, w_glu, layer).reshape(t, D_MODEL)
    return _linear_ln(zg, w_out, layer, h, res_g, res_b, tm=256)


def _dsa_proj_kernel(x_ref, w_ref, qn_ref, kn_ref, cq_ref, ckv_ref, ckvt_ref, kidx_ref, widxt_ref):
    xb = x_ref[...].astype(BF16)

    def rms(v, g):
        return v * lax.rsqrt(jnp.mean(v * v, axis=-1, keepdims=True) + RMS_EPS) * g

    o1 = Q_LORA + KV_LORA
    cq = jnp.dot(xb, w_ref[:, 0:Q_LORA], preferred_element_type=F32)
    cq_ref[...] = rms(cq, qn_ref[...]).astype(BF16)
    ckv = rms(jnp.dot(xb, w_ref[:, Q_LORA:o1], preferred_element_type=F32), kn_ref[...])
    ckv_ref[...] = ckv.astype(BF16)
    ckvt_ref[...] = ckv.T.astype(BF16)
    kidx_ref[...] = jnp.dot(xb, w_ref[:, o1:o1 + IDX_DIM], preferred_element_type=F32).astype(BF16)
    widx = jnp.dot(xb, w_ref[:, o1 + IDX_DIM:o1 + IDX_DIM + LANES], preferred_element_type=F32)
    widxt_ref[...] = (widx * (IDX_HEADS ** -0.5)).T[0:IDX_HEADS, :]


def _dsa_proj(h, w_in_pad, layer, q_norm, kv_norm, *, batch, seq, tm=512):
    t = h.shape[0]
    per_b = seq // tm
    row = lambda w: pl.BlockSpec((tm, w), lambda i: (i, 0))
    return pl.pallas_call(
        _dsa_proj_kernel,
        out_shape=(jax.ShapeDtypeStruct((t, Q_LORA), BF16), jax.ShapeDtypeStruct((t, KV_LORA), BF16),
                   jax.ShapeDtypeStruct((batch, KV_LORA, seq), BF16),
                   jax.ShapeDtypeStruct((t, IDX_DIM), BF16), jax.ShapeDtypeStruct((IDX_HEADS, t), F32)),
        grid=(t // tm,),
        in_specs=[row(D_MODEL), _layer_spec(w_in_pad, layer),
                  pl.BlockSpec((1, Q_LORA), lambda i: (0, 0)), pl.BlockSpec((1, KV_LORA), lambda i: (0, 0))],
        out_specs=(row(Q_LORA), row(KV_LORA),
                   pl.BlockSpec((None, KV_LORA, tm), lambda i: (i // per_b, 0, i % per_b)),
                   row(IDX_DIM), pl.BlockSpec((IDX_HEADS, tm), lambda i: (0, i))),
        compiler_params=_params("parallel"),
        name="dsa_proj",
    )(h, w_in_pad, q_norm.reshape(1, Q_LORA), kv_norm.reshape(1, KV_LORA))


def _reduce_rows(op, x):
    rows, lanes = x.shape
    part = op(x.reshape(rows // 64, 8, 8, lanes), axis=0)
    return op(op(part, axis=0), axis=0, keepdims=True)


def _indexer_kernel(q_ref, k_ref, w_ref, s_ref, thr_ref, key_scr, *, seq):
    jb = pl.program_id(1)
    n_sel = float(min(IDX_TOPK, seq // 4))
    int_min = jnp.int32(-2 ** 31)
    flip = jnp.int32(0x7FFFFFFF)

    def run(width):
        k = k_ref[0:width, :]
        wrow = w_ref[...] * (IDX_DIM ** -0.5)
        acc = jnp.zeros((width, Q_BLOCK), F32)
        for g in range(IDX_HEADS // 2):
            qg = jnp.concatenate([q_ref[:, h * IDX_DIM:(h + 1) * IDX_DIM] for h in (2 * g, 2 * g + 1)], axis=0)
            s = lax.dot_general(k, qg, (((1,), (1,)), ((), ())), preferred_element_type=F32)
            acc = (acc + jnp.maximum(s[:, 0:Q_BLOCK], 0.0) * wrow[2 * g:2 * g + 1, :]
                   + jnp.maximum(s[:, Q_BLOCK:2 * Q_BLOCK], 0.0) * wrow[2 * g + 1:2 * g + 2, :])
        k_pos = lax.broadcasted_iota(jnp.int32, (width, Q_BLOCK), 0)
        q_pos = jb * Q_BLOCK + lax.broadcasted_iota(jnp.int32, (width, Q_BLOCK), 1)
        acc = jnp.where(k_pos <= q_pos, acc, -jnp.inf)
        s_ref[0:width, :] = acc
        if width < seq:
            s_ref[width:seq, :] = jnp.full((seq - width, Q_BLOCK), -jnp.inf, F32)

        bits = lax.bitcast_convert_type(acc, jnp.int32)
        key_scr[0:width, :] = jnp.where(bits < 0, bits ^ flip, bits)

        def count_ge(cand):
            return _reduce_rows(jnp.sum, (key_scr[0:width, :] >= cand).astype(F32))

        lo = jnp.where(count_ge(jnp.zeros((1, Q_BLOCK), jnp.int32)) >= n_sel, jnp.int32(0), int_min)

        def body(i, lo):
            cand = lo + jnp.left_shift(jnp.int32(1), jnp.int32(30) - i)
            return jnp.where(count_ge(cand) >= n_sel, cand, lo)

        lo = lax.fori_loop(0, 31, body, lo)
        thr = lax.bitcast_convert_type(jnp.where(lo < 0, lo ^ flip, lo), F32)
        thr_ref[...] = jnp.broadcast_to(thr, thr_ref.shape)

    n_cls = seq // KEY_CHUNK
    per_cls = KEY_CHUNK // Q_BLOCK
    for cls in range(n_cls):
        pl.when(jb // per_cls == cls)(functools.partial(run, KEY_CHUNK * (cls + 1)))


def _indexer(qcat, kidx, widx_t, *, batch, seq):
    n_blk = seq // Q_BLOCK
    return pl.pallas_call(
        functools.partial(_indexer_kernel, seq=seq),
        out_shape=(jax.ShapeDtypeStruct((batch, n_blk, seq, Q_BLOCK), F32),
                   jax.ShapeDtypeStruct((batch, n_blk, 8, Q_BLOCK), F32)),
        grid=(batch, n_blk),
        in_specs=[pl.BlockSpec((Q_BLOCK, IDX_HEADS * IDX_DIM), lambda b, j: (b * n_blk + j, 1)),
                  pl.BlockSpec((seq, IDX_DIM), lambda b, j: (b, 0)),
                  pl.BlockSpec((IDX_HEADS, Q_BLOCK), lambda b, j: (0, b * n_blk + j))],
        out_specs=(pl.BlockSpec((None, None, seq, Q_BLOCK), lambda b, j: (b, j, 0, 0)),
                   pl.BlockSpec((None, None, 8, Q_BLOCK), lambda b, j: (b, j, 0, 0))),
        scratch_shapes=[pltpu.VMEM((seq, Q_BLOCK), jnp.int32)],
        compiler_params=_params("parallel", "parallel"),
        name="indexer",
    )(qcat, kidx, widx_t)


def _attn_kernel(q_ref, kv_ref, kvt_ref, s_ref, thr_ref, wuk_ref, wuvt_ref, bd_ref, bp_ref, o_ref,
                 ql_scr, m_scr, l_scr, acc_scr):
    jb = pl.program_id(1)
    kc = pl.program_id(2)
    last = (jb * Q_BLOCK) // KEY_CHUNK
    hd, qb = ATT_HEAD_DIM, Q_BLOCK
    sub = KEY_CHUNK // qb
    pair = 2 * qb

    @pl.when(kc == 0)
    def _():
        for h in range(ATT_HEADS):
            ql = jnp.dot(q_ref[:, h * hd:(h + 1) * hd], wuk_ref[h], preferred_element_type=F32)
            ql_scr[h * qb:(h + 1) * qb, :] = (ql * (hd ** -0.5)).astype(BF16)
        m_scr[...] = jnp.full(m_scr.shape, MASKED, F32)
        l_scr[...] = jnp.zeros(l_scr.shape, F32)
        acc_scr[...] = jnp.zeros(acc_scr.shape, F32)

    def attend(biases):
        width = len(biases) * qb
        kv = kv_ref[0:width, :]
        kvt = kvt_ref[:, 0:width]
        k_pos = kc * KEY_CHUNK + lax.broadcasted_iota(jnp.int32, (width, qb), 0)
        q_pos = jb * qb + lax.broadcasted_iota(jnp.int32, (width, qb), 1)
        keep = (s_ref[0:width, :] >= thr_ref[0:1, :]) & (k_pos <= q_pos)
        mask_add = jnp.where(keep, 0.0, MASKED)
        mask_add = jnp.concatenate([mask_add, mask_add], axis=1)
        for g in range(ATT_HEADS // 2):
            s = lax.dot_general(kv, ql_scr[g * pair:(g + 1) * pair, :], (((1,), (1,)), ((), ())),
                                preferred_element_type=F32)
            lg = s + mask_add
            if any(b is not None for b in biases):
                parts = []
                for c, b in enumerate(biases):
                    part = lg[c * qb:(c + 1) * qb]
                    if b is not None:
                        part = part + jnp.concatenate([b[2 * g], b[2 * g + 1]], axis=1)
                    parts.append(part)
                lg = jnp.concatenate(parts, axis=0)
            m_prev = m_scr[g, 0:1, :]
            m_new = jnp.maximum(m_prev, _reduce_rows(jnp.max, lg))
            corr = jnp.exp(m_prev - m_new)
            pr = jnp.exp(lg - m_new)
            l_scr[g] = jnp.broadcast_to(corr * l_scr[g, 0:1, :] + _reduce_rows(jnp.sum, pr), (8, pair))
            m_scr[g] = jnp.broadcast_to(m_new, (8, pair))
            pv = jnp.dot(kvt, pr.astype(BF16), preferred_element_type=F32)
            acc_scr[g] = acc_scr[g] * corr + pv

    r = jb - last * sub
    for nb in range(1, sub + 1):
        biases = (None,) * (nb - 2) + ((bp_ref,) if nb >= 2 else ()) + (bd_ref,)
        pl.when((kc == last) & (r == nb - 1))(functools.partial(attend, biases))
    prev_only = (kc == last - 1) & (r == 0)
    pl.when(prev_only)(functools.partial(attend, (None,) * (sub - 1) + (bp_ref,)))
    pl.when((kc < last) & jnp.logical_not(prev_only))(functools.partial(attend, (None,) * sub))

    @pl.when(kc == last)
    def _():
        for h in range(ATT_HEADS):
            g, half = h // 2, (h % 2) * qb
            ol = (acc_scr[g, :, half:half + qb] / l_scr[g, 0:1, half:half + qb]).astype(BF16)
            oh = jnp.dot(wuvt_ref[h], ol, preferred_element_type=F32)
            o_ref[:, h * hd:(h + 1) * hd] = oh.T.astype(o_ref.dtype)


def _rel_bucket_table():
    n = np.arange(2 * Q_BLOCK)
    max_exact = REL_BUCKETS // 2
    nf = np.maximum(n, 1).astype(np.float32)
    large = max_exact + (np.log(nf / np.float32(max_exact)) / np.float32(math.log(REL_MAX_DIST / max_exact))
                         * np.float32(REL_BUCKETS - max_exact)).astype(np.int32)
    large = np.minimum(large, REL_BUCKETS - 1)
    return np.where(n < max_exact, n, large)


def _attention(qcat, ckv, ckv_t, scores, thr, w_uk, w_uv_t, layer, rel_bias, *, batch, seq):
    t = batch * seq
    n_blk = seq // Q_BLOCK
    n_kc = seq // KEY_CHUNK
    bucket = _rel_bucket_table()
    assert bucket[Q_BLOCK + 1:].min() == REL_BUCKETS - 1
    kk = np.arange(Q_BLOCK)[:, None]
    qq = np.arange(Q_BLOCK)[None, :]
    rb = rel_bias.astype(F32)
    rb = rb - rb[REL_BUCKETS - 1]
    diag = jnp.where(jnp.asarray(kk <= qq)[None], jnp.moveaxis(rb[bucket[np.maximum(qq - kk, 0)]], -1, 0), 0.0)
    prev = jnp.moveaxis(rb[bucket[Q_BLOCK + qq - kk]], -1, 0)
    n_pair = ATT_HEADS // 2
    pair = 2 * Q_BLOCK

    def chunk(jj, kc):
        return jnp.minimum(kc, (jj * Q_BLOCK) // KEY_CHUNK)

    qrow = lambda b, jj, kc: (b * n_blk + jj, 0)
    full3 = lambda b, jj, kc: (0, 0, 0)
    return pl.pallas_call(
        _attn_kernel,
        out_shape=jax.ShapeDtypeStruct((t, ATT_HEADS * ATT_HEAD_DIM), BF16),
        grid=(batch, n_blk, n_kc),
        in_specs=[pl.BlockSpec((Q_BLOCK, ATT_HEADS * ATT_HEAD_DIM), qrow),
                  pl.BlockSpec((KEY_CHUNK, KV_LORA), lambda b, jj, kc: (b * n_kc + chunk(jj, kc), 0)),
                  pl.BlockSpec((None, KV_LORA, KEY_CHUNK), lambda b, jj, kc: (b, 0, chunk(jj, kc))),
                  pl.BlockSpec((None, None, KEY_CHUNK, Q_BLOCK), lambda b, jj, kc: (b, jj, chunk(jj, kc), 0)),
                  pl.BlockSpec((None, None, 8, Q_BLOCK), lambda b, jj, kc: (b, jj, 0, 0)),
                  _layer_spec(w_uk, layer),
                  _layer_spec(w_uv_t, layer),
                  pl.BlockSpec((ATT_HEADS, Q_BLOCK, Q_BLOCK), full3),
                  pl.BlockSpec((ATT_HEADS, Q_BLOCK, Q_BLOCK), full3)],
        out_specs=pl.BlockSpec((Q_BLOCK, ATT_HEADS * ATT_HEAD_DIM), qrow),
        scratch_shapes=[pltpu.VMEM((ATT_HEADS * Q_BLOCK, KV_LORA), BF16),
                        pltpu.VMEM((n_pair, 8, pair), F32),
                        pltpu.VMEM((n_pair, 8, pair), F32),
                        pltpu.VMEM((n_pair, KV_LORA, pair), F32)],
        compiler_params=_params("parallel", "parallel", "arbitrary"),
        name="latent_attention",
    )(qcat, ckv, ckv_t, scores, thr, w_uk, w_uv_t, diag, prev)


def _dsa_mixer(h, res_g, res_b, layer, rel_bias, w_in_pad, q_norm, kv_norm, w_q, w_uk, w_uv_t, w_out,
               *, batch, seq):
    cq, ckv, ckv_t, kidx, widx_t = _dsa_proj(h, w_in_pad, layer, q_norm, kv_norm, batch=batch, seq=seq)
    qcat = _linear(cq, w_q, layer, tm=512, out_dtype=BF16)
    scores, thr = _indexer(qcat, kidx, widx_t, batch=batch, seq=seq)
    o = _attention(qcat, ckv, ckv_t, scores, thr, w_uk, w_uv_t, layer, rel_bias, batch=batch, seq=seq)
    return _linear_ln(o, w_out, layer, h, res_g, res_b, tm=256)


def _router_kernel(x_ref, w_ref, b_ref, route_ref, cnt_ref, cnt_scr):
    x = x_ref[...]
    x_hi = x.astype(BF16)
    x_lo = (x - x_hi.astype(F32)).astype(BF16)
    logits = (jnp.dot(x_hi, w_ref[0], preferred_element_type=F32)
              + jnp.dot(x_lo, w_ref[0], preferred_element_type=F32)
              + jnp.dot(x_hi, w_ref[1], preferred_element_type=F32)) + b_ref[...]
    lane = lax.broadcasted_iota(jnp.int32, logits.shape, 1)
    big = jnp.int32(LANES)
    is_group = (lane >= MOE_EXPERTS) & (lane < MOE_EXPERTS + MOE_GROUPS)
    gl = jnp.where(is_group, logits, -jnp.inf)
    gm = jnp.max(gl, axis=1, keepdims=True)
    g_p = 1.0 / jnp.sum(jnp.exp(gl - gm), axis=1, keepdims=True)
    g_idx = jnp.min(jnp.where(gl == gm, lane, big), axis=1, keepdims=True) - MOE_EXPERTS
    in_group = (lane < MOE_EXPERTS) & (jnp.right_shift(lane, 3) == g_idx)
    el = jnp.where(in_group, logits, -jnp.inf)
    em = jnp.max(el, axis=1, keepdims=True)
    ee = jnp.exp(el - em)
    prob = ee / jnp.sum(ee, axis=1, keepdims=True)
    p1 = jnp.max(prob, axis=1, keepdims=True)
    i1 = jnp.min(jnp.where(in_group & (prob == p1), lane, big), axis=1, keepdims=True)
    rest = in_group & (lane != i1)
    p2 = jnp.max(jnp.where(rest, prob, -1.0), axis=1, keepdims=True)
    i2 = jnp.min(jnp.where(rest & (prob == p2), lane, big), axis=1, keepdims=True)
    den = p1 + p2
    g1 = g_p * (p1 / den)
    g2 = g_p * (p2 / den)

    @pl.when(pl.program_id(0) == 0)
    def _():
        cnt_scr[...] = jnp.zeros(cnt_scr.shape, F32)

    tm = logits.shape[0]
    oh1 = (lane == i1).astype(BF16)
    oh2 = (lane == i2).astype(BF16)
    r_i = lax.broadcasted_iota(jnp.int32, (tm, tm), 0)
    c_i = lax.broadcasted_iota(jnp.int32, (tm, tm), 1)
    tri = (c_i < r_i).astype(BF16)
    pre1 = jnp.dot(tri, oh1, preferred_element_type=F32)
    pre2 = jnp.dot(tri, oh2, preferred_element_type=F32)
    tot1 = jnp.sum(oh1.astype(F32), axis=0, keepdims=True)
    tot2 = jnp.sum(oh2.astype(F32), axis=0, keepdims=True)
    base = cnt_scr[...]
    rank1 = jnp.sum(jnp.where(lane == i1, base + pre1, 0.0), axis=1, keepdims=True)
    rank2 = jnp.sum(jnp.where(lane == i2, base + tot1 + pre2, 0.0), axis=1, keepdims=True)
    cnt_scr[...] = base + tot1 + tot2
    cnt_ref[...] = jnp.broadcast_to(cnt_scr[...], cnt_ref.shape)

    out = jnp.where(lane == 0, i1.astype(F32), jnp.where(lane == 1, i2.astype(F32), 0.0))
    out = jnp.where(lane == 2, g1, jnp.where(lane == 3, g2, out))
    out = jnp.where(lane == 4, rank1, jnp.where(lane == 5, rank2, out))
    route_ref[...] = out


def _router(h, w_r, b_r, *, tm=512):
    t = h.shape[0]
    return pl.pallas_call(
        _router_kernel,
        out_shape=(jax.ShapeDtypeStruct((t, LANES), F32), jax.ShapeDtypeStruct((8, LANES), F32)),
        grid=(t // tm,),
        in_specs=[pl.BlockSpec((tm, D_MODEL), lambda i: (i, 0)),
                  pl.BlockSpec((2, D_MODEL, LANES), lambda i: (0, 0, 0)),
                  pl.BlockSpec((1, LANES), lambda i: (0, 0))],
        out_specs=(pl.BlockSpec((tm, LANES), lambda i: (i, 0)),
                   pl.BlockSpec((8, LANES), lambda i: (0, 0))),
        scratch_shapes=[pltpu.VMEM((1, LANES), F32)],
        compiler_params=_params("arbitrary"),
        name="moe_router",
    )(h, w_r, b_r)


def _moe_expert_kernel(te_ref, nu_ref, rt_ref, x_hbm, wg_ref, wu_ref, wd_ref, y_ref,
                       xbuf, sem, wgb, wub, wdb):
    i = pl.program_id(0)
    n_used = nu_ref[0]
    nbuf = GATHER_AHEAD + 1
    slot = lax.rem(i, nbuf)
    tme = EXPERT_TILE
    pr = PACK_ROWS

    def start_gather(tile, buf):
        def body(r, carry):
            src = pl.multiple_of(rt_ref[tile * tme + r] * pr, pr)
            dst = pl.multiple_of(r * pr, pr)
            pltpu.make_async_copy(x_hbm.at[pl.ds(src, pr)], xbuf.at[buf, pl.ds(dst, pr)], sem.at[buf]).start()
            return carry
        lax.fori_loop(0, tme, body, 0, unroll=8)

    for j in range(GATHER_AHEAD):
        pl.when((i == 0) & (j < n_used))(functools.partial(start_gather, j, j))

    @pl.when(i + GATHER_AHEAD < n_used)
    def _():
        start_gather(i + GATHER_AHEAD, lax.rem(i + GATHER_AHEAD, nbuf))

    te = te_ref[i]
    prev = te_ref[jnp.maximum(i - 1, 0)]

    @pl.when((i == 0) | (te != prev))
    def _():
        wgb[...] = wg_ref[...].astype(BF16)
        wub[...] = wu_ref[...].astype(BF16)
        wdb[...] = wd_ref[...].astype(BF16)

    @pl.when(i < n_used)
    def _():
        pltpu.make_async_copy(x_hbm.at[pl.ds(0, tme * pr)], xbuf.at[slot], sem.at[slot]).wait()
        xb = _load_packed(xbuf.at[slot], tme).astype(BF16)
        gp = jnp.dot(xb, wgb[...], preferred_element_type=F32)
        up = jnp.dot(xb, wub[...], preferred_element_type=F32)
        hidden = (gp * jax.nn.sigmoid(gp)) * up
        _store_packed(y_ref, jnp.dot(hidden.astype(BF16), wdb[...], preferred_element_type=F32))

    @pl.when(i >= n_used)
    def _():
        y_ref[...] = jnp.zeros(y_ref.shape, y_ref.dtype)


def _moe_experts(tile_expert, n_used, row_token, h_packed, w_gate, w_up, w_down, layer):
    n_tiles = tile_expert.shape[0]
    rows = EXPERT_TILE * PACK_ROWS
    wspec = lambda a, b: pl.BlockSpec((None, None, a, b), lambda i, te, nu, rt: (layer, te[i], 0, 0))
    return pl.pallas_call(
        _moe_expert_kernel,
        out_shape=jax.ShapeDtypeStruct((n_tiles * rows, LANES), jnp.uint32),
        grid_spec=pltpu.PrefetchScalarGridSpec(
            num_scalar_prefetch=3,
            grid=(n_tiles,),
            in_specs=[pl.BlockSpec(memory_space=pl.ANY),
                      wspec(D_MODEL, MOE_FF), wspec(D_MODEL, MOE_FF), wspec(MOE_FF, D_MODEL)],
            out_specs=pl.BlockSpec((rows, LANES), lambda i, te, nu, rt: (i, 0)),
            scratch_shapes=[pltpu.VMEM((GATHER_AHEAD + 1, rows, LANES), jnp.uint32),
                            pltpu.SemaphoreType.DMA((GATHER_AHEAD + 1,)),
                            pltpu.VMEM((D_MODEL, MOE_FF), BF16),
                            pltpu.VMEM((D_MODEL, MOE_FF), BF16),
                            pltpu.VMEM((MOE_FF, D_MODEL), BF16)]),
        compiler_params=_params("arbitrary"),
        name="moe_experts",
    )(tile_expert, n_used, row_token, h_packed, w_gate, w_up, w_down)


def _moe_combine_kernel(slot_ref, h_ref, route_ref, y_hbm, g_ref, b_ref, o_ref, ybuf, sem):
    i = pl.program_id(0)
    n = pl.num_programs(0)
    nbuf = GATHER_AHEAD + 1
    slot = lax.rem(i, nbuf)
    tm = h_ref.shape[0]
    pr = PACK_ROWS

    def start_gather(tile, buf):
        def body(r, carry):
            for k in range(2):
                src = pl.multiple_of(slot_ref[(tile * tm + r) * 2 + k] * pr, pr)
                dst = pl.multiple_of((k * tm + r) * pr, pr)
                pltpu.make_async_copy(y_hbm.at[pl.ds(src, pr)], ybuf.at[buf, pl.ds(dst, pr)], sem.at[buf]).start()
            return carry
        lax.fori_loop(0, tm, body, 0, unroll=4)

    for j in range(GATHER_AHEAD):
        pl.when((i == 0) & (j < n))(functools.partial(start_gather, j, j))

    @pl.when(i + GATHER_AHEAD < n)
    def _():
        start_gather(i + GATHER_AHEAD, lax.rem(i + GATHER_AHEAD, nbuf))

    pltpu.make_async_copy(y_hbm.at[pl.ds(0, 2 * tm * pr)], ybuf.at[slot], sem.at[slot]).wait()
    y1 = _load_packed(ybuf.at[slot, pl.ds(0, tm * pr)], tm)
    y2 = _load_packed(ybuf.at[slot, pl.ds(tm * pr, tm * pr)], tm)
    ffn = route_ref[:, 2:3] * y1 + route_ref[:, 3:4] * y2
    o_ref[...] = _layer_norm_rows(DN_ALPHA * h_ref[...] + ffn, g_ref[...], b_ref[...])


def _moe_combine(slots, h, route, y, ln_g, ln_b, *, tm):
    t = h.shape[0]
    row = lambda w: pl.BlockSpec((tm, w), lambda i, s: (i, 0))
    vec = pl.BlockSpec((1, D_MODEL), lambda i, s: (0, 0))
    return pl.pallas_call(
        _moe_combine_kernel,
        out_shape=jax.ShapeDtypeStruct((t, D_MODEL), F32),
        grid_spec=pltpu.PrefetchScalarGridSpec(
            num_scalar_prefetch=1,
            grid=(t // tm,),
            in_specs=[row(D_MODEL), row(LANES), pl.BlockSpec(memory_space=pl.ANY), vec, vec],
            out_specs=row(D_MODEL),
            scratch_shapes=[pltpu.VMEM((GATHER_AHEAD + 1, 2 * tm * PACK_ROWS, LANES), jnp.uint32),
                            pltpu.SemaphoreType.DMA((GATHER_AHEAD + 1,))]),
        compiler_params=_params("arbitrary"),
        name="moe_combine",
    )(slots, h, route, y, ln_g.reshape(1, D_MODEL), ln_b.reshape(1, D_MODEL))


def _moe_plan(route, counts, n_tiles):
    t = route.shape[0]
    tme = EXPERT_TILE
    cnt = counts[0, :MOE_EXPERTS].astype(jnp.int32)
    tiles = (cnt + tme - 1) // tme
    tile_end = jnp.cumsum(tiles)
    tile_start = tile_end - tiles
    n_used = tile_end[-1]
    choice = route[:, 0:2].astype(jnp.int32)
    rank = route[:, 4:6].astype(jnp.int32)
    expert_ids = jnp.arange(MOE_EXPERTS, dtype=jnp.int32)
    start = jnp.sum(jnp.where(choice[..., None] == expert_ids, tile_start, 0), axis=-1)
    slots = (start * tme + rank).reshape(-1)
    tile_ids = jnp.arange(n_tiles, dtype=jnp.int32)
    te = jnp.sum((tile_ids[:, None] >= tile_end[None, :]).astype(jnp.int32), axis=1)
    last_e = jnp.sum((n_used - 1 >= tile_end).astype(jnp.int32))
    te = jnp.where(tile_ids < n_used, te, last_e)
    row_token = jnp.zeros((n_tiles * tme,), jnp.int32).at[slots].set(jnp.arange(2 * t, dtype=jnp.int32) // 2)
    return te, n_used.reshape(1).astype(jnp.int32), row_token, slots.astype(jnp.int32)


def _hier_moe(h, h_packed, layer, w_group, b_group, w_expert, b_expert, w_gate, w_up, w_down, ln_g, ln_b):
    t = h.shape[0]
    pad = LANES - MOE_EXPERTS - MOE_GROUPS
    w_r = jnp.concatenate([w_expert, w_group, jnp.zeros((D_MODEL, pad), F32)], axis=1)
    w_hi = w_r.astype(BF16)
    w_r = jnp.stack([w_hi, (w_r - w_hi.astype(F32)).astype(BF16)])
    b_r = jnp.concatenate([b_expert, b_group, jnp.zeros((pad,), F32)]).reshape(1, LANES)
    route, counts = _router(h, w_r, b_r)
    n_tiles = (2 * t) // EXPERT_TILE + MOE_EXPERTS
    te, n_used, row_token, slots = _moe_plan(route, counts, n_tiles)
    y = _moe_experts(te, n_used, row_token, h_packed, w_gate, w_up, w_down, layer)
    return _moe_combine(slots, h, route, y, ln_g, ln_b, tm=COMBINE_TILE)


def kernel(x, rel_bias, s5_w_in, s5_a_re, s5_a_im, s5_log_dt, s5_b_re, s5_b_im, s5_c_re, s5_c_im, s5_d, s5_w_glu, s5_w_out, dsa_w_in, dsa_q_norm, dsa_kv_norm, dsa_w_uq, dsa_w_qidx, dsa_w_uk, dsa_w_uv, dsa_w_out, moe_w_group, moe_b_group, moe_w_expert, moe_b_expert, moe_w_gate, moe_w_up, moe_w_down, ln_mix_g, ln_mix_b, ln_ffn_g, ln_ffn_b):
    batch, seq, dm = x.shape
    h = x.reshape(batch * seq, dm)
    s5_w_in, s5_w_glu, s5_w_out = (w.astype(BF16) for w in (s5_w_in, s5_w_glu, s5_w_out))
    n_pad = Q_LORA + KV_LORA + IDX_DIM + LANES - dsa_w_in.shape[-1]
    dsa_w_in_pad = jnp.pad(dsa_w_in, ((0, 0), (0, 0), (0, n_pad))).astype(BF16)
    dsa_w_q = jnp.concatenate([dsa_w_uq, dsa_w_qidx], axis=-1).astype(BF16)
    dsa_w_uk, dsa_w_out = dsa_w_uk.astype(BF16), dsa_w_out.astype(BF16)
    dsa_w_uv_t = jnp.swapaxes(dsa_w_uv, -1, -2).astype(BF16)
    for i in range(DEPTH):
        j = i // 2
        if i % 2 == 0:
            h, hp = _s5_mixer(h, ln_mix_g[i], ln_mix_b[i], j, s5_w_in, s5_a_re[j], s5_a_im[j], s5_log_dt[j],
                              s5_b_re[j], s5_b_im[j], s5_c_re[j], s5_c_im[j], s5_d[j], s5_w_glu, s5_w_out,
                              batch=batch, seq=seq)
        else:
            h, hp = _dsa_mixer(h, ln_mix_g[i], ln_mix_b[i], j, rel_bias, dsa_w_in_pad, dsa_q_norm[j],
                               dsa_kv_norm[j], dsa_w_q, dsa_w_uk, dsa_w_uv_t, dsa_w_out, batch=batch, seq=seq)
        h = _hier_moe(h, hp, i, moe_w_group[i], moe_b_group[i], moe_w_expert[i], moe_b_expert[i],
                      moe_w_gate, moe_w_up, moe_w_down, ln_ffn_g[i], ln_ffn_b[i])
    return h.reshape(batch, seq, dm)
```

```python
import functools
import math

import numpy as np
import jax
import jax.numpy as jnp
from jax import lax
from jax.experimental import pallas as pl
from jax.experimental.pallas import tpu as pltpu

F32 = jnp.float32
BF16 = jnp.bfloat16

D_MODEL = 2048
DEPTH = 4
S5_GROUP = 16
S5_GROUPS = D_MODEL // S5_GROUP
S5_STATE = 64
S5_CHUNK = 64
ATT_HEADS = 16
ATT_HEAD_DIM = 128
Q_LORA = 512
KV_LORA = 512
IDX_HEADS = 16
IDX_DIM = 128
IDX_TOPK = 256
Q_BLOCK = 128
KEY_CHUNK = 512
REL_BUCKETS = 32
REL_MAX_DIST = 128
MOE_GROUPS = 4
MOE_PER_GROUP = 8
MOE_EXPERTS = 32
MOE_FF = 256
EXPERT_TILE = 256
COMBINE_TILE = 256
GATHER_AHEAD = 2
DN_ALPHA = (2 * DEPTH) ** 0.25
LN_EPS = 1e-5
RMS_EPS = 1e-6

LANES = 128
MASKED = -1e30
VMEM_LIMIT = 56 * 1024 * 1024


def _params(*sem):
    return pltpu.CompilerParams(dimension_semantics=sem, vmem_limit_bytes=VMEM_LIMIT)


def _layer_norm_rows(y, g, b):
    mean = jnp.mean(y, axis=-1, keepdims=True)
    yc = y - mean
    var = jnp.mean(yc * yc, axis=-1, keepdims=True)
    return yc * lax.rsqrt(var + LN_EPS) * g + b


def _mm_plain_kernel(x_ref, w_ref, o_ref, *, tn):
    xb = x_ref[...].astype(BF16)
    for n0 in range(0, o_ref.shape[1], tn):
        acc = jnp.dot(xb, w_ref[:, n0:n0 + tn], preferred_element_type=F32)
        o_ref[:, n0:n0 + tn] = acc.astype(o_ref.dtype)


def _mm_glu_kernel(z_ref, w_ref, o_ref, *, tn):
    zb = z_ref[...]
    for n0 in range(0, o_ref.shape[1], tn):
        acc = jnp.dot(zb, w_ref[:, n0:n0 + tn], preferred_element_type=F32)
        zc = z_ref[:, n0:n0 + tn].astype(F32)
        o_ref[:, n0:n0 + tn] = (zc * jax.nn.sigmoid(acc)).astype(o_ref.dtype)


PACK_ROWS = D_MODEL // (2 * LANES)


def _store_packed(o_ref, y):
    half = y.shape[1] // 2
    bits = lambda v: lax.bitcast_convert_type(v.astype(BF16).astype(F32), jnp.uint32)
    word = (bits(y[:, half:]) & jnp.uint32(0xFFFF0000)) | (bits(y[:, :half]) >> 16)
    rows = y.shape[0]
    for c in range(PACK_ROWS):
        o_ref[pl.ds(c, rows, stride=PACK_ROWS), :] = word[:, c * LANES:(c + 1) * LANES]


def _load_packed(x_ref, rows):
    word = jnp.concatenate([x_ref[pl.ds(c, rows, stride=PACK_ROWS), :] for c in range(PACK_ROWS)], axis=1)
    lo = lax.bitcast_convert_type(word << 16, F32)
    hi = lax.bitcast_convert_type(word & jnp.uint32(0xFFFF0000), F32)
    return jnp.concatenate([lo, hi], axis=1)


def _mm_ln_kernel(x_ref, w_ref, res_ref, g_ref, b_ref, o_ref, op_ref, *, tn):
    xb = x_ref[...].astype(BF16)
    for n0 in range(0, o_ref.shape[1], tn):
        acc = jnp.dot(xb, w_ref[:, n0:n0 + tn], preferred_element_type=F32)
        o_ref[:, n0:n0 + tn] = DN_ALPHA * res_ref[:, n0:n0 + tn] + acc
    y = _layer_norm_rows(o_ref[...], g_ref[...], b_ref[...])
    o_ref[...] = y
    _store_packed(op_ref, y)


def _layer_spec(w_all, layer):
    shape = w_all.shape[1:]
    return pl.BlockSpec((None,) + shape, lambda *_: (layer,) + (0,) * len(shape))


def _linear(x, w_all, layer, *, tm, out_dtype, tn=512):
    m, k = x.shape
    n = w_all.shape[-1]
    return pl.pallas_call(
        functools.partial(_mm_plain_kernel, tn=tn),
        out_shape=jax.ShapeDtypeStruct((m, n), out_dtype),
        grid=(m // tm,),
        in_specs=[pl.BlockSpec((tm, k), lambda i: (i, 0)), _layer_spec(w_all, layer)],
        out_specs=pl.BlockSpec((tm, n), lambda i: (i, 0)),
        compiler_params=_params("parallel"),
        name="linear",
    )(x, w_all)


def _linear_glu(z, w_all, layer, *, tm, tn=512):
    m, k = z.shape
    return pl.pallas_call(
        functools.partial(_mm_glu_kernel, tn=tn),
        out_shape=jax.ShapeDtypeStruct((m, k), BF16),
        grid=(m // tm,),
        in_specs=[pl.BlockSpec((tm, k), lambda i: (i, 0)), _layer_spec(w_all, layer)],
        out_specs=pl.BlockSpec((tm, k), lambda i: (i, 0)),
        compiler_params=_params("parallel"),
        name="linear_glu",
    )(z, w_all)


def _linear_ln(x, w_all, layer, res, g, b, *, tm, tn=512):
    m, k = x.shape
    n = w_all.shape[-1]
    return pl.pallas_call(
        functools.partial(_mm_ln_kernel, tn=tn),
        out_shape=(jax.ShapeDtypeStruct((m, n), F32), jax.ShapeDtypeStruct((m * PACK_ROWS, LANES), jnp.uint32)),
        grid=(m // tm,),
        in_specs=[pl.BlockSpec((tm, k), lambda i: (i, 0)),
                  _layer_spec(w_all, layer),
                  pl.BlockSpec((tm, n), lambda i: (i, 0)),
                  pl.BlockSpec((1, n), lambda i: (0, 0)),
                  pl.BlockSpec((1, n), lambda i: (0, 0))],
        out_specs=(pl.BlockSpec((tm, n), lambda i: (i, 0)),
                   pl.BlockSpec((tm * PACK_ROWS, LANES), lambda i: (i, 0))),
        compiler_params=_params("parallel"),
        name="linear_ln",
    )(x, w_all, res, g.reshape(1, n), b.reshape(1, n))


def _gelu_tanh(x):
    return 0.5 * x * (1.0 + jnp.tanh(math.sqrt(2.0 / math.pi) * (x + 0.044715 * x * x * x)))


def _s5_kernel(u_ref, ar_ref, ai_ref, ldt_ref, br_ref, bi_ref, cr_ref, ci_ref, d_ref, o_ref, a_scr,
               *, n_chunks):
    cs, p, n = S5_CHUNK, S5_GROUP, S5_STATE
    width = cs * p
    n2 = 2 * n

    @pl.when(pl.program_id(0) == 0)
    def _():
        a_scr[...] = jnp.zeros(a_scr.shape, a_scr.dtype)

    dt = jnp.exp(ldt_ref[0])
    lam_r = jnp.minimum(ar_ref[0], -1e-4)
    lam_i = ai_ref[0]
    lo_half = lax.broadcasted_iota(jnp.int32, (1, n2), 1) < n

    def a_pow(tau):
        mag = jnp.exp(lam_r * dt * tau)
        ph = lam_i * dt * tau
        return mag * jnp.cos(ph), mag * jnp.sin(ph)

    tau = lax.broadcasted_iota(jnp.int32, (cs, 1), 0).astype(F32)
    a0r, a0i = a_pow(tau)
    a1r, a1i = a_pow(tau + 1.0)
    avr, avi = a_pow((cs - 1.0) - tau)

    abr, abi = a1r[0:1], a1i[0:1]
    den = lam_r * lam_r + lam_i * lam_i
    kr = ((abr - 1.0) * lam_r + abi * lam_i) / den
    ki = (abi * lam_r - (abr - 1.0) * lam_i) / den
    b_r, b_i = br_ref[0], bi_ref[0]
    bbr = kr * b_r - ki * b_i
    bbi = kr * b_i + ki * b_r
    c_r, c_i = cr_ref[0], ci_ref[0]

    def rep_rows(x):
        return jnp.concatenate([jnp.broadcast_to(x[t:t + 1, :], (p, n2)) for t in range(cs)], axis=0)

    def tile_rows(x):
        return jnp.concatenate([x] * cs, axis=0)

    def sel(lo, hi):
        return jnp.where(lo_half, lo, hi)

    crr, cii = tile_rows(c_r), tile_rows(c_i)
    brr, bii = tile_rows(bbr), tile_rows(bbi)
    c_taps = crr * rep_rows(sel(a0r, -a0i)) - cii * rep_rows(sel(a0i, a0r))
    c_out = crr * rep_rows(sel(a1r, -a1i)) - cii * rep_rows(sel(a1i, a1r))
    b_in = brr * rep_rows(sel(avr, avi)) + bii * rep_rows(sel(-avi, avr))
    b_in_sw = brr * rep_rows(sel(avi, avr)) + bii * rep_rows(sel(avr, -avi))

    b_mat = sel(bbr, bbi)
    taps = lax.dot_general(b_mat, c_taps, (((1,), (1,)), ((), ())),
                           precision=lax.Precision.HIGHEST, preferred_element_type=F32)

    lane = lax.broadcasted_iota(jnp.int32, (p, width), 1)
    band = [taps]
    for j in range(1, LANES // p):
        band.append(jnp.where(lane >= p * j, pltpu.roll(taps, p * j, axis=1), 0.0))
    band = jnp.concatenate(band, axis=0)
    n_blk = width // LANES
    band_t = jnp.concatenate([band[:, d * LANES:(d + 1) * LANES].T for d in reversed(range(n_blk))],
                             axis=1).astype(BF16)
    for q in range(n_blk):
        a_scr[q * LANES:(q + 1) * LANES, 0:(q + 1) * LANES] = band_t[:, (n_blk - 1 - q) * LANES:width]
    a_scr[width:width + n2, :] = b_in.T.astype(BF16)
    a_scr[width + n2:width + 2 * n2, :] = b_in_sw.T.astype(BF16)

    u = u_ref[...].reshape(width, u_ref.shape[2])
    res = jnp.dot(a_scr[...], u, preferred_element_type=F32)
    y = res[0:width]
    x_in = res[width:width + n2]
    x_sw = res[width + n2:width + 2 * n2]

    n_lanes = x_in.shape[1]
    levels = n_chunks.bit_length() - 1
    mults = []
    for k in range(levels):
        mr, mi = a_pow(jnp.full((1, 1), float(cs * 2 ** k), F32))
        mults += [mr, sel(-mi, mi), sel(mi, -mi)]
    mults = jnp.concatenate(mults + [jnp.zeros((n2 - len(mults), n2), F32)], axis=0)
    mcol = mults.T
    chunk_id = lax.rem(lax.broadcasted_iota(jnp.int32, (n2, n_lanes), 1), n_chunks)
    for k in range(levels):
        sh = 2 ** k
        ok = chunk_id >= sh
        xs = jnp.where(ok, pltpu.roll(x_in, sh, axis=1), 0.0)
        xs_sw = jnp.where(ok, pltpu.roll(x_sw, sh, axis=1), 0.0)
        m_rr, m_ni, m_pi = (mcol[:, 3 * k + j:3 * k + j + 1] for j in range(3))
        x_in, x_sw = (x_in + xs * m_rr + xs_sw * m_ni, x_sw + xs_sw * m_rr + xs * m_pi)
    h_start = jnp.where(chunk_id >= 1, pltpu.roll(x_in, 1, axis=1), 0.0).astype(BF16)

    y = y + jnp.dot(c_out.astype(BF16), h_start, preferred_element_type=F32)
    y3 = y.reshape(cs, p, n_lanes) + d_ref[0] * u_ref[...].astype(F32)
    o_ref[...] = _gelu_tanh(y3).astype(o_ref.dtype)


def _s5_core(u_t, a_re, a_im, log_dt, b_re, b_im, c_re, c_im, d, *, n_chunks):
    cs, dm, n_lanes = u_t.shape
    g = dm // S5_GROUP
    assert n_chunks & (n_chunks - 1) == 0
    dup = lambda x: jnp.concatenate([x, x], axis=-1)
    a_re2 = dup(a_re).reshape(g, 1, 2 * S5_STATE)
    a_im2 = dup(a_im).reshape(g, 1, 2 * S5_STATE)
    ldt = log_dt.reshape(g, 1, 1)
    bt_re2 = dup(jnp.swapaxes(b_re, 1, 2))
    bt_im2 = dup(jnp.swapaxes(b_im, 1, 2))
    c_re2 = dup(c_re)
    c_im2 = dup(c_im)
    d_c = d.reshape(g, S5_GROUP, 1)
    width = cs * S5_GROUP
    vec = lambda w: pl.BlockSpec((1, 1, w), lambda i: (i, 0, 0))
    mat = pl.BlockSpec((1, S5_GROUP, 2 * S5_STATE), lambda i: (i, 0, 0))
    blk = pl.BlockSpec((cs, S5_GROUP, n_lanes), lambda i: (0, i, 0))
    return pl.pallas_call(
        functools.partial(_s5_kernel, n_chunks=n_chunks),
        out_shape=jax.ShapeDtypeStruct((cs, dm, n_lanes), BF16),
        grid=(g,),
        in_specs=[blk, vec(2 * S5_STATE), vec(2 * S5_STATE), vec(1), mat, mat, mat, mat,
                  pl.BlockSpec((1, S5_GROUP, 1), lambda i: (i, 0, 0))],
        out_specs=blk,
        scratch_shapes=[pltpu.VMEM((width + 4 * S5_STATE, width), BF16)],
        compiler_params=_params("arbitrary"),
        name="s5_core",
    )(u_t, a_re2, a_im2, ldt, bt_re2, bt_im2, c_re2, c_im2, d_c)


S5_STEP = 8
LANE_ROWS = 128


def _s5_in_kernel(x_ref, w_ref, o_ref, u_scr):
    rows = LANE_ROWS * S5_STEP
    dm = x_ref.shape[2]
    xb = x_ref[...].reshape(rows, dm).astype(BF16)
    for n0 in range(0, dm, 512):
        acc = jnp.dot(xb, w_ref[:, n0:n0 + 512], preferred_element_type=F32)
        for c in range(512 // LANES):
            u_scr[n0 // LANES + c] = acc[:, c * LANES:(c + 1) * LANES]
    for j in range(S5_STEP):
        for cb in range(dm // LANES):
            tile = u_scr[cb, pl.ds(j, LANE_ROWS, stride=S5_STEP), :]
            o_ref[j, cb * LANES:(cb + 1) * LANES, :] = tile.T.astype(o_ref.dtype)


def _s5_in_proj(h3, w_all, layer):
    n_bc, cs, dm = h3.shape
    return pl.pallas_call(
        _s5_in_kernel,
        out_shape=jax.ShapeDtypeStruct((cs, dm, n_bc), BF16),
        grid=(cs // S5_STEP, n_bc // LANE_ROWS),
        in_specs=[pl.BlockSpec((LANE_ROWS, S5_STEP, dm), lambda s, r: (r, s, 0)), _layer_spec(w_all, layer)],
        out_specs=pl.BlockSpec((S5_STEP, dm, LANE_ROWS), lambda s, r: (s, 0, r)),
        scratch_shapes=[pltpu.VMEM((dm // LANES, LANE_ROWS * S5_STEP, LANES), F32)],
        compiler_params=_params("parallel", "parallel"),
        name="s5_in_proj",
    )(h3, w_all)


def _s5_glu_kernel(z_ref, w_ref, o_ref, z_scr, o_scr):
    nh = pl.program_id(2)
    half = o_ref.shape[2]

    @pl.when(nh == 0)
    def _():
        for j in range(S5_STEP):
            z_scr[j * LANE_ROWS:(j + 1) * LANE_ROWS, :] = z_ref[j].astype(F32).T

    zb = z_scr[...].astype(BF16)
    col0 = pl.multiple_of(nh * half, half)
    for n0 in range(0, half, 512):
        acc = jnp.dot(zb, w_ref[:, n0:n0 + 512], preferred_element_type=F32)
        zg = z_scr[:, pl.ds(col0 + n0, 512)] * jax.nn.sigmoid(acc)
        for j in range(S5_STEP):
            for c in range(512 // LANES):
                o_scr[n0 // LANES + c, pl.ds(j, LANE_ROWS, stride=S5_STEP), :] = (
                    zg[j * LANE_ROWS:(j + 1) * LANE_ROWS, c * LANES:(c + 1) * LANES])
    out = jnp.concatenate([o_scr[cb] for cb in range(half // LANES)], axis=1)
    o_ref[...] = out.reshape(o_ref.shape)


def _s5_glu(z_t, w_all, layer):
    cs, dm, n_bc = z_t.shape
    rows = LANE_ROWS * S5_STEP
    half = dm // 2
    return pl.pallas_call(
        _s5_glu_kernel,
        out_shape=jax.ShapeDtypeStruct((n_bc, cs, dm), F32),
        grid=(cs // S5_STEP, n_bc // LANE_ROWS, 2),
        in_specs=[pl.BlockSpec((S5_STEP, dm, LANE_ROWS), lambda s, r, nh: (s, 0, r)),
                  pl.BlockSpec((None, dm, half), lambda s, r, nh: (layer, 0, nh))],
        out_specs=pl.BlockSpec((LANE_ROWS, S5_STEP, half), lambda s, r, nh: (r, s, nh)),
        scratch_shapes=[pltpu.VMEM((rows, dm), F32), pltpu.VMEM((half // LANES, rows, LANES), F32)],
        compiler_params=_params("parallel", "parallel", "arbitrary"),
        name="s5_glu",
    )(z_t, w_all)


def _s5_mixer(h, res_g, res_b, layer, w_in, a_re, a_im, log_dt, b_re, b_im, c_re, c_im, d, w_glu, w_out,
              *, batch, seq):
    t = batch * seq
    nc = seq // S5_CHUNK
    u_t = _s5_in_proj(h.reshape(batch * nc, S5_CHUNK, D_MODEL), w_in, layer)
    z_t = _s5_core(u_t, a_re, a_im, log_dt, b_re, b_im, c_re, c_im, d, n_chunks=nc)
    zg = _s5_glu(z_t, w_glu, layer).reshape(t, D_MODEL)
    return _linear_ln(zg, w_out, layer, h, res_g, res_b, tm=256)


def _dsa_proj_kernel(x_ref, w_ref, qn_ref, kn_ref, cq_ref, ckv_ref, ckvt_ref, kidx_ref, widxt_ref):
    xb = x_ref[...].astype(BF16)

    def rms(v, g):
        return v * lax.rsqrt(jnp.mean(v * v, axis=-1, keepdims=True) + RMS_EPS) * g

    o1 = Q_LORA + KV_LORA
    cq = jnp.dot(xb, w_ref[:, 0:Q_LORA], preferred_element_type=F32)
    cq_ref[...] = rms(cq, qn_ref[...]).astype(BF16)
    ckv = rms(jnp.dot(xb, w_ref[:, Q_LORA:o1], preferred_element_type=F32), kn_ref[...])
    ckv_ref[...] = ckv.astype(BF16)
    ckvt_ref[...] = ckv.T.astype(BF16)
    kidx_ref[...] = jnp.dot(xb, w_ref[:, o1:o1 + IDX_DIM], preferred_element_type=F32).astype(BF16)
    widx = jnp.dot(xb, w_ref[:, o1 + IDX_DIM:o1 + IDX_DIM + LANES], preferred_element_type=F32)
    widxt_ref[...] = (widx * (IDX_HEADS ** -0.5)).T[0:IDX_HEADS, :]


def _dsa_proj(h, w_in_pad, layer, q_norm, kv_norm, *, batch, seq, tm=512):
    t = h.shape[0]
    per_b = seq // tm
    row = lambda w: pl.BlockSpec((tm, w), lambda i: (i, 0))
    return pl.pallas_call(
        _dsa_proj_kernel,
        out_shape=(jax.ShapeDtypeStruct((t, Q_LORA), BF16), jax.ShapeDtypeStruct((t, KV_LORA), BF16),
                   jax.ShapeDtypeStruct((batch, KV_LORA, seq), BF16),
                   jax.ShapeDtypeStruct((t, IDX_DIM), BF16), jax.ShapeDtypeStruct((IDX_HEADS, t), F32)),
        grid=(t // tm,),
        in_specs=[row(D_MODEL), _layer_spec(w_in_pad, layer),
                  pl.BlockSpec((1, Q_LORA), lambda i: (0, 0)), pl.BlockSpec((1, KV_LORA), lambda i: (0, 0))],
        out_specs=(row(Q_LORA), row(KV_LORA),
                   pl.BlockSpec((None, KV_LORA, tm), lambda i: (i // per_b, 0, i % per_b)),
                   row(IDX_DIM), pl.BlockSpec((IDX_HEADS, tm), lambda i: (0, i))),
        compiler_params=_params("parallel"),
        name="dsa_proj",
    )(h, w_in_pad, q_norm.reshape(1, Q_LORA), kv_norm.reshape(1, KV_LORA))


def _reduce_rows(op, x):
    rows, lanes = x.shape
    part = op(x.reshape(rows // 64, 8, 8, lanes), axis=0)
    return op(op(part, axis=0), axis=0, keepdims=True)


def _indexer_kernel(q_ref, k_ref, w_ref, s_ref, thr_ref, key_scr, *, seq):
    jb = pl.program_id(1)
    n_sel = float(min(IDX_TOPK, seq // 4))
    int_min = jnp.int32(-2 ** 31)
    flip = jnp.int32(0x7FFFFFFF)

    def run(width):
        k = k_ref[0:width, :]
        wrow = w_ref[...] * (IDX_DIM ** -0.5)
        acc = jnp.zeros((width, Q_BLOCK), F32)
        for g in range(IDX_HEADS // 2):
            qg = jnp.concatenate([q_ref[:, h * IDX_DIM:(h + 1) * IDX_DIM] for h in (2 * g, 2 * g + 1)], axis=0)
            s = lax.dot_general(k, qg, (((1,), (1,)), ((), ())), preferred_element_type=F32)
            acc = (acc + jnp.maximum(s[:, 0:Q_BLOCK], 0.0) * wrow[2 * g:2 * g + 1, :]
                   + jnp.maximum(s[:, Q_BLOCK:2 * Q_BLOCK], 0.0) * wrow[2 * g + 1:2 * g + 2, :])
        k_pos = lax.broadcasted_iota(jnp.int32, (width, Q_BLOCK), 0)
        q_pos = jb * Q_BLOCK + lax.broadcasted_iota(jnp.int32, (width, Q_BLOCK), 1)
        acc = jnp.where(k_pos <= q_pos, acc, -jnp.inf)
        s_ref[0:width, :] = acc
        if width < seq:
            s_ref[width:seq, :] = jnp.full((seq - width, Q_BLOCK), -jnp.inf, F32)

        bits = lax.bitcast_convert_type(acc, jnp.int32)
        key_scr[0:width, :] = jnp.where(bits < 0, bits ^ flip, bits)

        def count_ge(cand):
            return _reduce_rows(jnp.sum, (key_scr[0:width, :] >= cand).astype(F32))

        lo = jnp.where(count_ge(jnp.zeros((1, Q_BLOCK), jnp.int32)) >= n_sel, jnp.int32(0), int_min)

        def body(i, lo):
            cand = lo + jnp.left_shift(jnp.int32(1), jnp.int32(30) - i)
            return jnp.where(count_ge(cand) >= n_sel, cand, lo)

        lo = lax.fori_loop(0, 31, body, lo)
        thr = lax.bitcast_convert_type(jnp.where(lo < 0, lo ^ flip, lo), F32)
        thr_ref[...] = jnp.broadcast_to(thr, thr_ref.shape)

    n_cls = seq // KEY_CHUNK
    per_cls = KEY_CHUNK // Q_BLOCK
    for cls in range(n_cls):
        pl.when(jb // per_cls == cls)(functools.partial(run, KEY_CHUNK * (cls + 1)))


def _indexer(qcat, kidx, widx_t, *, batch, seq):
    n_blk = seq // Q_BLOCK
    return pl.pallas_call(
        functools.partial(_indexer_kernel, seq=seq),
        out_shape=(jax.ShapeDtypeStruct((batch, n_blk, seq, Q_BLOCK), F32),
                   jax.ShapeDtypeStruct((batch, n_blk, 8, Q_BLOCK), F32)),
        grid=(batch, n_blk),
        in_specs=[pl.BlockSpec((Q_BLOCK, IDX_HEADS * IDX_DIM), lambda b, j: (b * n_blk + j, 1)),
                  pl.BlockSpec((seq, IDX_DIM), lambda b, j: (b, 0)),
                  pl.BlockSpec((IDX_HEADS, Q_BLOCK), lambda b, j: (0, b * n_blk + j))],
        out_specs=(pl.BlockSpec((None, None, seq, Q_BLOCK), lambda b, j: (b, j, 0, 0)),
                   pl.BlockSpec((None, None, 8, Q_BLOCK), lambda b, j: (b, j, 0, 0))),
        scratch_shapes=[pltpu.VMEM((seq, Q_BLOCK), jnp.int32)],
        compiler_params=_params("parallel", "parallel"),
        name="indexer",
    )(qcat, kidx, widx_t)


def _attn_kernel(jb_tbl, kc_tbl, q_ref, kv_ref, kvt_ref, s_ref, thr_ref, wuk_ref, wuvt_ref, bd_ref, bp_ref,
                 o_ref, ql_scr, m_scr, l_scr, acc_scr):
    jb = jb_tbl[pl.program_id(1)]
    kc = kc_tbl[pl.program_id(1)]
    last = (jb * Q_BLOCK) // KEY_CHUNK
    hd, qb = ATT_HEAD_DIM, Q_BLOCK
    sub = KEY_CHUNK // qb
    pair = 2 * qb

    @pl.when(kc == 0)
    def _():
        for h in range(ATT_HEADS):
            ql = jnp.dot(q_ref[:, h * hd:(h + 1) * hd], wuk_ref[h], preferred_element_type=F32)
            ql_scr[h * qb:(h + 1) * qb, :] = (ql * (hd ** -0.5)).astype(BF16)
        m_scr[...] = jnp.full(m_scr.shape, MASKED, F32)
        l_scr[...] = jnp.zeros(l_scr.shape, F32)
        acc_scr[...] = jnp.zeros(acc_scr.shape, F32)

    def attend(biases):
        width = len(biases) * qb
        kv = kv_ref[0:width, :]
        kvt = kvt_ref[:, 0:width]
        k_pos = kc * KEY_CHUNK + lax.broadcasted_iota(jnp.int32, (width, qb), 0)
        q_pos = jb * qb + lax.broadcasted_iota(jnp.int32, (width, qb), 1)
        keep = (s_ref[0:width, :] >= thr_ref[0:1, :]) & (k_pos <= q_pos)
        mask_add = jnp.where(keep, 0.0, MASKED)
        mask_add = jnp.concatenate([mask_add, mask_add], axis=1)
        for g in range(ATT_HEADS // 2):
            s = lax.dot_general(kv, ql_scr[g * pair:(g + 1) * pair, :], (((1,), (1,)), ((), ())),
                                preferred_element_type=F32)
            lg = s + mask_add
            if any(b is not None for b in biases):
                parts = []
                for c, b in enumerate(biases):
                    part = lg[c * qb:(c + 1) * qb]
                    if b is not None:
                        part = part + jnp.concatenate([b[2 * g], b[2 * g + 1]], axis=1)
                    parts.append(part)
                lg = jnp.concatenate(parts, axis=0)
            m_prev = m_scr[g, 0:1, :]
            m_new = jnp.maximum(m_prev, _reduce_rows(jnp.max, lg))
            corr = jnp.exp(m_prev - m_new)
            pr = jnp.exp(lg - m_new)
            l_scr[g] = jnp.broadcast_to(corr * l_scr[g, 0:1, :] + _reduce_rows(jnp.sum, pr), (8, pair))
            m_scr[g] = jnp.broadcast_to(m_new, (8, pair))
            pv = jnp.dot(kvt, pr.astype(BF16), preferred_element_type=F32)
            acc_scr[g] = acc_scr[g] * corr + pv

    r = jb - last * sub
    for nb in range(1, sub + 1):
        biases = (None,) * (nb - 2) + ((bp_ref,) if nb >= 2 else ()) + (bd_ref,)
        pl.when((kc == last) & (r == nb - 1))(functools.partial(attend, biases))
    prev_only = (kc == last - 1) & (r == 0)
    pl.when(prev_only)(functools.partial(attend, (None,) * (sub - 1) + (bp_ref,)))
    pl.when((kc < last) & jnp.logical_not(prev_only))(functools.partial(attend, (None,) * sub))

    @pl.when(kc == last)
    def _():
        for h in range(ATT_HEADS):
            g, half = h // 2, (h % 2) * qb
            ol = (acc_scr[g, :, half:half + qb] / l_scr[g, 0:1, half:half + qb]).astype(BF16)
            oh = jnp.dot(wuvt_ref[h], ol, preferred_element_type=F32)
            o_ref[:, h * hd:(h + 1) * hd] = oh.T.astype(o_ref.dtype)


def _rel_bucket_table():
    n = np.arange(2 * Q_BLOCK)
    max_exact = REL_BUCKETS // 2
    nf = np.maximum(n, 1).astype(np.float32)
    large = max_exact + (np.log(nf / np.float32(max_exact)) / np.float32(math.log(REL_MAX_DIST / max_exact))
                         * np.float32(REL_BUCKETS - max_exact)).astype(np.int32)
    large = np.minimum(large, REL_BUCKETS - 1)
    return np.where(n < max_exact, n, large)


def _attention(qcat, ckv, ckv_t, scores, thr, w_uk, w_uv_t, layer, rel_bias, *, batch, seq):
    t = batch * seq
    n_blk = seq // Q_BLOCK
    n_kc = seq // KEY_CHUNK
    bucket = _rel_bucket_table()
    assert bucket[Q_BLOCK + 1:].min() == REL_BUCKETS - 1
    kk = np.arange(Q_BLOCK)[:, None]
    qq = np.arange(Q_BLOCK)[None, :]
    rb = rel_bias.astype(F32)
    rb = rb - rb[REL_BUCKETS - 1]
    lookup = lambda dist: jnp.einsum("bh,kqb->hkq", rb, np.eye(REL_BUCKETS, dtype=np.float32)[bucket[dist]],
                                     precision=lax.Precision.HIGHEST)
    diag = jnp.where(jnp.asarray(kk <= qq)[None], lookup(np.maximum(qq - kk, 0)), 0.0)
    prev = lookup(Q_BLOCK + qq - kk)
    n_pair = ATT_HEADS // 2
    pair = 2 * Q_BLOCK

    steps = [(jj, kc) for jj in range(n_blk) for kc in range((jj * Q_BLOCK) // KEY_CHUNK + 1)]
    jb_tbl = jnp.asarray([s[0] for s in steps], jnp.int32)
    kc_tbl = jnp.asarray([s[1] for s in steps], jnp.int32)

    qrow = lambda b, s, jt, kt: (b * n_blk + jt[s], 0)
    full3 = lambda b, s, jt, kt: (0, 0, 0)
    return pl.pallas_call(
        _attn_kernel,
        out_shape=jax.ShapeDtypeStruct((t, ATT_HEADS * ATT_HEAD_DIM), BF16),
        grid_spec=pltpu.PrefetchScalarGridSpec(
            num_scalar_prefetch=2,
            grid=(batch, len(steps)),
            in_specs=[pl.BlockSpec((Q_BLOCK, ATT_HEADS * ATT_HEAD_DIM), qrow),
                      pl.BlockSpec((KEY_CHUNK, KV_LORA), lambda b, s, jt, kt: (b * n_kc + kt[s], 0)),
                      pl.BlockSpec((None, KV_LORA, KEY_CHUNK), lambda b, s, jt, kt: (b, 0, kt[s])),
                      pl.BlockSpec((None, None, KEY_CHUNK, Q_BLOCK), lambda b, s, jt, kt: (b, jt[s], kt[s], 0)),
                      pl.BlockSpec((None, None, 8, Q_BLOCK), lambda b, s, jt, kt: (b, jt[s], 0, 0)),
                      _layer_spec(w_uk, layer),
                      _layer_spec(w_uv_t, layer),
                      pl.BlockSpec((ATT_HEADS, Q_BLOCK, Q_BLOCK), full3),
                      pl.BlockSpec((ATT_HEADS, Q_BLOCK, Q_BLOCK), full3)],
            out_specs=pl.BlockSpec((Q_BLOCK, ATT_HEADS * ATT_HEAD_DIM), qrow),
            scratch_shapes=[pltpu.VMEM((ATT_HEADS * Q_BLOCK, KV_LORA), BF16),
                            pltpu.VMEM((n_pair, 8, pair), F32),
                            pltpu.VMEM((n_pair, 8, pair), F32),
                            pltpu.VMEM((n_pair, KV_LORA, pair), F32)]),
        compiler_params=_params("parallel", "arbitrary"),
        name="latent_attention",
    )(jb_tbl, kc_tbl, qcat, ckv, ckv_t, scores, thr, w_uk, w_uv_t, diag, prev)


def _dsa_mixer(h, res_g, res_b, layer, rel_bias, w_in_pad, q_norm, kv_norm, w_q, w_uk, w_uv_t, w_out,
               *, batch, seq):
    cq, ckv, ckv_t, kidx, widx_t = _dsa_proj(h, w_in_pad, layer, q_norm, kv_norm, batch=batch, seq=seq)
    qcat = _linear(cq, w_q, layer, tm=512, out_dtype=BF16)
    scores, thr = _indexer(qcat, kidx, widx_t, batch=batch, seq=seq)
    o = _attention(qcat, ckv, ckv_t, scores, thr, w_uk, w_uv_t, layer, rel_bias, batch=batch, seq=seq)
    return _linear_ln(o, w_out, layer, h, res_g, res_b, tm=256)


def _router_kernel(x_ref, w_ref, b_ref, route_ref, cnt_ref, cnt_scr):
    x = x_ref[...]
    x_hi = x.astype(BF16)
    x_lo = (x - x_hi.astype(F32)).astype(BF16)
    logits = (jnp.dot(x_hi, w_ref[0], preferred_element_type=F32)
              + jnp.dot(x_lo, w_ref[0], preferred_element_type=F32)
              + jnp.dot(x_hi, w_ref[1], preferred_element_type=F32)) + b_ref[...]
    lane = lax.broadcasted_iota(jnp.int32, logits.shape, 1)
    big = jnp.int32(LANES)
    is_group = (lane >= MOE_EXPERTS) & (lane < MOE_EXPERTS + MOE_GROUPS)
    gl = jnp.where(is_group, logits, -jnp.inf)
    gm = jnp.max(gl, axis=1, keepdims=True)
    g_p = 1.0 / jnp.sum(jnp.exp(gl - gm), axis=1, keepdims=True)
    g_idx = jnp.min(jnp.where(gl == gm, lane, big), axis=1, keepdims=True) - MOE_EXPERTS
    in_group = (lane < MOE_EXPERTS) & (jnp.right_shift(lane, 3) == g_idx)
    el = jnp.where(in_group, logits, -jnp.inf)
    em = jnp.max(el, axis=1, keepdims=True)
    ee = jnp.exp(el - em)
    prob = ee / jnp.sum(ee, axis=1, keepdims=True)
    p1 = jnp.max(prob, axis=1, keepdims=True)
    i1 = jnp.min(jnp.where(in_group & (prob == p1), lane, big), axis=1, keepdims=True)
    rest = in_group & (lane != i1)
    p2 = jnp.max(jnp.where(rest, prob, -1.0), axis=1, keepdims=True)
    i2 = jnp.min(jnp.where(rest & (prob == p2), lane, big), axis=1, keepdims=True)
    den = p1 + p2
    g1 = g_p * (p1 / den)
    g2 = g_p * (p2 / den)

    @pl.when(pl.program_id(0) == 0)
    def _():
        cnt_scr[...] = jnp.zeros(cnt_scr.shape, F32)

    tm = logits.shape[0]
    oh1 = (lane == i1).astype(BF16)
    oh2 = (lane == i2).astype(BF16)
    r_i = lax.broadcasted_iota(jnp.int32, (tm, tm), 0)
    c_i = lax.broadcasted_iota(jnp.int32, (tm, tm), 1)
    tri = (c_i < r_i).astype(BF16)
    pre1 = jnp.dot(tri, oh1, preferred_element_type=F32)
    pre2 = jnp.dot(tri, oh2, preferred_element_type=F32)
    tot1 = jnp.sum(oh1.astype(F32), axis=0, keepdims=True)
    tot2 = jnp.sum(oh2.astype(F32), axis=0, keepdims=True)
    base = cnt_scr[...]
    rank1 = jnp.sum(jnp.where(lane == i1, base + pre1, 0.0), axis=1, keepdims=True)
    rank2 = jnp.sum(jnp.where(lane == i2, base + tot1 + pre2, 0.0), axis=1, keepdims=True)
    cnt_scr[...] = base + tot1 + tot2
    cnt_ref[...] = jnp.broadcast_to(cnt_scr[...], cnt_ref.shape)

    out = jnp.where(lane == 0, i1.astype(F32), jnp.where(lane == 1, i2.astype(F32), 0.0))
    out = jnp.where(lane == 2, g1, jnp.where(lane == 3, g2, out))
    out = jnp.where(lane == 4, rank1, jnp.where(lane == 5, rank2, out))
    route_ref[...] = out


def _router(h, w_r, b_r, *, tm=512):
    t = h.shape[0]
    return pl.pallas_call(
        _router_kernel,
        out_shape=(jax.ShapeDtypeStruct((t, LANES), F32), jax.ShapeDtypeStruct((8, LANES), F32)),
        grid=(t // tm,),
        in_specs=[pl.BlockSpec((tm, D_MODEL), lambda i: (i, 0)),
                  pl.BlockSpec((2, D_MODEL, LANES), lambda i: (0, 0, 0)),
                  pl.BlockSpec((1, LANES), lambda i: (0, 0))],
        out_specs=(pl.BlockSpec((tm, LANES), lambda i: (i, 0)),
                   pl.BlockSpec((8, LANES), lambda i: (0, 0))),
        scratch_shapes=[pltpu.VMEM((1, LANES), F32)],
        compiler_params=_params("arbitrary"),
        name="moe_router",
    )(h, w_r, b_r)


def _moe_expert_kernel(te_ref, nu_ref, rt_ref, x_hbm, wg_ref, wu_ref, wd_ref, y_ref,
                       xbuf, sem, wgb, wub, wdb):
    i = pl.program_id(0)
    n_used = nu_ref[0]
    nbuf = GATHER_AHEAD + 1
    slot = lax.rem(i, nbuf)
    tme = EXPERT_TILE
    pr = PACK_ROWS

    def start_gather(tile, buf):
        def body(r, carry):
            src = pl.multiple_of(rt_ref[tile * tme + r] * pr, pr)
            dst = pl.multiple_of(r * pr, pr)
            pltpu.make_async_copy(x_hbm.at[pl.ds(src, pr)], xbuf.at[buf, pl.ds(dst, pr)], sem.at[buf]).start()
            return carry
        lax.fori_loop(0, tme, body, 0, unroll=8)

    for j in range(GATHER_AHEAD):
        pl.when((i == 0) & (j < n_used))(functools.partial(start_gather, j, j))

    @pl.when(i + GATHER_AHEAD < n_used)
    def _():
        start_gather(i + GATHER_AHEAD, lax.rem(i + GATHER_AHEAD, nbuf))

    te = te_ref[i]
    prev = te_ref[jnp.maximum(i - 1, 0)]

    @pl.when((i == 0) | (te != prev))
    def _():
        wgb[...] = wg_ref[...].astype(BF16)
        wub[...] = wu_ref[...].astype(BF16)
        wdb[...] = wd_ref[...].astype(BF16)

    @pl.when(i < n_used)
    def _():
        pltpu.make_async_copy(x_hbm.at[pl.ds(0, tme * pr)], xbuf.at[slot], sem.at[slot]).wait()
        xb = _load_packed(xbuf.at[slot], tme).astype(BF16)
        gp = jnp.dot(xb, wgb[...], preferred_element_type=F32)
        up = jnp.dot(xb, wub[...], preferred_element_type=F32)
        hidden = (gp * jax.nn.sigmoid(gp)) * up
        _store_packed(y_ref, jnp.dot(hidden.astype(BF16), wdb[...], preferred_element_type=F32))

    @pl.when(i >= n_used)
    def _():
        y_ref[...] = jnp.zeros(y_ref.shape, y_ref.dtype)


def _moe_experts(tile_expert, n_used, row_token, h_packed, w_gate, w_up, w_down, layer):
    n_tiles = tile_expert.shape[0]
    rows = EXPERT_TILE * PACK_ROWS
    wspec = lambda a, b: pl.BlockSpec((None, None, a, b), lambda i, te, nu, rt: (layer, te[i], 0, 0))
    return pl.pallas_call(
        _moe_expert_kernel,
        out_shape=jax.ShapeDtypeStruct((n_tiles * rows, LANES), jnp.uint32),
        grid_spec=pltpu.PrefetchScalarGridSpec(
            num_scalar_prefetch=3,
            grid=(n_tiles,),
            in_specs=[pl.BlockSpec(memory_space=pl.ANY),
                      wspec(D_MODEL, MOE_FF), wspec(D_MODEL, MOE_FF), wspec(MOE_FF, D_MODEL)],
            out_specs=pl.BlockSpec((rows, LANES), lambda i, te, nu, rt: (i, 0)),
            scratch_shapes=[pltpu.VMEM((GATHER_AHEAD + 1, rows, LANES), jnp.uint32),
                            pltpu.SemaphoreType.DMA((GATHER_AHEAD + 1,)),
                            pltpu.VMEM((D_MODEL, MOE_FF), BF16),
                            pltpu.VMEM((D_MODEL, MOE_FF), BF16),
                            pltpu.VMEM((MOE_FF, D_MODEL), BF16)]),
        compiler_params=_params("arbitrary"),
        name="moe_experts",
    )(tile_expert, n_used, row_token, h_packed, w_gate, w_up, w_down)


def _moe_combine_kernel(slot_ref, h_ref, route_ref, y_hbm, g_ref, b_ref, o_ref, ybuf, sem):
    i = pl.program_id(0)
    n = pl.num_programs(0)
    nbuf = GATHER_AHEAD + 1
    slot = lax.rem(i, nbuf)
    tm = h_ref.shape[0]
    pr = PACK_ROWS

    def start_gather(tile, buf):
        def body(r, carry):
            for k in range(2):
                src = pl.multiple_of(slot_ref[(tile * tm + r) * 2 + k] * pr, pr)
                dst = pl.multiple_of((k * tm + r) * pr, pr)
                pltpu.make_async_copy(y_hbm.at[pl.ds(src, pr)], ybuf.at[buf, pl.ds(dst, pr)], sem.at[buf]).start()
            return carry
        lax.fori_loop(0, tm, body, 0, unroll=4)

    for j in range(GATHER_AHEAD):
        pl.when((i == 0) & (j < n))(functools.partial(start_gather, j, j))

    @pl.when(i + GATHER_AHEAD < n)
    def _():
        start_gather(i + GATHER_AHEAD, lax.rem(i + GATHER_AHEAD, nbuf))

    pltpu.make_async_copy(y_hbm.at[pl.ds(0, 2 * tm * pr)], ybuf.at[slot], sem.at[slot]).wait()
    y1 = _load_packed(ybuf.at[slot, pl.ds(0, tm * pr)], tm)
    y2 = _load_packed(ybuf.at[slot, pl.ds(tm * pr, tm * pr)], tm)
    ffn = route_ref[:, 2:3] * y1 + route_ref[:, 3:4] * y2
    o_ref[...] = _layer_norm_rows(DN_ALPHA * h_ref[...] + ffn, g_ref[...], b_ref[...])


def _moe_combine(slots, h, route, y, ln_g, ln_b, *, tm):
    t = h.shape[0]
    row = lambda w: pl.BlockSpec((tm, w), lambda i, s: (i, 0))
    vec = pl.BlockSpec((1, D_MODEL), lambda i, s: (0, 0))
    return pl.pallas_call(
        _moe_combine_kernel,
        out_shape=jax.ShapeDtypeStruct((t, D_MODEL), F32),
        grid_spec=pltpu.PrefetchScalarGridSpec(
            num_scalar_prefetch=1,
            grid=(t // tm,),
            in_specs=[row(D_MODEL), row(LANES), pl.BlockSpec(memory_space=pl.ANY), vec, vec],
            out_specs=row(D_MODEL),
            scratch_shapes=[pltpu.VMEM((GATHER_AHEAD + 1, 2 * tm * PACK_ROWS, LANES), jnp.uint32),
                            pltpu.SemaphoreType.DMA((GATHER_AHEAD + 1,))]),
        compiler_params=_params("arbitrary"),
        name="moe_combine",
    )(slots, h, route, y, ln_g.reshape(1, D_MODEL), ln_b.reshape(1, D_MODEL))


def _moe_plan(route, counts, n_tiles):
    t = route.shape[0]
    tme = EXPERT_TILE
    cnt = counts[0, :MOE_EXPERTS].astype(jnp.int32)
    tiles = (cnt + tme - 1) // tme
    tile_end = jnp.cumsum(tiles)
    tile_start = tile_end - tiles
    n_used = tile_end[-1]
    choice = route[:, 0:2].astype(jnp.int32)
    rank = route[:, 4:6].astype(jnp.int32)
    expert_ids = jnp.arange(MOE_EXPERTS, dtype=jnp.int32)
    start = jnp.sum(jnp.where(choice[..., None] == expert_ids, tile_start, 0), axis=-1)
    slots = (start * tme + rank).reshape(-1)
    tile_ids = jnp.arange(n_tiles, dtype=jnp.int32)
    te = jnp.sum((tile_ids[:, None] >= tile_end[None, :]).astype(jnp.int32), axis=1)
    last_e = jnp.sum((n_used - 1 >= tile_end).astype(jnp.int32))
    te = jnp.where(tile_ids < n_used, te, last_e)
    row_token = jnp.zeros((n_tiles * tme,), jnp.int32).at[slots].set(jnp.arange(2 * t, dtype=jnp.int32) // 2)
    return te, n_used.reshape(1).astype(jnp.int32), row_token, slots.astype(jnp.int32)


def _hier_moe(h, h_packed, layer, w_group, b_group, w_expert, b_expert, w_gate, w_up, w_down, ln_g, ln_b):
    t = h.shape[0]
    pad = LANES - MOE_EXPERTS - MOE_GROUPS
    w_r = jnp.concatenate([w_expert, w_group, jnp.zeros((D_MODEL, pad), F32)], axis=1)
    w_hi = w_r.astype(BF16)
    w_r = jnp.stack([w_hi, (w_r - w_hi.astype(F32)).astype(BF16)])
    b_r = jnp.concatenate([b_expert, b_group, jnp.zeros((pad,), F32)]).reshape(1, LANES)
    route, counts = _router(h, w_r, b_r)
    n_tiles = (2 * t) // EXPERT_TILE + MOE_EXPERTS
    te, n_used, row_token, slots = _moe_plan(route, counts, n_tiles)
    y = _moe_experts(te, n_used, row_token, h_packed, w_gate, w_up, w_down, layer)
    return _moe_combine(slots, h, route, y, ln_g, ln_b, tm=COMBINE_TILE)


def kernel(x, rel_bias, s5_w_in, s5_a_re, s5_a_im, s5_log_dt, s5_b_re, s5_b_im, s5_c_re, s5_c_im, s5_d, s5_w_glu, s5_w_out, dsa_w_in, dsa_q_norm, dsa_kv_norm, dsa_w_uq, dsa_w_qidx, dsa_w_uk, dsa_w_uv, dsa_w_out, moe_w_group, moe_b_group, moe_w_expert, moe_b_expert, moe_w_gate, moe_w_up, moe_w_down, ln_mix_g, ln_mix_b, ln_ffn_g, ln_ffn_b):
    batch, seq, dm = x.shape
    h = x.reshape(batch * seq, dm)
    s5_w_in, s5_w_glu, s5_w_out = (w.astype(BF16) for w in (s5_w_in, s5_w_glu, s5_w_out))
    n_pad = Q_LORA + KV_LORA + IDX_DIM + LANES - dsa_w_in.shape[-1]
    dsa_w_in_pad = jnp.pad(dsa_w_in, ((0, 0), (0, 0), (0, n_pad))).astype(BF16)
    dsa_w_q = jnp.concatenate([dsa_w_uq, dsa_w_qidx], axis=-1).astype(BF16)
    dsa_w_uk, dsa_w_out = dsa_w_uk.astype(BF16), dsa_w_out.astype(BF16)
    dsa_w_uv_t = jnp.swapaxes(dsa_w_uv, -1, -2).astype(BF16)
    for i in range(DEPTH):
        j = i // 2
        if i % 2 == 0:
            h, hp = _s5_mixer(h, ln_mix_g[i], ln_mix_b[i], j, s5_w_in, s5_a_re[j], s5_a_im[j], s5_log_dt[j],
                              s5_b_re[j], s5_b_im[j], s5_c_re[j], s5_c_im[j], s5_d[j], s5_w_glu, s5_w_out,
                              batch=batch, seq=seq)
        else:
            h, hp = _dsa_mixer(h, ln_mix_g[i], ln_mix_b[i], j, rel_bias, dsa_w_in_pad, dsa_q_norm[j],
                               dsa_kv_norm[j], dsa_w_q, dsa_w_uk, dsa_w_uv_t, dsa_w_out, batch=batch, seq=seq)
        h = _hier_moe(h, hp, i, moe_w_group[i], moe_b_group[i], moe_w_expert[i], moe_b_expert[i],
                      moe_w_gate, moe_w_up, moe_w_down, ln_ffn_g[i], ln_ffn_b[i])
    return h.reshape(batch, seq, dm)
```

```python
import functools
import math

import numpy as np
import jax
import jax.numpy as jnp
from jax import lax
from jax.experimental import pallas as pl
from jax.experimental.pallas import tpu as pltpu

F32 = jnp.float32
BF16 = jnp.bfloat16

D_MODEL = 2048
DEPTH = 4
S5_GROUP = 16
S5_GROUPS = D_MODEL // S5_GROUP
S5_STATE = 64
S5_CHUNK = 64
ATT_HEADS = 16
ATT_HEAD_DIM = 128
Q_LORA = 512
KV_LORA = 512
IDX_HEADS = 16
IDX_DIM = 128
IDX_TOPK = 256
Q_BLOCK = 128
KEY_CHUNK = 1024
IDX_WIDTH_STEP = 128
REL_BUCKETS = 32
REL_MAX_DIST = 128
MOE_GROUPS = 4
MOE_PER_GROUP = 8
MOE_EXPERTS = 32
MOE_FF = 256
EXPERT_TILE = 256
COMBINE_TILE = 256
GATHER_AHEAD = 2
DN_ALPHA = (2 * DEPTH) ** 0.25
LN_EPS = 1e-5
RMS_EPS = 1e-6

LANES = 128
MASKED = -1e30
VMEM_LIMIT = 56 * 1024 * 1024


def _params(*sem):
    return pltpu.CompilerParams(dimension_semantics=sem, vmem_limit_bytes=VMEM_LIMIT)


def _layer_norm_rows(y, g, b):
    mean = jnp.mean(y, axis=-1, keepdims=True)
    yc = y - mean
    var = jnp.mean(yc * yc, axis=-1, keepdims=True)
    return yc * lax.rsqrt(var + LN_EPS) * g + b


def _mm_plain_kernel(x_ref, w_ref, o_ref, *, tn):
    xb = x_ref[...].astype(BF16)
    for n0 in range(0, o_ref.shape[1], tn):
        acc = jnp.dot(xb, w_ref[:, n0:n0 + tn], preferred_element_type=F32)
        o_ref[:, n0:n0 + tn] = acc.astype(o_ref.dtype)


def _mm_glu_kernel(z_ref, w_ref, o_ref, *, tn):
    zb = z_ref[...]
    for n0 in range(0, o_ref.shape[1], tn):
        acc = jnp.dot(zb, w_ref[:, n0:n0 + tn], preferred_element_type=F32)
        zc = z_ref[:, n0:n0 + tn].astype(F32)
        o_ref[:, n0:n0 + tn] = (zc * jax.nn.sigmoid(acc)).astype(o_ref.dtype)


PACK_ROWS = D_MODEL // (2 * LANES)


def _store_packed(o_ref, y):
    half = y.shape[1] // 2
    bits = lambda v: lax.bitcast_convert_type(v.astype(BF16).astype(F32), jnp.uint32)
    word = (bits(y[:, half:]) & jnp.uint32(0xFFFF0000)) | (bits(y[:, :half]) >> 16)
    rows = y.shape[0]
    for c in range(PACK_ROWS):
        o_ref[pl.ds(c, rows, stride=PACK_ROWS), :] = word[:, c * LANES:(c + 1) * LANES]


def _load_packed(x_ref, rows):
    word = jnp.concatenate([x_ref[pl.ds(c, rows, stride=PACK_ROWS), :] for c in range(PACK_ROWS)], axis=1)
    lo = lax.bitcast_convert_type(word << 16, F32)
    hi = lax.bitcast_convert_type(word & jnp.uint32(0xFFFF0000), F32)
    return jnp.concatenate([lo, hi], axis=1)


def _mm_ln_kernel(x_ref, w_ref, res_ref, g_ref, b_ref, o_ref, op_ref, *, tn):
    xb = x_ref[...].astype(BF16)
    for n0 in range(0, o_ref.shape[1], tn):
        acc = jnp.dot(xb, w_ref[:, n0:n0 + tn], preferred_element_type=F32)
        o_ref[:, n0:n0 + tn] = DN_ALPHA * res_ref[:, n0:n0 + tn] + acc
    y = _layer_norm_rows(o_ref[...], g_ref[...], b_ref[...])
    o_ref[...] = y
    _store_packed(op_ref, y)


def _layer_spec(w_all, layer):
    shape = w_all.shape[1:]
    return pl.BlockSpec((None,) + shape, lambda *_: (layer,) + (0,) * len(shape))


def _linear(x, w_all, layer, *, tm, out_dtype, tn=512):
    m, k = x.shape
    n = w_all.shape[-1]
    return pl.pallas_call(
        functools.partial(_mm_plain_kernel, tn=tn),
        out_shape=jax.ShapeDtypeStruct((m, n), out_dtype),
        grid=(m // tm,),
        in_specs=[pl.BlockSpec((tm, k), lambda i: (i, 0)), _layer_spec(w_all, layer)],
        out_specs=pl.BlockSpec((tm, n), lambda i: (i, 0)),
        compiler_params=_params("parallel"),
        name="linear",
    )(x, w_all)


def _linear_glu(z, w_all, layer, *, tm, tn=512):
    m, k = z.shape
    return pl.pallas_call(
        functools.partial(_mm_glu_kernel, tn=tn),
        out_shape=jax.ShapeDtypeStruct((m, k), BF16),
        grid=(m // tm,),
        in_specs=[pl.BlockSpec((tm, k), lambda i: (i, 0)), _layer_spec(w_all, layer)],
        out_specs=pl.BlockSpec((tm, k), lambda i: (i, 0)),
        compiler_params=_params("parallel"),
        name="linear_glu",
    )(z, w_all)


def _linear_ln(x, w_all, layer, res, g, b, *, tm, tn=512):
    m, k = x.shape
    n = w_all.shape[-1]
    return pl.pallas_call(
        functools.partial(_mm_ln_kernel, tn=tn),
        out_shape=(jax.ShapeDtypeStruct((m, n), F32), jax.ShapeDtypeStruct((m * PACK_ROWS, LANES), jnp.uint32)),
        grid=(m // tm,),
        in_specs=[pl.BlockSpec((tm, k), lambda i: (i, 0)),
                  _layer_spec(w_all, layer),
                  pl.BlockSpec((tm, n), lambda i: (i, 0)),
                  pl.BlockSpec((1, n), lambda i: (0, 0)),
                  pl.BlockSpec((1, n), lambda i: (0, 0))],
        out_specs=(pl.BlockSpec((tm, n), lambda i: (i, 0)),
                   pl.BlockSpec((tm * PACK_ROWS, LANES), lambda i: (i, 0))),
        compiler_params=_params("parallel"),
        name="linear_ln",
    )(x, w_all, res, g.reshape(1, n), b.reshape(1, n))


def _gelu_tanh(x):
    return 0.5 * x * (1.0 + jnp.tanh(math.sqrt(2.0 / math.pi) * (x + 0.044715 * x * x * x)))


def _s5_kernel(u_ref, ar_ref, ai_ref, ldt_ref, br_ref, bi_ref, cr_ref, ci_ref, d_ref, o_ref, a_scr,
               *, n_chunks):
    cs, p, n = S5_CHUNK, S5_GROUP, S5_STATE
    width = cs * p
    n2 = 2 * n

    @pl.when(pl.program_id(0) == 0)
    def _():
        a_scr[...] = jnp.zeros(a_scr.shape, a_scr.dtype)

    dt = jnp.exp(ldt_ref[0])
    lam_r = jnp.minimum(ar_ref[0], -1e-4)
    lam_i = ai_ref[0]
    lo_half = lax.broadcasted_iota(jnp.int32, (1, n2), 1) < n

    def a_pow(tau):
        mag = jnp.exp(lam_r * dt * tau)
        ph = lam_i * dt * tau
        return mag * jnp.cos(ph), mag * jnp.sin(ph)

    tau = lax.broadcasted_iota(jnp.int32, (cs, 1), 0).astype(F32)
    a0r, a0i = a_pow(tau)
    a1r, a1i = a_pow(tau + 1.0)
    avr, avi = a_pow((cs - 1.0) - tau)

    abr, abi = a1r[0:1], a1i[0:1]
    den = lam_r * lam_r + lam_i * lam_i
    kr = ((abr - 1.0) * lam_r + abi * lam_i) / den
    ki = (abi * lam_r - (abr - 1.0) * lam_i) / den
    b_r, b_i = br_ref[0], bi_ref[0]
    bbr = kr * b_r - ki * b_i
    bbi = kr * b_i + ki * b_r
    c_r, c_i = cr_ref[0], ci_ref[0]

    def rep_rows(x):
        return jnp.concatenate([jnp.broadcast_to(x[t:t + 1, :], (p, n2)) for t in range(cs)], axis=0)

    def tile_rows(x):
        return jnp.concatenate([x] * cs, axis=0)

    def sel(lo, hi):
        return jnp.where(lo_half, lo, hi)

    crr, cii = tile_rows(c_r), tile_rows(c_i)
    brr, bii = tile_rows(bbr), tile_rows(bbi)
    c_taps = crr * rep_rows(sel(a0r, -a0i)) - cii * rep_rows(sel(a0i, a0r))
    c_out = crr * rep_rows(sel(a1r, -a1i)) - cii * rep_rows(sel(a1i, a1r))
    b_in = brr * rep_rows(sel(avr, avi)) + bii * rep_rows(sel(-avi, avr))
    b_in_sw = brr * rep_rows(sel(avi, avr)) + bii * rep_rows(sel(avr, -avi))

    b_mat = sel(bbr, bbi)
    taps = lax.dot_general(b_mat, c_taps, (((1,), (1,)), ((), ())),
                           precision=lax.Precision.HIGHEST, preferred_element_type=F32)

    lane = lax.broadcasted_iota(jnp.int32, (p, width), 1)
    band = [taps]
    for j in range(1, LANES // p):
        band.append(jnp.where(lane >= p * j, pltpu.roll(taps, p * j, axis=1), 0.0))
    band = jnp.concatenate(band, axis=0)
    n_blk = width // LANES
    band_t = jnp.concatenate([band[:, d * LANES:(d + 1) * LANES].T for d in reversed(range(n_blk))],
                             axis=1).astype(BF16)
    for q in range(n_blk):
        a_scr[q * LANES:(q + 1) * LANES, 0:(q + 1) * LANES] = band_t[:, (n_blk - 1 - q) * LANES:width]
    a_scr[width:width + n2, :] = b_in.T.astype(BF16)
    a_scr[width + n2:width + 2 * n2, :] = b_in_sw.T.astype(BF16)

    u = u_ref[...].reshape(width, u_ref.shape[2])
    res = jnp.dot(a_scr[...], u, preferred_element_type=F32)
    y = res[0:width]
    x_in = res[width:width + n2]
    x_sw = res[width + n2:width + 2 * n2]

    n_lanes = x_in.shape[1]
    levels = n_chunks.bit_length() - 1
    mults = []
    for k in range(levels):
        mr, mi = a_pow(jnp.full((1, 1), float(cs * 2 ** k), F32))
        mults += [mr, sel(-mi, mi), sel(mi, -mi)]
    mults = jnp.concatenate(mults + [jnp.zeros((n2 - len(mults), n2), F32)], axis=0)
    mcol = mults.T
    chunk_id = lax.rem(lax.broadcasted_iota(jnp.int32, (n2, n_lanes), 1), n_chunks)
    for k in range(levels):
        sh = 2 ** k
        ok = chunk_id >= sh
        xs = jnp.where(ok, pltpu.roll(x_in, sh, axis=1), 0.0)
        xs_sw = jnp.where(ok, pltpu.roll(x_sw, sh, axis=1), 0.0)
        m_rr, m_ni, m_pi = (mcol[:, 3 * k + j:3 * k + j + 1] for j in range(3))
        x_in, x_sw = (x_in + xs * m_rr + xs_sw * m_ni, x_sw + xs_sw * m_rr + xs * m_pi)
    h_start = jnp.where(chunk_id >= 1, pltpu.roll(x_in, 1, axis=1), 0.0).astype(BF16)

    y = y + jnp.dot(c_out.astype(BF16), h_start, preferred_element_type=F32)
    y3 = y.reshape(cs, p, n_lanes) + d_ref[0] * u_ref[...].astype(F32)
    o_ref[...] = _gelu_tanh(y3).astype(o_ref.dtype)


def _s5_core(u_t, a_re, a_im, log_dt, b_re, b_im, c_re, c_im, d, *, n_chunks):
    cs, dm, n_lanes = u_t.shape
    g = dm // S5_GROUP
    assert n_chunks & (n_chunks - 1) == 0
    dup = lambda x: jnp.concatenate([x, x], axis=-1)
    a_re2 = dup(a_re).reshape(g, 1, 2 * S5_STATE)
    a_im2 = dup(a_im).reshape(g, 1, 2 * S5_STATE)
    ldt = log_dt.reshape(g, 1, 1)
    bt_re2 = dup(jnp.swapaxes(b_re, 1, 2))
    bt_im2 = dup(jnp.swapaxes(b_im, 1, 2))
    c_re2 = dup(c_re)
    c_im2 = dup(c_im)
    d_c = d.reshape(g, S5_GROUP, 1)
    width = cs * S5_GROUP
    vec = lambda w: pl.BlockSpec((1, 1, w), lambda i: (i, 0, 0))
    mat = pl.BlockSpec((1, S5_GROUP, 2 * S5_STATE), lambda i: (i, 0, 0))
    blk = pl.BlockSpec((cs, S5_GROUP, n_lanes), lambda i: (0, i, 0))
    return pl.pallas_call(
        functools.partial(_s5_kernel, n_chunks=n_chunks),
        out_shape=jax.ShapeDtypeStruct((cs, dm, n_lanes), BF16),
        grid=(g,),
        in_specs=[blk, vec(2 * S5_STATE), vec(2 * S5_STATE), vec(1), mat, mat, mat, mat,
                  pl.BlockSpec((1, S5_GROUP, 1), lambda i: (i, 0, 0))],
        out_specs=blk,
        scratch_shapes=[pltpu.VMEM((width + 4 * S5_STATE, width), BF16)],
        compiler_params=_params("arbitrary"),
        name="s5_core",
    )(u_t, a_re2, a_im2, ldt, bt_re2, bt_im2, c_re2, c_im2, d_c)


S5_STEP = 8
LANE_ROWS = 128


def _s5_in_kernel(x_ref, w_ref, o_ref, u_scr):
    rows = LANE_ROWS * S5_STEP
    dm = x_ref.shape[2]
    xb = x_ref[...].reshape(rows, dm).astype(BF16)
    for n0 in range(0, dm, 512):
        acc = jnp.dot(xb, w_ref[:, n0:n0 + 512], preferred_element_type=F32)
        for c in range(512 // LANES):
            u_scr[n0 // LANES + c] = acc[:, c * LANES:(c + 1) * LANES]
    for j in range(S5_STEP):
        for cb in range(dm // LANES):
            tile = u_scr[cb, pl.ds(j, LANE_ROWS, stride=S5_STEP), :]
            o_ref[j, cb * LANES:(cb + 1) * LANES, :] = tile.T.astype(o_ref.dtype)


def _s5_in_proj(h3, w_all, layer):
    n_bc, cs, dm = h3.shape
    return pl.pallas_call(
        _s5_in_kernel,
        out_shape=jax.ShapeDtypeStruct((cs, dm, n_bc), BF16),
        grid=(cs // S5_STEP, n_bc // LANE_ROWS),
        in_specs=[pl.BlockSpec((LANE_ROWS, S5_STEP, dm), lambda s, r: (r, s, 0)), _layer_spec(w_all, layer)],
        out_specs=pl.BlockSpec((S5_STEP, dm, LANE_ROWS), lambda s, r: (s, 0, r)),
        scratch_shapes=[pltpu.VMEM((dm // LANES, LANE_ROWS * S5_STEP, LANES), F32)],
        compiler_params=_params("parallel", "parallel"),
        name="s5_in_proj",
    )(h3, w_all)


def _s5_glu_kernel(z_ref, w_ref, o_ref, z_scr, o_scr):
    nh = pl.program_id(2)
    half = o_ref.shape[2]

    @pl.when(nh == 0)
    def _():
        for j in range(S5_STEP):
            z_scr[j * LANE_ROWS:(j + 1) * LANE_ROWS, :] = z_ref[j].astype(F32).T

    zb = z_scr[...].astype(BF16)
    col0 = pl.multiple_of(nh * half, half)
    for n0 in range(0, half, 512):
        acc = jnp.dot(zb, w_ref[:, n0:n0 + 512], preferred_element_type=F32)
        zg = z_scr[:, pl.ds(col0 + n0, 512)] * jax.nn.sigmoid(acc)
        for j in range(S5_STEP):
            for c in range(512 // LANES):
                o_scr[n0 // LANES + c, pl.ds(j, LANE_ROWS, stride=S5_STEP), :] = (
                    zg[j * LANE_ROWS:(j + 1) * LANE_ROWS, c * LANES:(c + 1) * LANES])
    out = jnp.concatenate([o_scr[cb] for cb in range(half // LANES)], axis=1)
    o_ref[...] = out.reshape(o_ref.shape)


def _s5_glu(z_t, w_all, layer):
    cs, dm, n_bc = z_t.shape
    rows = LANE_ROWS * S5_STEP
    half = dm // 2
    return pl.pallas_call(
        _s5_glu_kernel,
        out_shape=jax.ShapeDtypeStruct((n_bc, cs, dm), F32),
        grid=(cs // S5_STEP, n_bc // LANE_ROWS, 2),
        in_specs=[pl.BlockSpec((S5_STEP, dm, LANE_ROWS), lambda s, r, nh: (s, 0, r)),
                  pl.BlockSpec((None, dm, half), lambda s, r, nh: (layer, 0, nh))],
        out_specs=pl.BlockSpec((LANE_ROWS, S5_STEP, half), lambda s, r, nh: (r, s, nh)),
        scratch_shapes=[pltpu.VMEM((rows, dm), F32), pltpu.VMEM((half // LANES, rows, LANES), F32)],
        compiler_params=_params("parallel", "parallel", "arbitrary"),
        name="s5_glu",
    )(z_t, w_all)


def _s5_mixer(h, res_g, res_b, layer, w_in, a_re, a_im, log_dt, b_re, b_im, c_re, c_im, d, w_glu, w_out,
              *, batch, seq):
    t = batch * seq
    nc = seq // S5_CHUNK
    u_t = _s5_in_proj(h.reshape(batch * nc, S5_CHUNK, D_MODEL), w_in, layer)
    z_t = _s5_core(u_t, a_re, a_im, log_dt, b_re, b_im, c_re, c_im, d, n_chunks=nc)
    zg = _s5_glu(z_t, w_glu, layer).reshape(t, D_MODEL)
    return _linear_ln(zg, w_out, layer, h, res_g, res_b, tm=256)


def _dsa_proj_kernel(x_ref, w_ref, qn_ref, kn_ref, cq_ref, ckv_ref, ckvt_ref, kidx_ref, widxt_ref):
    xb = x_ref[...].astype(BF16)

    def rms(v, g):
        return v * lax.rsqrt(jnp.mean(v * v, axis=-1, keepdims=True) + RMS_EPS) * g

    o1 = Q_LORA + KV_LORA
    cq = jnp.dot(xb, w_ref[:, 0:Q_LORA], preferred_element_type=F32)
    cq_ref[...] = rms(cq, qn_ref[...]).astype(BF16)
    ckv = rms(jnp.dot(xb, w_ref[:, Q_LORA:o1], preferred_element_type=F32), kn_ref[...])
    ckv_ref[...] = ckv.astype(BF16)
    ckvt_ref[...] = ckv.T.astype(BF16)
    kidx_ref[...] = jnp.dot(xb, w_ref[:, o1:o1 + IDX_DIM], preferred_element_type=F32).astype(BF16)
    widx = jnp.dot(xb, w_ref[:, o1 + IDX_DIM:o1 + IDX_DIM + LANES], preferred_element_type=F32)
    widxt_ref[...] = (widx * (IDX_HEADS ** -0.5)).T[0:IDX_HEADS, :]


def _dsa_proj(h, w_in_pad, layer, q_norm, kv_norm, *, batch, seq, tm=512):
    t = h.shape[0]
    per_b = seq // tm
    row = lambda w: pl.BlockSpec((tm, w), lambda i: (i, 0))
    return pl.pallas_call(
        _dsa_proj_kernel,
        out_shape=(jax.ShapeDtypeStruct((t, Q_LORA), BF16), jax.ShapeDtypeStruct((t, KV_LORA), BF16),
                   jax.ShapeDtypeStruct((batch, KV_LORA, seq), BF16),
                   jax.ShapeDtypeStruct((t, IDX_DIM), BF16), jax.ShapeDtypeStruct((IDX_HEADS, t), F32)),
        grid=(t // tm,),
        in_specs=[row(D_MODEL), _layer_spec(w_in_pad, layer),
                  pl.BlockSpec((1, Q_LORA), lambda i: (0, 0)), pl.BlockSpec((1, KV_LORA), lambda i: (0, 0))],
        out_specs=(row(Q_LORA), row(KV_LORA),
                   pl.BlockSpec((None, KV_LORA, tm), lambda i: (i // per_b, 0, i % per_b)),
                   row(IDX_DIM), pl.BlockSpec((IDX_HEADS, tm), lambda i: (0, i))),
        compiler_params=_params("parallel"),
        name="dsa_proj",
    )(h, w_in_pad, q_norm.reshape(1, Q_LORA), kv_norm.reshape(1, KV_LORA))


def _reduce_rows(op, x):
    rows, lanes = x.shape
    part = op(x.reshape(rows // 64, 8, 8, lanes), axis=0)
    return op(op(part, axis=0), axis=0, keepdims=True)


def _indexer_kernel(q_ref, k_ref, w_ref, s_ref, thr_ref, key_scr, *, seq):
    jb = pl.program_id(1)
    n_sel = float(min(IDX_TOPK, seq // 4))
    int_min = jnp.int32(-2 ** 31)
    flip = jnp.int32(0x7FFFFFFF)

    def run(width):
        k = k_ref[0:width, :]
        wrow = w_ref[...] * (IDX_DIM ** -0.5)
        acc = jnp.zeros((width, Q_BLOCK), F32)
        for g in range(IDX_HEADS // 2):
            qg = jnp.concatenate([q_ref[:, h * IDX_DIM:(h + 1) * IDX_DIM] for h in (2 * g, 2 * g + 1)], axis=0)
            s = lax.dot_general(k, qg, (((1,), (1,)), ((), ())), preferred_element_type=F32)
            acc = (acc + jnp.maximum(s[:, 0:Q_BLOCK], 0.0) * wrow[2 * g:2 * g + 1, :]
                   + jnp.maximum(s[:, Q_BLOCK:2 * Q_BLOCK], 0.0) * wrow[2 * g + 1:2 * g + 2, :])
        k_pos = lax.broadcasted_iota(jnp.int32, (width, Q_BLOCK), 0)
        q_pos = jb * Q_BLOCK + lax.broadcasted_iota(jnp.int32, (width, Q_BLOCK), 1)
        acc = jnp.where(k_pos <= q_pos, acc, -jnp.inf)
        s_ref[0:width, :] = acc
        if width < seq:
            s_ref[width:seq, :] = jnp.full((seq - width, Q_BLOCK), -jnp.inf, F32)

        bits = lax.bitcast_convert_type(acc, jnp.int32)
        key_scr[0:width, :] = jnp.where(bits < 0, bits ^ flip, bits)

        def count_ge(cand):
            return _reduce_rows(jnp.sum, (key_scr[0:width, :] >= cand).astype(F32))

        lo = jnp.where(count_ge(jnp.zeros((1, Q_BLOCK), jnp.int32)) >= n_sel, jnp.int32(0), int_min)

        def body(i, lo):
            cand = lo + jnp.left_shift(jnp.int32(1), jnp.int32(30) - i)
            return jnp.where(count_ge(cand) >= n_sel, cand, lo)

        lo = lax.fori_loop(0, 31, body, lo)
        thr = lax.bitcast_convert_type(jnp.where(lo < 0, lo ^ flip, lo), F32)
        thr_ref[...] = jnp.broadcast_to(thr, thr_ref.shape)

    cls_of_block = jb // (IDX_WIDTH_STEP // Q_BLOCK)
    min_cls = -(-int(n_sel) // IDX_WIDTH_STEP) - 1
    for cls in range(min_cls, seq // IDX_WIDTH_STEP):
        here = (cls_of_block <= cls) if cls == min_cls else (cls_of_block == cls)
        pl.when(here)(functools.partial(run, IDX_WIDTH_STEP * (cls + 1)))


def _indexer(qcat, kidx, widx_t, *, batch, seq):
    n_blk = seq // Q_BLOCK
    return pl.pallas_call(
        functools.partial(_indexer_kernel, seq=seq),
        out_shape=(jax.ShapeDtypeStruct((batch, n_blk, seq, Q_BLOCK), F32),
                   jax.ShapeDtypeStruct((batch, n_blk, 8, Q_BLOCK), F32)),
        grid=(batch, n_blk),
        in_specs=[pl.BlockSpec((Q_BLOCK, IDX_HEADS * IDX_DIM), lambda b, j: (b * n_blk + j, 1)),
                  pl.BlockSpec((seq, IDX_DIM), lambda b, j: (b, 0)),
                  pl.BlockSpec((IDX_HEADS, Q_BLOCK), lambda b, j: (0, b * n_blk + j))],
        out_specs=(pl.BlockSpec((None, None, seq, Q_BLOCK), lambda b, j: (b, j, 0, 0)),
                   pl.BlockSpec((None, None, 8, Q_BLOCK), lambda b, j: (b, j, 0, 0))),
        scratch_shapes=[pltpu.VMEM((seq, Q_BLOCK), jnp.int32)],
        compiler_params=_params("parallel", "parallel"),
        name="indexer",
    )(qcat, kidx, widx_t)


def _attn_kernel(jb_tbl, kc_tbl, q_ref, kv_ref, kvt_ref, s_ref, thr_ref, wuk_ref, wuvt_ref, bd_ref, bp_ref,
                 o_ref, ql_scr, m_scr, l_scr, acc_scr):
    jb = jb_tbl[pl.program_id(1)]
    kc = kc_tbl[pl.program_id(1)]
    last = (jb * Q_BLOCK) // KEY_CHUNK
    hd, qb = ATT_HEAD_DIM, Q_BLOCK
    sub = KEY_CHUNK // qb
    pair = 2 * qb

    @pl.when(kc == 0)
    def _():
        for h in range(ATT_HEADS):
            ql = jnp.dot(q_ref[:, h * hd:(h + 1) * hd], wuk_ref[h], preferred_element_type=F32)
            ql_scr[h * qb:(h + 1) * qb, :] = (ql * (hd ** -0.5)).astype(BF16)
        m_scr[...] = jnp.full(m_scr.shape, MASKED, F32)
        l_scr[...] = jnp.zeros(l_scr.shape, F32)
        acc_scr[...] = jnp.zeros(acc_scr.shape, F32)

    def attend(biases):
        width = len(biases) * qb
        kv = kv_ref[0:width, :]
        kvt = kvt_ref[:, 0:width]
        k_pos = kc * KEY_CHUNK + lax.broadcasted_iota(jnp.int32, (width, qb), 0)
        q_pos = jb * qb + lax.broadcasted_iota(jnp.int32, (width, qb), 1)
        keep = (s_ref[0:width, :] >= thr_ref[0:1, :]) & (k_pos <= q_pos)
        mask_add = jnp.where(keep, 0.0, MASKED)
        mask_add = jnp.concatenate([mask_add, mask_add], axis=1)
        for g in range(ATT_HEADS // 2):
            s = lax.dot_general(kv, ql_scr[g * pair:(g + 1) * pair, :], (((1,), (1,)), ((), ())),
                                preferred_element_type=F32)
            lg = s + mask_add
            if any(b is not None for b in biases):
                parts = []
                for c, b in enumerate(biases):
                    part = lg[c * qb:(c + 1) * qb]
                    if b is not None:
                        part = part + jnp.concatenate([b[2 * g], b[2 * g + 1]], axis=1)
                    parts.append(part)
                lg = jnp.concatenate(parts, axis=0)
            m_prev = m_scr[g, 0:1, :]
            m_new = jnp.maximum(m_prev, _reduce_rows(jnp.max, lg))
            corr = jnp.exp(m_prev - m_new)
            pr = jnp.exp(lg - m_new)
            l_scr[g] = jnp.broadcast_to(corr * l_scr[g, 0:1, :] + _reduce_rows(jnp.sum, pr), (8, pair))
            m_scr[g] = jnp.broadcast_to(m_new, (8, pair))
            pv = jnp.dot(kvt, pr.astype(BF16), preferred_element_type=F32)
            acc_scr[g] = acc_scr[g] * corr + pv

    r = jb - last * sub
    for nb in range(1, sub + 1):
        biases = (None,) * (nb - 2) + ((bp_ref,) if nb >= 2 else ()) + (bd_ref,)
        pl.when((kc == last) & (r == nb - 1))(functools.partial(attend, biases))
    prev_only = (kc == last - 1) & (r == 0)
    pl.when(prev_only)(functools.partial(attend, (None,) * (sub - 1) + (bp_ref,)))
    pl.when((kc < last) & jnp.logical_not(prev_only))(functools.partial(attend, (None,) * sub))

    @pl.when(kc == last)
    def _():
        for h in range(ATT_HEADS):
            g, half = h // 2, (h % 2) * qb
            ol = (acc_scr[g, :, half:half + qb] / l_scr[g, 0:1, half:half + qb]).astype(BF16)
            oh = jnp.dot(wuvt_ref[h], ol, preferred_element_type=F32)
            o_ref[:, h * hd:(h + 1) * hd] = oh.T.astype(o_ref.dtype)


def _rel_bucket_table():
    n = np.arange(2 * Q_BLOCK)
    max_exact = REL_BUCKETS // 2
    nf = np.maximum(n, 1).astype(np.float32)
    large = max_exact + (np.log(nf / np.float32(max_exact)) / np.float32(math.log(REL_MAX_DIST / max_exact))
                         * np.float32(REL_BUCKETS - max_exact)).astype(np.int32)
    large = np.minimum(large, REL_BUCKETS - 1)
    return np.where(n < max_exact, n, large)


def _attention(qcat, ckv, ckv_t, scores, thr, w_uk, w_uv_t, layer, rel_bias, *, batch, seq):
    t = batch * seq
    n_blk = seq // Q_BLOCK
    n_kc = seq // KEY_CHUNK
    bucket = _rel_bucket_table()
    assert bucket[Q_BLOCK + 1:].min() == REL_BUCKETS - 1
    kk = np.arange(Q_BLOCK)[:, None]
    qq = np.arange(Q_BLOCK)[None, :]
    rb = rel_bias.astype(F32)
    rb = rb - rb[REL_BUCKETS - 1]
    lookup = lambda dist: jnp.einsum("bh,kqb->hkq", rb, np.eye(REL_BUCKETS, dtype=np.float32)[bucket[dist]],
                                     precision=lax.Precision.HIGHEST)
    diag = jnp.where(jnp.asarray(kk <= qq)[None], lookup(np.maximum(qq - kk, 0)), 0.0)
    prev = lookup(Q_BLOCK + qq - kk)
    n_pair = ATT_HEADS // 2
    pair = 2 * Q_BLOCK

    steps = [(jj, kc) for jj in range(n_blk) for kc in range((jj * Q_BLOCK) // KEY_CHUNK + 1)]
    jb_tbl = jnp.asarray([s[0] for s in steps], jnp.int32)
    kc_tbl = jnp.asarray([s[1] for s in steps], jnp.int32)

    qrow = lambda b, s, jt, kt: (b * n_blk + jt[s], 0)
    full3 = lambda b, s, jt, kt: (0, 0, 0)
    return pl.pallas_call(
        _attn_kernel,
        out_shape=jax.ShapeDtypeStruct((t, ATT_HEADS * ATT_HEAD_DIM), BF16),
        grid_spec=pltpu.PrefetchScalarGridSpec(
            num_scalar_prefetch=2,
            grid=(batch, len(steps)),
            in_specs=[pl.BlockSpec((Q_BLOCK, ATT_HEADS * ATT_HEAD_DIM), qrow),
                      pl.BlockSpec((KEY_CHUNK, KV_LORA), lambda b, s, jt, kt: (b * n_kc + kt[s], 0)),
                      pl.BlockSpec((None, KV_LORA, KEY_CHUNK), lambda b, s, jt, kt: (b, 0, kt[s])),
                      pl.BlockSpec((None, None, KEY_CHUNK, Q_BLOCK), lambda b, s, jt, kt: (b, jt[s], kt[s], 0)),
                      pl.BlockSpec((None, None, 8, Q_BLOCK), lambda b, s, jt, kt: (b, jt[s], 0, 0)),
                      _layer_spec(w_uk, layer),
                      _layer_spec(w_uv_t, layer),
                      pl.BlockSpec((ATT_HEADS, Q_BLOCK, Q_BLOCK), full3),
                      pl.BlockSpec((ATT_HEADS, Q_BLOCK, Q_BLOCK), full3)],
            out_specs=pl.BlockSpec((Q_BLOCK, ATT_HEADS * ATT_HEAD_DIM), qrow),
            scratch_shapes=[pltpu.VMEM((ATT_HEADS * Q_BLOCK, KV_LORA), BF16),
                            pltpu.VMEM((n_pair, 8, pair), F32),
                            pltpu.VMEM((n_pair, 8, pair), F32),
                            pltpu.VMEM((n_pair, KV_LORA, pair), F32)]),
        compiler_params=_params("parallel", "arbitrary"),
        name="latent_attention",
    )(jb_tbl, kc_tbl, qcat, ckv, ckv_t, scores, thr, w_uk, w_uv_t, diag, prev)


def _dsa_mixer(h, res_g, res_b, layer, rel_bias, w_in_pad, q_norm, kv_norm, w_q, w_uk, w_uv_t, w_out,
               *, batch, seq):
    cq, ckv, ckv_t, kidx, widx_t = _dsa_proj(h, w_in_pad, layer, q_norm, kv_norm, batch=batch, seq=seq)
    qcat = _linear(cq, w_q, layer, tm=512, out_dtype=BF16)
    scores, thr = _indexer(qcat, kidx, widx_t, batch=batch, seq=seq)
    o = _attention(qcat, ckv, ckv_t, scores, thr, w_uk, w_uv_t, layer, rel_bias, batch=batch, seq=seq)
    return _linear_ln(o, w_out, layer, h, res_g, res_b, tm=256)


def _router_kernel(x_ref, w_ref, b_ref, route_ref, cnt_ref, cnt_scr):
    x = x_ref[...]
    x_hi = x.astype(BF16)
    x_lo = (x - x_hi.astype(F32)).astype(BF16)
    logits = (jnp.dot(x_hi, w_ref[0], preferred_element_type=F32)
              + jnp.dot(x_lo, w_ref[0], preferred_element_type=F32)
              + jnp.dot(x_hi, w_ref[1], preferred_element_type=F32)) + b_ref[...]
    lane = lax.broadcasted_iota(jnp.int32, logits.shape, 1)
    big = jnp.int32(LANES)
    is_group = (lane >= MOE_EXPERTS) & (lane < MOE_EXPERTS + MOE_GROUPS)
    gl = jnp.where(is_group, logits, -jnp.inf)
    gm = jnp.max(gl, axis=1, keepdims=True)
    g_p = 1.0 / jnp.sum(jnp.exp(gl - gm), axis=1, keepdims=True)
    g_idx = jnp.min(jnp.where(gl == gm, lane, big), axis=1, keepdims=True) - MOE_EXPERTS
    in_group = (lane < MOE_EXPERTS) & (jnp.right_shift(lane, 3) == g_idx)
    el = jnp.where(in_group, logits, -jnp.inf)
    em = jnp.max(el, axis=1, keepdims=True)
    ee = jnp.exp(el - em)
    prob = ee / jnp.sum(ee, axis=1, keepdims=True)
    p1 = jnp.max(prob, axis=1, keepdims=True)
    i1 = jnp.min(jnp.where(in_group & (prob == p1), lane, big), axis=1, keepdims=True)
    rest = in_group & (lane != i1)
    p2 = jnp.max(jnp.where(rest, prob, -1.0), axis=1, keepdims=True)
    i2 = jnp.min(jnp.where(rest & (prob == p2), lane, big), axis=1, keepdims=True)
    den = p1 + p2
    g1 = g_p * (p1 / den)
    g2 = g_p * (p2 / den)

    @pl.when(pl.program_id(0) == 0)
    def _():
        cnt_scr[...] = jnp.zeros(cnt_scr.shape, F32)

    tm = logits.shape[0]
    oh1 = (lane == i1).astype(BF16)
    oh2 = (lane == i2).astype(BF16)
    r_i = lax.broadcasted_iota(jnp.int32, (tm, tm), 0)
    c_i = lax.broadcasted_iota(jnp.int32, (tm, tm), 1)
    tri = (c_i < r_i).astype(BF16)
    pre1 = jnp.dot(tri, oh1, preferred_element_type=F32)
    pre2 = jnp.dot(tri, oh2, preferred_element_type=F32)
    tot1 = jnp.sum(oh1.astype(F32), axis=0, keepdims=True)
    tot2 = jnp.sum(oh2.astype(F32), axis=0, keepdims=True)
    base = cnt_scr[...]
    rank1 = jnp.sum(jnp.where(lane == i1, base + pre1, 0.0), axis=1, keepdims=True)
    rank2 = jnp.sum(jnp.where(lane == i2, base + tot1 + pre2, 0.0), axis=1, keepdims=True)
    cnt_scr[...] = base + tot1 + tot2
    cnt_ref[...] = jnp.broadcast_to(cnt_scr[...], cnt_ref.shape)

    out = jnp.where(lane == 0, i1.astype(F32), jnp.where(lane == 1, i2.astype(F32), 0.0))
    out = jnp.where(lane == 2, g1, jnp.where(lane == 3, g2, out))
    out = jnp.where(lane == 4, rank1, jnp.where(lane == 5, rank2, out))
    route_ref[...] = out


def _router(h, w_r, b_r, *, tm=512):
    t = h.shape[0]
    return pl.pallas_call(
        _router_kernel,
        out_shape=(jax.ShapeDtypeStruct((t, LANES), F32), jax.ShapeDtypeStruct((8, LANES), F32)),
        grid=(t // tm,),
        in_specs=[pl.BlockSpec((tm, D_MODEL), lambda i: (i, 0)),
                  pl.BlockSpec((2, D_MODEL, LANES), lambda i: (0, 0, 0)),
                  pl.BlockSpec((1, LANES), lambda i: (0, 0))],
        out_specs=(pl.BlockSpec((tm, LANES), lambda i: (i, 0)),
                   pl.BlockSpec((8, LANES), lambda i: (0, 0))),
        scratch_shapes=[pltpu.VMEM((1, LANES), F32)],
        compiler_params=_params("arbitrary"),
        name="moe_router",
    )(h, w_r, b_r)


def _moe_expert_kernel(te_ref, nu_ref, rt_ref, x_hbm, wg_ref, wu_ref, wd_ref, y_ref,
                       xbuf, sem, wgb, wub, wdb):
    i = pl.program_id(0)
    n_used = nu_ref[0]
    nbuf = GATHER_AHEAD + 1
    slot = lax.rem(i, nbuf)
    tme = EXPERT_TILE
    pr = PACK_ROWS

    def start_gather(tile, buf):
        def body(r, carry):
            src = pl.multiple_of(rt_ref[tile * tme + r] * pr, pr)
            dst = pl.multiple_of(r * pr, pr)
            pltpu.make_async_copy(x_hbm.at[pl.ds(src, pr)], xbuf.at[buf, pl.ds(dst, pr)], sem.at[buf]).start()
            return carry
        lax.fori_loop(0, tme, body, 0, unroll=8)

    for j in range(GATHER_AHEAD):
        pl.when((i == 0) & (j < n_used))(functools.partial(start_gather, j, j))

    @pl.when(i + GATHER_AHEAD < n_used)
    def _():
        start_gather(i + GATHER_AHEAD, lax.rem(i + GATHER_AHEAD, nbuf))

    te = te_ref[i]
    prev = te_ref[jnp.maximum(i - 1, 0)]

    @pl.when((i == 0) | (te != prev))
    def _():
        wgb[...] = wg_ref[...].astype(BF16)
        wub[...] = wu_ref[...].astype(BF16)
        wdb[...] = wd_ref[...].astype(BF16)

    @pl.when(i < n_used)
    def _():
        pltpu.make_async_copy(x_hbm.at[pl.ds(0, tme * pr)], xbuf.at[slot], sem.at[slot]).wait()
        xb = _load_packed(xbuf.at[slot], tme).astype(BF16)
        gp = jnp.dot(xb, wgb[...], preferred_element_type=F32)
        up = jnp.dot(xb, wub[...], preferred_element_type=F32)
        hidden = (gp * jax.nn.sigmoid(gp)) * up
        _store_packed(y_ref, jnp.dot(hidden.astype(BF16), wdb[...], preferred_element_type=F32))

    @pl.when(i >= n_used)
    def _():
        y_ref[...] = jnp.zeros(y_ref.shape, y_ref.dtype)


def _moe_experts(tile_expert, n_used, row_token, h_packed, w_gate, w_up, w_down, layer):
    n_tiles = tile_expert.shape[0]
    rows = EXPERT_TILE * PACK_ROWS
    wspec = lambda a, b: pl.BlockSpec((None, None, a, b), lambda i, te, nu, rt: (layer, te[i], 0, 0))
    return pl.pallas_call(
        _moe_expert_kernel,
        out_shape=jax.ShapeDtypeStruct((n_tiles * rows, LANES), jnp.uint32),
        grid_spec=pltpu.PrefetchScalarGridSpec(
            num_scalar_prefetch=3,
            grid=(n_tiles,),
            in_specs=[pl.BlockSpec(memory_space=pl.ANY),
                      wspec(D_MODEL, MOE_FF), wspec(D_MODEL, MOE_FF), wspec(MOE_FF, D_MODEL)],
            out_specs=pl.BlockSpec((rows, LANES), lambda i, te, nu, rt: (i, 0)),
            scratch_shapes=[pltpu.VMEM((GATHER_AHEAD + 1, rows, LANES), jnp.uint32),
                            pltpu.SemaphoreType.DMA((GATHER_AHEAD + 1,)),
                            pltpu.VMEM((D_MODEL, MOE_FF), BF16),
                            pltpu.VMEM((D_MODEL, MOE_FF), BF16),
                            pltpu.VMEM((MOE_FF, D_MODEL), BF16)]),
        compiler_params=_params("arbitrary"),
        name="moe_experts",
    )(tile_expert, n_used, row_token, h_packed, w_gate, w_up, w_down)


def _moe_combine_kernel(slot_ref, h_ref, route_ref, y_hbm, g_ref, b_ref, o_ref, ybuf, sem):
    i = pl.program_id(0)
    n = pl.num_programs(0)
    nbuf = GATHER_AHEAD + 1
    slot = lax.rem(i, nbuf)
    tm = h_ref.shape[0]
    pr = PACK_ROWS

    def start_gather(tile, buf):
        def body(r, carry):
            for k in range(2):
                src = pl.multiple_of(slot_ref[(tile * tm + r) * 2 + k] * pr, pr)
                dst = pl.multiple_of((k * tm + r) * pr, pr)
                pltpu.make_async_copy(y_hbm.at[pl.ds(src, pr)], ybuf.at[buf, pl.ds(dst, pr)], sem.at[buf]).start()
            return carry
        lax.fori_loop(0, tm, body, 0, unroll=4)

    for j in range(GATHER_AHEAD):
        pl.when((i == 0) & (j < n))(functools.partial(start_gather, j, j))

    @pl.when(i + GATHER_AHEAD < n)
    def _():
        start_gather(i + GATHER_AHEAD, lax.rem(i + GATHER_AHEAD, nbuf))

    pltpu.make_async_copy(y_hbm.at[pl.ds(0, 2 * tm * pr)], ybuf.at[slot], sem.at[slot]).wait()
    y1 = _load_packed(ybuf.at[slot, pl.ds(0, tm * pr)], tm)
    y2 = _load_packed(ybuf.at[slot, pl.ds(tm * pr, tm * pr)], tm)
    ffn = route_ref[:, 2:3] * y1 + route_ref[:, 3:4] * y2
    o_ref[...] = _layer_norm_rows(DN_ALPHA * h_ref[...] + ffn, g_ref[...], b_ref[...])


def _moe_combine(slots, h, route, y, ln_g, ln_b, *, tm):
    t = h.shape[0]
    row = lambda w: pl.BlockSpec((tm, w), lambda i, s: (i, 0))
    vec = pl.BlockSpec((1, D_MODEL), lambda i, s: (0, 0))
    return pl.pallas_call(
        _moe_combine_kernel,
        out_shape=jax.ShapeDtypeStruct((t, D_MODEL), F32),
        grid_spec=pltpu.PrefetchScalarGridSpec(
            num_scalar_prefetch=1,
            grid=(t // tm,),
            in_specs=[row(D_MODEL), row(LANES), pl.BlockSpec(memory_space=pl.ANY), vec, vec],
            out_specs=row(D_MODEL),
            scratch_shapes=[pltpu.VMEM((GATHER_AHEAD + 1, 2 * tm * PACK_ROWS, LANES), jnp.uint32),
                            pltpu.SemaphoreType.DMA((GATHER_AHEAD + 1,))]),
        compiler_params=_params("arbitrary"),
        name="moe_combine",
    )(slots, h, route, y, ln_g.reshape(1, D_MODEL), ln_b.reshape(1, D_MODEL))


def _moe_plan(route, counts, n_tiles):
    t = route.shape[0]
    tme = EXPERT_TILE
    cnt = counts[0, :MOE_EXPERTS].astype(jnp.int32)
    tiles = (cnt + tme - 1) // tme
    tile_end = jnp.cumsum(tiles)
    tile_start = tile_end - tiles
    n_used = tile_end[-1]
    choice = route[:, 0:2].astype(jnp.int32)
    rank = route[:, 4:6].astype(jnp.int32)
    expert_ids = jnp.arange(MOE_EXPERTS, dtype=jnp.int32)
    start = jnp.sum(jnp.where(choice[..., None] == expert_ids, tile_start, 0), axis=-1)
    slots = (start * tme + rank).reshape(-1)
    tile_ids = jnp.arange(n_tiles, dtype=jnp.int32)
    te = jnp.sum((tile_ids[:, None] >= tile_end[None, :]).astype(jnp.int32), axis=1)
    last_e = jnp.sum((n_used - 1 >= tile_end).astype(jnp.int32))
    te = jnp.where(tile_ids < n_used, te, last_e)
    row_token = jnp.zeros((n_tiles * tme,), jnp.int32).at[slots].set(jnp.arange(2 * t, dtype=jnp.int32) // 2)
    return te, n_used.reshape(1).astype(jnp.int32), row_token, slots.astype(jnp.int32)


def _hier_moe(h, h_packed, layer, w_group, b_group, w_expert, b_expert, w_gate, w_up, w_down, ln_g, ln_b):
    t = h.shape[0]
    pad = LANES - MOE_EXPERTS - MOE_GROUPS
    w_r = jnp.concatenate([w_expert, w_group, jnp.zeros((D_MODEL, pad), F32)], axis=1)
    w_hi = w_r.astype(BF16)
    w_r = jnp.stack([w_hi, (w_r - w_hi.astype(F32)).astype(BF16)])
    b_r = jnp.concatenate([b_expert, b_group, jnp.zeros((pad,), F32)]).reshape(1, LANES)
    route, counts = _router(h, w_r, b_r)
    n_tiles = (2 * t) // EXPERT_TILE + MOE_EXPERTS
    te, n_used, row_token, slots = _moe_plan(route, counts, n_tiles)
    y = _moe_experts(te, n_used, row_token, h_packed, w_gate, w_up, w_down, layer)
    return _moe_combine(slots, h, route, y, ln_g, ln_b, tm=COMBINE_TILE)


def kernel(x, rel_bias, s5_w_in, s5_a_re, s5_a_im, s5_log_dt, s5_b_re, s5_b_im, s5_c_re, s5_c_im, s5_d, s5_w_glu, s5_w_out, dsa_w_in, dsa_q_norm, dsa_kv_norm, dsa_w_uq, dsa_w_qidx, dsa_w_uk, dsa_w_uv, dsa_w_out, moe_w_group, moe_b_group, moe_w_expert, moe_b_expert, moe_w_gate, moe_w_up, moe_w_down, ln_mix_g, ln_mix_b, ln_ffn_g, ln_ffn_b):
    batch, seq, dm = x.shape
    h = x.reshape(batch * seq, dm)
    s5_w_in, s5_w_glu, s5_w_out = (w.astype(BF16) for w in (s5_w_in, s5_w_glu, s5_w_out))
    n_pad = Q_LORA + KV_LORA + IDX_DIM + LANES - dsa_w_in.shape[-1]
    dsa_w_in_pad = jnp.pad(dsa_w_in, ((0, 0), (0, 0), (0, n_pad))).astype(BF16)
    dsa_w_q = jnp.concatenate([dsa_w_uq, dsa_w_qidx], axis=-1).astype(BF16)
    dsa_w_uk, dsa_w_out = dsa_w_uk.astype(BF16), dsa_w_out.astype(BF16)
    dsa_w_uv_t = jnp.swapaxes(dsa_w_uv, -1, -2).astype(BF16)
    for i in range(DEPTH):
        j = i // 2
        if i % 2 == 0:
            h, hp = _s5_mixer(h, ln_mix_g[i], ln_mix_b[i], j, s5_w_in, s5_a_re[j], s5_a_im[j], s5_log_dt[j],
                              s5_b_re[j], s5_b_im[j], s5_c_re[j], s5_c_im[j], s5_d[j], s5_w_glu, s5_w_out,
                              batch=batch, seq=seq)
        else:
            h, hp = _dsa_mixer(h, ln_mix_g[i], ln_mix_b[i], j, rel_bias, dsa_w_in_pad, dsa_q_norm[j],
                               dsa_kv_norm[j], dsa_w_q, dsa_w_uk, dsa_w_uv_t, dsa_w_out, batch=batch, seq=seq)
        h = _hier_moe(h, hp, i, moe_w_group[i], moe_b_group[i], moe_w_expert[i], moe_b_expert[i],
                      moe_w_gate, moe_w_up, moe_w_down, ln_ffn_g[i], ln_ffn_b[i])
    return h.reshape(batch, seq, dm)
```

```python
import functools
import math

import numpy as np
import jax
import jax.numpy as jnp
from jax import lax
from jax.experimental import pallas as pl
from jax.experimental.pallas import tpu as pltpu

F32 = jnp.float32
BF16 = jnp.bfloat16

D_MODEL = 2048
DEPTH = 4
S5_GROUP = 16
S5_GROUPS = D_MODEL // S5_GROUP
S5_STATE = 64
S5_CHUNK = 64
ATT_HEADS = 16
ATT_HEAD_DIM = 128
Q_LORA = 512
KV_LORA = 512
IDX_HEADS = 16
IDX_DIM = 128
IDX_TOPK = 256
Q_BLOCK = 128
KEY_CHUNK = 512
IDX_WIDTH_STEP = 128
REL_BUCKETS = 32
REL_MAX_DIST = 128
MOE_GROUPS = 4
MOE_PER_GROUP = 8
MOE_EXPERTS = 32
MOE_FF = 256
EXPERT_TILE = 256
COMBINE_TILE = 256
GATHER_AHEAD = 2
DN_ALPHA = (2 * DEPTH) ** 0.25
LN_EPS = 1e-5
RMS_EPS = 1e-6

LANES = 128
MASKED = -1e30
VMEM_LIMIT = 56 * 1024 * 1024


def _params(*sem):
    return pltpu.CompilerParams(dimension_semantics=sem, vmem_limit_bytes=VMEM_LIMIT)


def _layer_norm_rows(y, g, b):
    mean = jnp.mean(y, axis=-1, keepdims=True)
    yc = y - mean
    var = jnp.mean(yc * yc, axis=-1, keepdims=True)
    return yc * lax.rsqrt(var + LN_EPS) * g + b


def _mm_plain_kernel(x_ref, w_ref, o_ref, *, tn):
    xb = x_ref[...].astype(BF16)
    for n0 in range(0, o_ref.shape[1], tn):
        acc = jnp.dot(xb, w_ref[:, n0:n0 + tn], preferred_element_type=F32)
        o_ref[:, n0:n0 + tn] = acc.astype(o_ref.dtype)


def _mm_glu_kernel(z_ref, w_ref, o_ref, *, tn):
    zb = z_ref[...]
    for n0 in range(0, o_ref.shape[1], tn):
        acc = jnp.dot(zb, w_ref[:, n0:n0 + tn], preferred_element_type=F32)
        zc = z_ref[:, n0:n0 + tn].astype(F32)
        o_ref[:, n0:n0 + tn] = (zc * jax.nn.sigmoid(acc)).astype(o_ref.dtype)


PACK_ROWS = D_MODEL // (2 * LANES)


def _store_packed(o_ref, y):
    half = y.shape[1] // 2
    bits = lambda v: lax.bitcast_convert_type(v.astype(BF16).astype(F32), jnp.uint32)
    word = (bits(y[:, half:]) & jnp.uint32(0xFFFF0000)) | (bits(y[:, :half]) >> 16)
    rows = y.shape[0]
    for c in range(PACK_ROWS):
        o_ref[pl.ds(c, rows, stride=PACK_ROWS), :] = word[:, c * LANES:(c + 1) * LANES]


def _load_packed(x_ref, rows):
    word = jnp.concatenate([x_ref[pl.ds(c, rows, stride=PACK_ROWS), :] for c in range(PACK_ROWS)], axis=1)
    lo = lax.bitcast_convert_type(word << 16, F32)
    hi = lax.bitcast_convert_type(word & jnp.uint32(0xFFFF0000), F32)
    return jnp.concatenate([lo, hi], axis=1)


def _mm_ln_kernel(x_ref, w_ref, res_ref, g_ref, b_ref, o_ref, op_ref, *, tn):
    xb = x_ref[...].astype(BF16)
    for n0 in range(0, o_ref.shape[1], tn):
        acc = jnp.dot(xb, w_ref[:, n0:n0 + tn], preferred_element_type=F32)
        o_ref[:, n0:n0 + tn] = DN_ALPHA * res_ref[:, n0:n0 + tn] + acc
    y = _layer_norm_rows(o_ref[...], g_ref[...], b_ref[...])
    o_ref[...] = y
    _store_packed(op_ref, y)


def _layer_spec(w_all, layer):
    shape = w_all.shape[1:]
    return pl.BlockSpec((None,) + shape, lambda *_: (layer,) + (0,) * len(shape))


def _linear(x, w_all, layer, *, tm, out_dtype, tn=512):
    m, k = x.shape
    n = w_all.shape[-1]
    return pl.pallas_call(
        functools.partial(_mm_plain_kernel, tn=tn),
        out_shape=jax.ShapeDtypeStruct((m, n), out_dtype),
        grid=(m // tm,),
        in_specs=[pl.BlockSpec((tm, k), lambda i: (i, 0)), _layer_spec(w_all, layer)],
        out_specs=pl.BlockSpec((tm, n), lambda i: (i, 0)),
        compiler_params=_params("parallel"),
        name="linear",
    )(x, w_all)


def _linear_glu(z, w_all, layer, *, tm, tn=512):
    m, k = z.shape
    return pl.pallas_call(
        functools.partial(_mm_glu_kernel, tn=tn),
        out_shape=jax.ShapeDtypeStruct((m, k), BF16),
        grid=(m // tm,),
        in_specs=[pl.BlockSpec((tm, k), lambda i: (i, 0)), _layer_spec(w_all, layer)],
        out_specs=pl.BlockSpec((tm, k), lambda i: (i, 0)),
        compiler_params=_params("parallel"),
        name="linear_glu",
    )(z, w_all)


def _linear_ln(x, w_all, layer, res, g, b, *, tm, tn=512):
    m, k = x.shape
    n = w_all.shape[-1]
    return pl.pallas_call(
        functools.partial(_mm_ln_kernel, tn=tn),
        out_shape=(jax.ShapeDtypeStruct((m, n), F32), jax.ShapeDtypeStruct((m * PACK_ROWS, LANES), jnp.uint32)),
        grid=(m // tm,),
        in_specs=[pl.BlockSpec((tm, k), lambda i: (i, 0)),
                  _layer_spec(w_all, layer),
                  pl.BlockSpec((tm, n), lambda i: (i, 0)),
                  pl.BlockSpec((1, n), lambda i: (0, 0)),
                  pl.BlockSpec((1, n), lambda i: (0, 0))],
        out_specs=(pl.BlockSpec((tm, n), lambda i: (i, 0)),
                   pl.BlockSpec((tm * PACK_ROWS, LANES), lambda i: (i, 0))),
        compiler_params=_params("parallel"),
        name="linear_ln",
    )(x, w_all, res, g.reshape(1, n), b.reshape(1, n))


def _gelu_tanh(x):
    return 0.5 * x * (1.0 + jnp.tanh(math.sqrt(2.0 / math.pi) * (x + 0.044715 * x * x * x)))


def _s5_kernel(u_ref, ar_ref, ai_ref, ldt_ref, br_ref, bi_ref, cr_ref, ci_ref, d_ref, o_ref, a_scr,
               *, n_chunks):
    cs, p, n = S5_CHUNK, S5_GROUP, S5_STATE
    width = cs * p
    n2 = 2 * n

    @pl.when(pl.program_id(0) == 0)
    def _():
        a_scr[...] = jnp.zeros(a_scr.shape, a_scr.dtype)

    dt = jnp.exp(ldt_ref[0])
    lam_r = jnp.minimum(ar_ref[0], -1e-4)
    lam_i = ai_ref[0]
    lo_half = lax.broadcasted_iota(jnp.int32, (1, n2), 1) < n

    def a_pow(tau):
        mag = jnp.exp(lam_r * dt * tau)
        ph = lam_i * dt * tau
        return mag * jnp.cos(ph), mag * jnp.sin(ph)

    tau = lax.broadcasted_iota(jnp.int32, (cs, 1), 0).astype(F32)
    a0r, a0i = a_pow(tau)
    a1r, a1i = a_pow(tau + 1.0)
    avr, avi = a_pow((cs - 1.0) - tau)

    abr, abi = a1r[0:1], a1i[0:1]
    den = lam_r * lam_r + lam_i * lam_i
    kr = ((abr - 1.0) * lam_r + abi * lam_i) / den
    ki = (abi * lam_r - (abr - 1.0) * lam_i) / den
    b_r, b_i = br_ref[0], bi_ref[0]
    bbr = kr * b_r - ki * b_i
    bbi = kr * b_i + ki * b_r
    c_r, c_i = cr_ref[0], ci_ref[0]

    def rep_rows(x):
        return jnp.concatenate([jnp.broadcast_to(x[t:t + 1, :], (p, n2)) for t in range(cs)], axis=0)

    def tile_rows(x):
        return jnp.concatenate([x] * cs, axis=0)

    def sel(lo, hi):
        return jnp.where(lo_half, lo, hi)

    crr, cii = tile_rows(c_r), tile_rows(c_i)
    brr, bii = tile_rows(bbr), tile_rows(bbi)
    c_taps = crr * rep_rows(sel(a0r, -a0i)) - cii * rep_rows(sel(a0i, a0r))
    c_out = crr * rep_rows(sel(a1r, -a1i)) - cii * rep_rows(sel(a1i, a1r))
    b_in = brr * rep_rows(sel(avr, avi)) + bii * rep_rows(sel(-avi, avr))
    b_in_sw = brr * rep_rows(sel(avi, avr)) + bii * rep_rows(sel(avr, -avi))

    b_mat = sel(bbr, bbi)
    taps = lax.dot_general(b_mat, c_taps, (((1,), (1,)), ((), ())),
                           precision=lax.Precision.HIGHEST, preferred_element_type=F32)

    lane = lax.broadcasted_iota(jnp.int32, (p, width), 1)
    band = [taps]
    for j in range(1, LANES // p):
        band.append(jnp.where(lane >= p * j, pltpu.roll(taps, p * j, axis=1), 0.0))
    band = jnp.concatenate(band, axis=0)
    n_blk = width // LANES
    band_t = jnp.concatenate([band[:, d * LANES:(d + 1) * LANES].T for d in reversed(range(n_blk))],
                             axis=1).astype(BF16)
    for q in range(n_blk):
        a_scr[q * LANES:(q + 1) * LANES, 0:(q + 1) * LANES] = band_t[:, (n_blk - 1 - q) * LANES:width]
    a_scr[width:width + n2, :] = b_in.T.astype(BF16)
    a_scr[width + n2:width + 2 * n2, :] = b_in_sw.T.astype(BF16)

    u = u_ref[...].reshape(width, u_ref.shape[2])
    res = jnp.dot(a_scr[...], u, preferred_element_type=F32)
    y = res[0:width]
    x_in = res[width:width + n2]
    x_sw = res[width + n2:width + 2 * n2]

    n_lanes = x_in.shape[1]
    levels = n_chunks.bit_length() - 1
    mults = []
    for k in range(levels):
        mr, mi = a_pow(jnp.full((1, 1), float(cs * 2 ** k), F32))
        mults += [mr, sel(-mi, mi), sel(mi, -mi)]
    mults = jnp.concatenate(mults + [jnp.zeros((n2 - len(mults), n2), F32)], axis=0)
    mcol = mults.T
    chunk_id = lax.rem(lax.broadcasted_iota(jnp.int32, (n2, n_lanes), 1), n_chunks)
    for k in range(levels):
        sh = 2 ** k
        ok = chunk_id >= sh
        xs = jnp.where(ok, pltpu.roll(x_in, sh, axis=1), 0.0)
        xs_sw = jnp.where(ok, pltpu.roll(x_sw, sh, axis=1), 0.0)
        m_rr, m_ni, m_pi = (mcol[:, 3 * k + j:3 * k + j + 1] for j in range(3))
        x_in, x_sw = (x_in + xs * m_rr + xs_sw * m_ni, x_sw + xs_sw * m_rr + xs * m_pi)
    h_start = jnp.where(chunk_id >= 1, pltpu.roll(x_in, 1, axis=1), 0.0).astype(BF16)

    y = y + jnp.dot(c_out.astype(BF16), h_start, preferred_element_type=F32)
    y3 = y.reshape(cs, p, n_lanes) + d_ref[0] * u_ref[...].astype(F32)
    o_ref[...] = _gelu_tanh(y3).astype(o_ref.dtype)


def _s5_core(u_t, a_re, a_im, log_dt, b_re, b_im, c_re, c_im, d, *, n_chunks):
    cs, dm, n_lanes = u_t.shape
    g = dm // S5_GROUP
    assert n_chunks & (n_chunks - 1) == 0
    dup = lambda x: jnp.concatenate([x, x], axis=-1)
    a_re2 = dup(a_re).reshape(g, 1, 2 * S5_STATE)
    a_im2 = dup(a_im).reshape(g, 1, 2 * S5_STATE)
    ldt = log_dt.reshape(g, 1, 1)
    bt_re2 = dup(jnp.swapaxes(b_re, 1, 2))
    bt_im2 = dup(jnp.swapaxes(b_im, 1, 2))
    c_re2 = dup(c_re)
    c_im2 = dup(c_im)
    d_c = d.reshape(g, S5_GROUP, 1)
    width = cs * S5_GROUP
    vec = lambda w: pl.BlockSpec((1, 1, w), lambda i: (i, 0, 0))
    mat = pl.BlockSpec((1, S5_GROUP, 2 * S5_STATE), lambda i: (i, 0, 0))
    blk = pl.BlockSpec((cs, S5_GROUP, n_lanes), lambda i: (0, i, 0))
    return pl.pallas_call(
        functools.partial(_s5_kernel, n_chunks=n_chunks),
        out_shape=jax.ShapeDtypeStruct((cs, dm, n_lanes), BF16),
        grid=(g,),
        in_specs=[blk, vec(2 * S5_STATE), vec(2 * S5_STATE), vec(1), mat, mat, mat, mat,
                  pl.BlockSpec((1, S5_GROUP, 1), lambda i: (i, 0, 0))],
        out_specs=blk,
        scratch_shapes=[pltpu.VMEM((width + 4 * S5_STATE, width), BF16)],
        compiler_params=_params("arbitrary"),
        name="s5_core",
    )(u_t, a_re2, a_im2, ldt, bt_re2, bt_im2, c_re2, c_im2, d_c)


S5_STEP = 8
LANE_ROWS = 128


def _s5_in_kernel(x_ref, w_ref, o_ref, u_scr):
    rows = LANE_ROWS * S5_STEP
    dm = x_ref.shape[2]
    xb = x_ref[...].reshape(rows, dm).astype(BF16)
    for n0 in range(0, dm, 512):
        acc = jnp.dot(xb, w_ref[:, n0:n0 + 512], preferred_element_type=F32)
        for c in range(512 // LANES):
            u_scr[n0 // LANES + c] = acc[:, c * LANES:(c + 1) * LANES]
    for j in range(S5_STEP):
        for cb in range(dm // LANES):
            tile = u_scr[cb, pl.ds(j, LANE_ROWS, stride=S5_STEP), :]
            o_ref[j, cb * LANES:(cb + 1) * LANES, :] = tile.T.astype(o_ref.dtype)


def _s5_in_proj(h3, w_all, layer):
    n_bc, cs, dm = h3.shape
    return pl.pallas_call(
        _s5_in_kernel,
        out_shape=jax.ShapeDtypeStruct((cs, dm, n_bc), BF16),
        grid=(cs // S5_STEP, n_bc // LANE_ROWS),
        in_specs=[pl.BlockSpec((LANE_ROWS, S5_STEP, dm), lambda s, r: (r, s, 0)), _layer_spec(w_all, layer)],
        out_specs=pl.BlockSpec((S5_STEP, dm, LANE_ROWS), lambda s, r: (s, 0, r)),
        scratch_shapes=[pltpu.VMEM((dm // LANES, LANE_ROWS * S5_STEP, LANES), F32)],
        compiler_params=_params("parallel", "parallel"),
        name="s5_in_proj",
    )(h3, w_all)


def _s5_glu_kernel(z_ref, w_ref, o_ref, z_scr, o_scr):
    nh = pl.program_id(2)
    half = o_ref.shape[2]

    @pl.when(nh == 0)
    def _():
        for j in range(S5_STEP):
            z_scr[j * LANE_ROWS:(j + 1) * LANE_ROWS, :] = z_ref[j].astype(F32).T

    zb = z_scr[...].astype(BF16)
    col0 = pl.multiple_of(nh * half, half)
    for n0 in range(0, half, 512):
        acc = jnp.dot(zb, w_ref[:, n0:n0 + 512], preferred_element_type=F32)
        zg = z_scr[:, pl.ds(col0 + n0, 512)] * jax.nn.sigmoid(acc)
        for j in range(S5_STEP):
            for c in range(512 // LANES):
                o_scr[n0 // LANES + c, pl.ds(j, LANE_ROWS, stride=S5_STEP), :] = (
                    zg[j * LANE_ROWS:(j + 1) * LANE_ROWS, c * LANES:(c + 1) * LANES])
    out = jnp.concatenate([o_scr[cb] for cb in range(half // LANES)], axis=1)
    o_ref[...] = out.reshape(o_ref.shape)


def _s5_glu(z_t, w_all, layer):
    cs, dm, n_bc = z_t.shape
    rows = LANE_ROWS * S5_STEP
    half = dm // 2
    return pl.pallas_call(
        _s5_glu_kernel,
        out_shape=jax.ShapeDtypeStruct((n_bc, cs, dm), F32),
        grid=(cs // S5_STEP, n_bc // LANE_ROWS, 2),
        in_specs=[pl.BlockSpec((S5_STEP, dm, LANE_ROWS), lambda s, r, nh: (s, 0, r)),
                  pl.BlockSpec((None, dm, half), lambda s, r, nh: (layer, 0, nh))],
        out_specs=pl.BlockSpec((LANE_ROWS, S5_STEP, half), lambda s, r, nh: (r, s, nh)),
        scratch_shapes=[pltpu.VMEM((rows, dm), F32), pltpu.VMEM((half // LANES, rows, LANES), F32)],
        compiler_params=_params("parallel", "parallel", "arbitrary"),
        name="s5_glu",
    )(z_t, w_all)


def _s5_mixer(h, res_g, res_b, layer, w_in, a_re, a_im, log_dt, b_re, b_im, c_re, c_im, d, w_glu, w_out,
              *, batch, seq):
    t = batch * seq
    nc = seq // S5_CHUNK
    u_t = _s5_in_proj(h.reshape(batch * nc, S5_CHUNK, D_MODEL), w_in, layer)
    z_t = _s5_core(u_t, a_re, a_im, log_dt, b_re, b_im, c_re, c_im, d, n_chunks=nc)
    zg = _s5_glu(z_t, w_glu, layer).reshape(t, D_MODEL)
    return _linear_ln(zg, w_out, layer, h, res_g, res_b, tm=256)


def _dsa_proj_kernel(x_ref, w_ref, qn_ref, kn_ref, cq_ref, ckv_ref, ckvt_ref, kidx_ref, widxt_ref):
    xb = x_ref[...].astype(BF16)

    def rms(v, g):
        return v * lax.rsqrt(jnp.mean(v * v, axis=-1, keepdims=True) + RMS_EPS) * g

    o1 = Q_LORA + KV_LORA
    cq = jnp.dot(xb, w_ref[:, 0:Q_LORA], preferred_element_type=F32)
    cq_ref[...] = rms(cq, qn_ref[...]).astype(BF16)
    ckv = rms(jnp.dot(xb, w_ref[:, Q_LORA:o1], preferred_element_type=F32), kn_ref[...])
    ckv_ref[...] = ckv.astype(BF16)
    ckvt_ref[...] = ckv.T.astype(BF16)
    kidx_ref[...] = jnp.dot(xb, w_ref[:, o1:o1 + IDX_DIM], preferred_element_type=F32).astype(BF16)
    widx = jnp.dot(xb, w_ref[:, o1 + IDX_DIM:o1 + IDX_DIM + LANES], preferred_element_type=F32)
    widxt_ref[...] = (widx * (IDX_HEADS ** -0.5)).T[0:IDX_HEADS, :]


def _dsa_proj(h, w_in_pad, layer, q_norm, kv_norm, *, batch, seq, tm=512):
    t = h.shape[0]
    per_b = seq // tm
    row = lambda w: pl.BlockSpec((tm, w), lambda i: (i, 0))
    return pl.pallas_call(
        _dsa_proj_kernel,
        out_shape=(jax.ShapeDtypeStruct((t, Q_LORA), BF16), jax.ShapeDtypeStruct((t, KV_LORA), BF16),
                   jax.ShapeDtypeStruct((batch, KV_LORA, seq), BF16),
                   jax.ShapeDtypeStruct((t, IDX_DIM), BF16), jax.ShapeDtypeStruct((IDX_HEADS, t), F32)),
        grid=(t // tm,),
        in_specs=[row(D_MODEL), _layer_spec(w_in_pad, layer),
                  pl.BlockSpec((1, Q_LORA), lambda i: (0, 0)), pl.BlockSpec((1, KV_LORA), lambda i: (0, 0))],
        out_specs=(row(Q_LORA), row(KV_LORA),
                   pl.BlockSpec((None, KV_LORA, tm), lambda i: (i // per_b, 0, i % per_b)),
                   row(IDX_DIM), pl.BlockSpec((IDX_HEADS, tm), lambda i: (0, i))),
        compiler_params=_params("parallel"),
        name="dsa_proj",
    )(h, w_in_pad, q_norm.reshape(1, Q_LORA), kv_norm.reshape(1, KV_LORA))


def _reduce_rows(op, x):
    rows, lanes = x.shape
    part = op(x.reshape(rows // 64, 8, 8, lanes), axis=0)
    return op(op(part, axis=0), axis=0, keepdims=True)


def _indexer_kernel(q_ref, k_ref, w_ref, s_ref, thr_ref, key_scr, *, seq):
    jb = pl.program_id(1)
    n_sel = float(min(IDX_TOPK, seq // 4))
    int_min = jnp.int32(-2 ** 31)
    flip = jnp.int32(0x7FFFFFFF)

    def run(width):
        k = k_ref[0:width, :]
        wrow = w_ref[...] * (IDX_DIM ** -0.5)
        acc = jnp.zeros((width, Q_BLOCK), F32)
        for g in range(IDX_HEADS // 2):
            qg = jnp.concatenate([q_ref[:, h * IDX_DIM:(h + 1) * IDX_DIM] for h in (2 * g, 2 * g + 1)], axis=0)
            s = lax.dot_general(k, qg, (((1,), (1,)), ((), ())), preferred_element_type=F32)
            acc = (acc + jnp.maximum(s[:, 0:Q_BLOCK], 0.0) * wrow[2 * g:2 * g + 1, :]
                   + jnp.maximum(s[:, Q_BLOCK:2 * Q_BLOCK], 0.0) * wrow[2 * g + 1:2 * g + 2, :])
        k_pos = lax.broadcasted_iota(jnp.int32, (width, Q_BLOCK), 0)
        q_pos = jb * Q_BLOCK + lax.broadcasted_iota(jnp.int32, (width, Q_BLOCK), 1)
        acc = jnp.where(k_pos <= q_pos, acc, -jnp.inf)
        s_ref[0:width, :] = acc
        if width < seq:
            s_ref[width:seq, :] = jnp.full((seq - width, Q_BLOCK), -jnp.inf, F32)

        bits = lax.bitcast_convert_type(acc, jnp.int32)
        key_scr[0:width, :] = jnp.where(bits < 0, bits ^ flip, bits)

        def count_ge(cand):
            return _reduce_rows(jnp.sum, (key_scr[0:width, :] >= cand).astype(F32))

        lo = jnp.where(count_ge(jnp.zeros((1, Q_BLOCK), jnp.int32)) >= n_sel, jnp.int32(0), int_min)

        def body(i, lo):
            cand = lo + jnp.left_shift(jnp.int32(1), jnp.int32(30) - i)
            return jnp.where(count_ge(cand) >= n_sel, cand, lo)

        lo = lax.fori_loop(0, 31, body, lo)
        thr = lax.bitcast_convert_type(jnp.where(lo < 0, lo ^ flip, lo), F32)
        thr_ref[...] = jnp.broadcast_to(thr, thr_ref.shape)

    cls_of_block = jb // (IDX_WIDTH_STEP // Q_BLOCK)
    min_cls = -(-int(n_sel) // IDX_WIDTH_STEP) - 1
    for cls in range(min_cls, seq // IDX_WIDTH_STEP):
        here = (cls_of_block <= cls) if cls == min_cls else (cls_of_block == cls)
        pl.when(here)(functools.partial(run, IDX_WIDTH_STEP * (cls + 1)))


def _indexer(qcat, kidx, widx_t, *, batch, seq):
    n_blk = seq // Q_BLOCK
    return pl.pallas_call(
        functools.partial(_indexer_kernel, seq=seq),
        out_shape=(jax.ShapeDtypeStruct((batch, n_blk, seq, Q_BLOCK), F32),
                   jax.ShapeDtypeStruct((batch, n_blk, 8, Q_BLOCK), F32)),
        grid=(batch, n_blk),
        in_specs=[pl.BlockSpec((Q_BLOCK, IDX_HEADS * IDX_DIM), lambda b, j: (b * n_blk + j, 1)),
                  pl.BlockSpec((seq, IDX_DIM), lambda b, j: (b, 0)),
                  pl.BlockSpec((IDX_HEADS, Q_BLOCK), lambda b, j: (0, b * n_blk + j))],
        out_specs=(pl.BlockSpec((None, None, seq, Q_BLOCK), lambda b, j: (b, j, 0, 0)),
                   pl.BlockSpec((None, None, 8, Q_BLOCK), lambda b, j: (b, j, 0, 0))),
        scratch_shapes=[pltpu.VMEM((seq, Q_BLOCK), jnp.int32)],
        compiler_params=_params("parallel", "parallel"),
        name="indexer",
    )(qcat, kidx, widx_t)


def _attn_kernel(jb_tbl, kc_tbl, q_ref, kv_ref, kvt_ref, s_ref, thr_ref, wuk_ref, wuvt_ref, bd_ref, bp_ref,
                 o_ref, ql_scr, m_scr, l_scr, acc_scr):
    jb = jb_tbl[pl.program_id(1)]
    kc = kc_tbl[pl.program_id(1)]
    last = (jb * Q_BLOCK) // KEY_CHUNK
    hd, qb = ATT_HEAD_DIM, Q_BLOCK
    sub = KEY_CHUNK // qb
    pair = 2 * qb

    @pl.when(kc == 0)
    def _():
        for h in range(ATT_HEADS):
            ql = jnp.dot(q_ref[:, h * hd:(h + 1) * hd], wuk_ref[h], preferred_element_type=F32)
            ql_scr[h * qb:(h + 1) * qb, :] = (ql * (hd ** -0.5)).astype(BF16)
        m_scr[...] = jnp.full(m_scr.shape, MASKED, F32)
        l_scr[...] = jnp.zeros(l_scr.shape, F32)
        acc_scr[...] = jnp.zeros(acc_scr.shape, F32)

    def attend(biases):
        width = len(biases) * qb
        kv = kv_ref[0:width, :]
        kvt = kvt_ref[:, 0:width]
        k_pos = kc * KEY_CHUNK + lax.broadcasted_iota(jnp.int32, (width, qb), 0)
        q_pos = jb * qb + lax.broadcasted_iota(jnp.int32, (width, qb), 1)
        keep = (s_ref[0:width, :] >= thr_ref[0:1, :]) & (k_pos <= q_pos)
        mask_add = jnp.where(keep, 0.0, MASKED)
        mask_add = jnp.concatenate([mask_add, mask_add], axis=1)
        for g in range(ATT_HEADS // 2):
            s = lax.dot_general(kv, ql_scr[g * pair:(g + 1) * pair, :], (((1,), (1,)), ((), ())),
                                preferred_element_type=F32)
            lg = s + mask_add
            if any(b is not None for b in biases):
                parts = []
                for c, b in enumerate(biases):
                    part = lg[c * qb:(c + 1) * qb]
                    if b is not None:
                        part = part + jnp.concatenate([b[2 * g], b[2 * g + 1]], axis=1)
                    parts.append(part)
                lg = jnp.concatenate(parts, axis=0)
            m_prev = m_scr[g, 0:1, :]
            m_new = jnp.maximum(m_prev, _reduce_rows(jnp.max, lg))
            corr = jnp.exp(m_prev - m_new)
            pr = jnp.exp(lg - m_new)
            l_scr[g] = jnp.broadcast_to(corr * l_scr[g, 0:1, :] + _reduce_rows(jnp.sum, pr), (8, pair))
            m_scr[g] = jnp.broadcast_to(m_new, (8, pair))
            pv = jnp.dot(kvt, pr.astype(BF16), preferred_element_type=F32)
            acc_scr[g] = acc_scr[g] * corr + pv

    r = jb - last * sub
    for nb in range(1, sub + 1):
        biases = (None,) * (nb - 2) + ((bp_ref,) if nb >= 2 else ()) + (bd_ref,)
        pl.when((kc == last) & (r == nb - 1))(functools.partial(attend, biases))
    prev_only = (kc == last - 1) & (r == 0)
    pl.when(prev_only)(functools.partial(attend, (None,) * (sub - 1) + (bp_ref,)))
    pl.when((kc < last) & jnp.logical_not(prev_only))(functools.partial(attend, (None,) * sub))

    @pl.when(kc == last)
    def _():
        for h in range(ATT_HEADS):
            g, half = h // 2, (h % 2) * qb
            ol = (acc_scr[g, :, half:half + qb] / l_scr[g, 0:1, half:half + qb]).astype(BF16)
            oh = jnp.dot(wuvt_ref[h], ol, preferred_element_type=F32)
            o_ref[:, h * hd:(h + 1) * hd] = oh.T.astype(o_ref.dtype)


def _rel_bucket_table():
    n = np.arange(2 * Q_BLOCK)
    max_exact = REL_BUCKETS // 2
    nf = np.maximum(n, 1).astype(np.float32)
    large = max_exact + (np.log(nf / np.float32(max_exact)) / np.float32(math.log(REL_MAX_DIST / max_exact))
                         * np.float32(REL_BUCKETS - max_exact)).astype(np.int32)
    large = np.minimum(large, REL_BUCKETS - 1)
    return np.where(n < max_exact, n, large)


def _attention(qcat, ckv, ckv_t, scores, thr, w_uk, w_uv_t, layer, rel_bias, *, batch, seq):
    t = batch * seq
    n_blk = seq // Q_BLOCK
    n_kc = seq // KEY_CHUNK
    bucket = _rel_bucket_table()
    assert bucket[Q_BLOCK + 1:].min() == REL_BUCKETS - 1
    kk = np.arange(Q_BLOCK)[:, None]
    qq = np.arange(Q_BLOCK)[None, :]
    rb = rel_bias.astype(F32)
    rb = rb - rb[REL_BUCKETS - 1]
    lookup = lambda dist: jnp.einsum("bh,kqb->hkq", rb, np.eye(REL_BUCKETS, dtype=np.float32)[bucket[dist]],
                                     precision=lax.Precision.HIGHEST)
    diag = jnp.where(jnp.asarray(kk <= qq)[None], lookup(np.maximum(qq - kk, 0)), 0.0)
    prev = lookup(Q_BLOCK + qq - kk)
    n_pair = ATT_HEADS // 2
    pair = 2 * Q_BLOCK

    steps = [(jj, kc) for jj in range(n_blk) for kc in range((jj * Q_BLOCK) // KEY_CHUNK + 1)]
    jb_tbl = jnp.asarray([s[0] for s in steps], jnp.int32)
    kc_tbl = jnp.asarray([s[1] for s in steps], jnp.int32)

    qrow = lambda b, s, jt, kt: (b * n_blk + jt[s], 0)
    full3 = lambda b, s, jt, kt: (0, 0, 0)
    return pl.pallas_call(
        _attn_kernel,
        out_shape=jax.ShapeDtypeStruct((t, ATT_HEADS * ATT_HEAD_DIM), BF16),
        grid_spec=pltpu.PrefetchScalarGridSpec(
            num_scalar_prefetch=2,
            grid=(batch, len(steps)),
            in_specs=[pl.BlockSpec((Q_BLOCK, ATT_HEADS * ATT_HEAD_DIM), qrow),
                      pl.BlockSpec((KEY_CHUNK, KV_LORA), lambda b, s, jt, kt: (b * n_kc + kt[s], 0)),
                      pl.BlockSpec((None, KV_LORA, KEY_CHUNK), lambda b, s, jt, kt: (b, 0, kt[s])),
                      pl.BlockSpec((None, None, KEY_CHUNK, Q_BLOCK), lambda b, s, jt, kt: (b, jt[s], kt[s], 0)),
                      pl.BlockSpec((None, None, 8, Q_BLOCK), lambda b, s, jt, kt: (b, jt[s], 0, 0)),
                      _layer_spec(w_uk, layer),
                      _layer_spec(w_uv_t, layer),
                      pl.BlockSpec((ATT_HEADS, Q_BLOCK, Q_BLOCK), full3),
                      pl.BlockSpec((ATT_HEADS, Q_BLOCK, Q_BLOCK), full3)],
            out_specs=pl.BlockSpec((Q_BLOCK, ATT_HEADS * ATT_HEAD_DIM), qrow),
            scratch_shapes=[pltpu.VMEM((ATT_HEADS * Q_BLOCK, KV_LORA), BF16),
                            pltpu.VMEM((n_pair, 8, pair), F32),
                            pltpu.VMEM((n_pair, 8, pair), F32),
                            pltpu.VMEM((n_pair, KV_LORA, pair), F32)]),
        compiler_params=_params("parallel", "arbitrary"),
        name="latent_attention",
    )(jb_tbl, kc_tbl, qcat, ckv, ckv_t, scores, thr, w_uk, w_uv_t, diag, prev)


def _dsa_mixer(h, res_g, res_b, layer, rel_bias, w_in_pad, q_norm, kv_norm, w_q, w_uk, w_uv_t, w_out,
               *, batch, seq):
    cq, ckv, ckv_t, kidx, widx_t = _dsa_proj(h, w_in_pad, layer, q_norm, kv_norm, batch=batch, seq=seq)
    qcat = _linear(cq, w_q, layer, tm=512, out_dtype=BF16)
    scores, thr = _indexer(qcat, kidx, widx_t, batch=batch, seq=seq)
    o = _attention(qcat, ckv, ckv_t, scores, thr, w_uk, w_uv_t, layer, rel_bias, batch=batch, seq=seq)
    return _linear_ln(o, w_out, layer, h, res_g, res_b, tm=256)


def _router_kernel(x_ref, w_ref, b_ref, route_ref, cnt_ref, cnt_scr):
    x = x_ref[...]
    x_hi = x.astype(BF16)
    x_lo = (x - x_hi.astype(F32)).astype(BF16)
    logits = (jnp.dot(x_hi, w_ref[0], preferred_element_type=F32)
              + jnp.dot(x_lo, w_ref[0], preferred_element_type=F32)
              + jnp.dot(x_hi, w_ref[1], preferred_element_type=F32)) + b_ref[...]
    lane = lax.broadcasted_iota(jnp.int32, logits.shape, 1)
    big = jnp.int32(LANES)
    is_group = (lane >= MOE_EXPERTS) & (lane < MOE_EXPERTS + MOE_GROUPS)
    gl = jnp.where(is_group, logits, -jnp.inf)
    gm = jnp.max(gl, axis=1, keepdims=True)
    g_p = 1.0 / jnp.sum(jnp.exp(gl - gm), axis=1, keepdims=True)
    g_idx = jnp.min(jnp.where(gl == gm, lane, big), axis=1, keepdims=True) - MOE_EXPERTS
    in_group = (lane < MOE_EXPERTS) & (jnp.right_shift(lane, 3) == g_idx)
    el = jnp.where(in_group, logits, -jnp.inf)
    em = jnp.max(el, axis=1, keepdims=True)
    ee = jnp.exp(el - em)
    prob = ee / jnp.sum(ee, axis=1, keepdims=True)
    p1 = jnp.max(prob, axis=1, keepdims=True)
    i1 = jnp.min(jnp.where(in_group & (prob == p1), lane, big), axis=1, keepdims=True)
    rest = in_group & (lane != i1)
    p2 = jnp.max(jnp.where(rest, prob, -1.0), axis=1, keepdims=True)
    i2 = jnp.min(jnp.where(rest & (prob == p2), lane, big), axis=1, keepdims=True)
    den = p1 + p2
    g1 = g_p * (p1 / den)
    g2 = g_p * (p2 / den)

    @pl.when(pl.program_id(0) == 0)
    def _():
        cnt_scr[...] = jnp.zeros(cnt_scr.shape, F32)

    tm = logits.shape[0]
    oh1 = (lane == i1).astype(BF16)
    oh2 = (lane == i2).astype(BF16)
    r_i = lax.broadcasted_iota(jnp.int32, (tm, tm), 0)
    c_i = lax.broadcasted_iota(jnp.int32, (tm, tm), 1)
    tri = (c_i < r_i).astype(BF16)
    pre1 = jnp.dot(tri, oh1, preferred_element_type=F32)
    pre2 = jnp.dot(tri, oh2, preferred_element_type=F32)
    tot1 = jnp.sum(oh1.astype(F32), axis=0, keepdims=True)
    tot2 = jnp.sum(oh2.astype(F32), axis=0, keepdims=True)
    base = cnt_scr[...]
    rank1 = jnp.sum(jnp.where(lane == i1, base + pre1, 0.0), axis=1, keepdims=True)
    rank2 = jnp.sum(jnp.where(lane == i2, base + tot1 + pre2, 0.0), axis=1, keepdims=True)
    cnt_scr[...] = base + tot1 + tot2
    cnt_ref[...] = jnp.broadcast_to(cnt_scr[...], cnt_ref.shape)

    out = jnp.where(lane == 0, i1.astype(F32), jnp.where(lane == 1, i2.astype(F32), 0.0))
    out = jnp.where(lane == 2, g1, jnp.where(lane == 3, g2, out))
    out = jnp.where(lane == 4, rank1, jnp.where(lane == 5, rank2, out))
    route_ref[...] = out


def _router(h, w_r, b_r, *, tm=512):
    t = h.shape[0]
    return pl.pallas_call(
        _router_kernel,
        out_shape=(jax.ShapeDtypeStruct((t, LANES), F32), jax.ShapeDtypeStruct((8, LANES), F32)),
        grid=(t // tm,),
        in_specs=[pl.BlockSpec((tm, D_MODEL), lambda i: (i, 0)),
                  pl.BlockSpec((2, D_MODEL, LANES), lambda i: (0, 0, 0)),
                  pl.BlockSpec((1, LANES), lambda i: (0, 0))],
        out_specs=(pl.BlockSpec((tm, LANES), lambda i: (i, 0)),
                   pl.BlockSpec((8, LANES), lambda i: (0, 0))),
        scratch_shapes=[pltpu.VMEM((1, LANES), F32)],
        compiler_params=_params("arbitrary"),
        name="moe_router",
    )(h, w_r, b_r)


def _moe_expert_kernel(te_ref, nu_ref, rt_ref, x_hbm, wg_ref, wu_ref, wd_ref, y_ref,
                       xbuf, sem, wgb, wub, wdb):
    i = pl.program_id(0)
    n_used = nu_ref[0]
    nbuf = GATHER_AHEAD + 1
    slot = lax.rem(i, nbuf)
    tme = EXPERT_TILE
    pr = PACK_ROWS

    def start_gather(tile, buf):
        def body(r, carry):
            src = pl.multiple_of(rt_ref[tile * tme + r] * pr, pr)
            dst = pl.multiple_of(r * pr, pr)
            pltpu.make_async_copy(x_hbm.at[pl.ds(src, pr)], xbuf.at[buf, pl.ds(dst, pr)], sem.at[buf]).start()
            return carry
        lax.fori_loop(0, tme, body, 0, unroll=8)

    for j in range(GATHER_AHEAD):
        pl.when((i == 0) & (j < n_used))(functools.partial(start_gather, j, j))

    @pl.when(i + GATHER_AHEAD < n_used)
    def _():
        start_gather(i + GATHER_AHEAD, lax.rem(i + GATHER_AHEAD, nbuf))

    te = te_ref[i]
    prev = te_ref[jnp.maximum(i - 1, 0)]

    @pl.when((i == 0) | (te != prev))
    def _():
        wgb[...] = wg_ref[...].astype(BF16)
        wub[...] = wu_ref[...].astype(BF16)
        wdb[...] = wd_ref[...].astype(BF16)

    @pl.when(i < n_used)
    def _():
        pltpu.make_async_copy(x_hbm.at[pl.ds(0, tme * pr)], xbuf.at[slot], sem.at[slot]).wait()
        xb = _load_packed(xbuf.at[slot], tme).astype(BF16)
        gp = jnp.dot(xb, wgb[...], preferred_element_type=F32)
        up = jnp.dot(xb, wub[...], preferred_element_type=F32)
        hidden = (gp * jax.nn.sigmoid(gp)) * up
        _store_packed(y_ref, jnp.dot(hidden.astype(BF16), wdb[...], preferred_element_type=F32))

    @pl.when(i >= n_used)
    def _():
        y_ref[...] = jnp.zeros(y_ref.shape, y_ref.dtype)


def _moe_experts(tile_expert, n_used, row_token, h_packed, w_gate, w_up, w_down, layer):
    n_tiles = tile_expert.shape[0]
    rows = EXPERT_TILE * PACK_ROWS
    wspec = lambda a, b: pl.BlockSpec((None, None, a, b), lambda i, te, nu, rt: (layer, te[i], 0, 0))
    return pl.pallas_call(
        _moe_expert_kernel,
        out_shape=jax.ShapeDtypeStruct((n_tiles * rows, LANES), jnp.uint32),
        grid_spec=pltpu.PrefetchScalarGridSpec(
            num_scalar_prefetch=3,
            grid=(n_tiles,),
            in_specs=[pl.BlockSpec(memory_space=pl.ANY),
                      wspec(D_MODEL, MOE_FF), wspec(D_MODEL, MOE_FF), wspec(MOE_FF, D_MODEL)],
            out_specs=pl.BlockSpec((rows, LANES), lambda i, te, nu, rt: (i, 0)),
            scratch_shapes=[pltpu.VMEM((GATHER_AHEAD + 1, rows, LANES), jnp.uint32),
                            pltpu.SemaphoreType.DMA((GATHER_AHEAD + 1,)),
                            pltpu.VMEM((D_MODEL, MOE_FF), BF16),
                            pltpu.VMEM((D_MODEL, MOE_FF), BF16),
                            pltpu.VMEM((MOE_FF, D_MODEL), BF16)]),
        compiler_params=_params("arbitrary"),
        name="moe_experts",
    )(tile_expert, n_used, row_token, h_packed, w_gate, w_up, w_down)


def _moe_combine_kernel(slot_ref, h_ref, route_ref, y_hbm, g_ref, b_ref, o_ref, ybuf, sem):
    i = pl.program_id(0)
    n = pl.num_programs(0)
    nbuf = GATHER_AHEAD + 1
    slot = lax.rem(i, nbuf)
    tm = h_ref.shape[0]
    pr = PACK_ROWS

    def start_gather(tile, buf):
        def body(r, carry):
            for k in range(2):
                src = pl.multiple_of(slot_ref[(tile * tm + r) * 2 + k] * pr, pr)
                dst = pl.multiple_of((k * tm + r) * pr, pr)
                pltpu.make_async_copy(y_hbm.at[pl.ds(src, pr)], ybuf.at[buf, pl.ds(dst, pr)], sem.at[buf]).start()
            return carry
        lax.fori_loop(0, tm, body, 0, unroll=4)

    for j in range(GATHER_AHEAD):
        pl.when((i == 0) & (j < n))(functools.partial(start_gather, j, j))

    @pl.when(i + GATHER_AHEAD < n)
    def _():
        start_gather(i + GATHER_AHEAD, lax.rem(i + GATHER_AHEAD, nbuf))

    pltpu.make_async_copy(y_hbm.at[pl.ds(0, 2 * tm * pr)], ybuf.at[slot], sem.at[slot]).wait()
    y1 = _load_packed(ybuf.at[slot, pl.ds(0, tm * pr)], tm)
    y2 = _load_packed(ybuf.at[slot, pl.ds(tm * pr, tm * pr)], tm)
    ffn = route_ref[:, 2:3] * y1 + route_ref[:, 3:4] * y2
    o_ref[...] = _layer_norm_rows(DN_ALPHA * h_ref[...] + ffn, g_ref[...], b_ref[...])


def _moe_combine(slots, h, route, y, ln_g, ln_b, *, tm):
    t = h.shape[0]
    row = lambda w: pl.BlockSpec((tm, w), lambda i, s: (i, 0))
    vec = pl.BlockSpec((1, D_MODEL), lambda i, s: (0, 0))
    return pl.pallas_call(
        _moe_combine_kernel,
        out_shape=jax.ShapeDtypeStruct((t, D_MODEL), F32),
        grid_spec=pltpu.PrefetchScalarGridSpec(
            num_scalar_prefetch=1,
            grid=(t // tm,),
            in_specs=[row(D_MODEL), row(LANES), pl.BlockSpec(memory_space=pl.ANY), vec, vec],
            out_specs=row(D_MODEL),
            scratch_shapes=[pltpu.VMEM((GATHER_AHEAD + 1, 2 * tm * PACK_ROWS, LANES), jnp.uint32),
                            pltpu.SemaphoreType.DMA((GATHER_AHEAD + 1,))]),
        compiler_params=_params("arbitrary"),
        name="moe_combine",
    )(slots, h, route, y, ln_g.reshape(1, D_MODEL), ln_b.reshape(1, D_MODEL))


def _moe_plan(route, counts, n_tiles):
    t = route.shape[0]
    tme = EXPERT_TILE
    cnt = counts[0, :MOE_EXPERTS].astype(jnp.int32)
    tiles = (cnt + tme - 1) // tme
    tile_end = jnp.cumsum(tiles)
    tile_start = tile_end - tiles
    n_used = tile_end[-1]
    choice = route[:, 0:2].astype(jnp.int32)
    rank = route[:, 4:6].astype(jnp.int32)
    expert_ids = jnp.arange(MOE_EXPERTS, dtype=jnp.int32)
    start = jnp.sum(jnp.where(choice[..., None] == expert_ids, tile_start, 0), axis=-1)
    slots = (start * tme + rank).reshape(-1)
    tile_ids = jnp.arange(n_tiles, dtype=jnp.int32)
    te = jnp.sum((tile_ids[:, None] >= tile_end[None, :]).astype(jnp.int32), axis=1)
    last_e = jnp.sum((n_used - 1 >= tile_end).astype(jnp.int32))
    te = jnp.where(tile_ids < n_used, te, last_e)
    row_token = jnp.zeros((n_tiles * tme,), jnp.int32).at[slots].set(jnp.arange(2 * t, dtype=jnp.int32) // 2)
    return te, n_used.reshape(1).astype(jnp.int32), row_token, slots.astype(jnp.int32)


def _hier_moe(h, h_packed, layer, w_group, b_group, w_expert, b_expert, w_gate, w_up, w_down, ln_g, ln_b):
    t = h.shape[0]
    pad = LANES - MOE_EXPERTS - MOE_GROUPS
    w_r = jnp.concatenate([w_expert, w_group, jnp.zeros((D_MODEL, pad), F32)], axis=1)
    w_hi = w_r.astype(BF16)
    w_r = jnp.stack([w_hi, (w_r - w_hi.astype(F32)).astype(BF16)])
    b_r = jnp.concatenate([b_expert, b_group, jnp.zeros((pad,), F32)]).reshape(1, LANES)
    route, counts = _router(h, w_r, b_r)
    n_tiles = (2 * t) // EXPERT_TILE + MOE_EXPERTS
    te, n_used, row_token, slots = _moe_plan(route, counts, n_tiles)
    y = _moe_experts(te, n_used, row_token, h_packed, w_gate, w_up, w_down, layer)
    return _moe_combine(slots, h, route, y, ln_g, ln_b, tm=COMBINE_TILE)


def kernel(x, rel_bias, s5_w_in, s5_a_re, s5_a_im, s5_log_dt, s5_b_re, s5_b_im, s5_c_re, s5_c_im, s5_d, s5_w_glu, s5_w_out, dsa_w_in, dsa_q_norm, dsa_kv_norm, dsa_w_uq, dsa_w_qidx, dsa_w_uk, dsa_w_uv, dsa_w_out, moe_w_group, moe_b_group, moe_w_expert, moe_b_expert, moe_w_gate, moe_w_up, moe_w_down, ln_mix_g, ln_mix_b, ln_ffn_g, ln_ffn_b):
    batch, seq, dm = x.shape
    h = x.reshape(batch * seq, dm)
    s5_w_in, s5_w_glu, s5_w_out = (w.astype(BF16) for w in (s5_w_in, s5_w_glu, s5_w_out))
    n_pad = Q_LORA + KV_LORA + IDX_DIM + LANES - dsa_w_in.shape[-1]
    dsa_w_in_pad = jnp.pad(dsa_w_in, ((0, 0), (0, 0), (0, n_pad))).astype(BF16)
    dsa_w_q = jnp.concatenate([dsa_w_uq, dsa_w_qidx], axis=-1).astype(BF16)
    dsa_w_uk, dsa_w_out = dsa_w_uk.astype(BF16), dsa_w_out.astype(BF16)
    dsa_w_uv_t = jnp.swapaxes(dsa_w_uv, -1, -2).astype(BF16)
    for i in range(DEPTH):
        j = i // 2
        if i % 2 == 0:
            h, hp = _s5_mixer(h, ln_mix_g[i], ln_mix_b[i], j, s5_w_in, s5_a_re[j], s5_a_im[j], s5_log_dt[j],
                              s5_b_re[j], s5_b_im[j], s5_c_re[j], s5_c_im[j], s5_d[j], s5_w_glu, s5_w_out,
                              batch=batch, seq=seq)
        else:
            h, hp = _dsa_mixer(h, ln_mix_g[i], ln_mix_b[i], j, rel_bias, dsa_w_in_pad, dsa_q_norm[j],
                               dsa_kv_norm[j], dsa_w_q, dsa_w_uk, dsa_w_uv_t, dsa_w_out, batch=batch, seq=seq)
        h = _hier_moe(h, hp, i, moe_w_group[i], moe_b_group[i], moe_w_expert[i], moe_b_expert[i],
                      moe_w_gate, moe_w_up, moe_w_down, ln_ffn_g[i], ln_ffn_b[i])
    return h.reshape(batch, seq, dm)
```

```python
import functools
import math

import numpy as np
import jax
import jax.numpy as jnp
from jax import lax
from jax.experimental import pallas as pl
from jax.experimental.pallas import tpu as pltpu

F32 = jnp.float32
BF16 = jnp.bfloat16

D_MODEL = 2048
DEPTH = 4
S5_GROUP = 16
S5_GROUPS = D_MODEL // S5_GROUP
S5_STATE = 64
S5_CHUNK = 64
ATT_HEADS = 16
ATT_HEAD_DIM = 128
Q_LORA = 512
KV_LORA = 512
IDX_HEADS = 16
IDX_DIM = 128
IDX_TOPK = 256
Q_BLOCK = 128
KEY_CHUNK = 512
IDX_WIDTH_STEP = 128
REL_BUCKETS = 32
REL_MAX_DIST = 128
MOE_GROUPS = 4
MOE_PER_GROUP = 8
MOE_EXPERTS = 32
MOE_FF = 256
EXPERT_TILE = 512
COMBINE_TILE = 256
GATHER_AHEAD = 2
DN_ALPHA = (2 * DEPTH) ** 0.25
LN_EPS = 1e-5
RMS_EPS = 1e-6

LANES = 128
MASKED = -1e30
VMEM_LIMIT = 56 * 1024 * 1024


def _params(*sem):
    return pltpu.CompilerParams(dimension_semantics=sem, vmem_limit_bytes=VMEM_LIMIT)


def _layer_norm_rows(y, g, b):
    mean = jnp.mean(y, axis=-1, keepdims=True)
    yc = y - mean
    var = jnp.mean(yc * yc, axis=-1, keepdims=True)
    return yc * lax.rsqrt(var + LN_EPS) * g + b


def _mm_plain_kernel(x_ref, w_ref, o_ref, *, tn):
    xb = x_ref[...].astype(BF16)
    for n0 in range(0, o_ref.shape[1], tn):
        acc = jnp.dot(xb, w_ref[:, n0:n0 + tn], preferred_element_type=F32)
        o_ref[:, n0:n0 + tn] = acc.astype(o_ref.dtype)


def _mm_glu_kernel(z_ref, w_ref, o_ref, *, tn):
    zb = z_ref[...]
    for n0 in range(0, o_ref.shape[1], tn):
        acc = jnp.dot(zb, w_ref[:, n0:n0 + tn], preferred_element_type=F32)
        zc = z_ref[:, n0:n0 + tn].astype(F32)
        o_ref[:, n0:n0 + tn] = (zc * jax.nn.sigmoid(acc)).astype(o_ref.dtype)


PACK_ROWS = D_MODEL // (2 * LANES)


def _store_packed(o_ref, y):
    half = y.shape[1] // 2
    bits = lambda v: lax.bitcast_convert_type(v.astype(BF16).astype(F32), jnp.uint32)
    word = (bits(y[:, half:]) & jnp.uint32(0xFFFF0000)) | (bits(y[:, :half]) >> 16)
    rows = y.shape[0]
    for c in range(PACK_ROWS):
        o_ref[pl.ds(c, rows, stride=PACK_ROWS), :] = word[:, c * LANES:(c + 1) * LANES]


def _load_packed(x_ref, rows):
    word = jnp.concatenate([x_ref[pl.ds(c, rows, stride=PACK_ROWS), :] for c in range(PACK_ROWS)], axis=1)
    lo = lax.bitcast_convert_type(word << 16, F32)
    hi = lax.bitcast_convert_type(word & jnp.uint32(0xFFFF0000), F32)
    return jnp.concatenate([lo, hi], axis=1)


def _mm_ln_kernel(x_ref, w_ref, res_ref, g_ref, b_ref, o_ref, op_ref, *, tn):
    xb = x_ref[...].astype(BF16)
    for n0 in range(0, o_ref.shape[1], tn):
        acc = jnp.dot(xb, w_ref[:, n0:n0 + tn], preferred_element_type=F32)
        o_ref[:, n0:n0 + tn] = DN_ALPHA * res_ref[:, n0:n0 + tn] + acc
    y = _layer_norm_rows(o_ref[...], g_ref[...], b_ref[...])
    o_ref[...] = y
    _store_packed(op_ref, y)


def _layer_spec(w_all, layer):
    shape = w_all.shape[1:]
    return pl.BlockSpec((None,) + shape, lambda *_: (layer,) + (0,) * len(shape))


def _linear(x, w_all, layer, *, tm, out_dtype, tn=512):
    m, k = x.shape
    n = w_all.shape[-1]
    return pl.pallas_call(
        functools.partial(_mm_plain_kernel, tn=tn),
        out_shape=jax.ShapeDtypeStruct((m, n), out_dtype),
        grid=(m // tm,),
        in_specs=[pl.BlockSpec((tm, k), lambda i: (i, 0)), _layer_spec(w_all, layer)],
        out_specs=pl.BlockSpec((tm, n), lambda i: (i, 0)),
        compiler_params=_params("parallel"),
        name="linear",
    )(x, w_all)


def _linear_glu(z, w_all, layer, *, tm, tn=512):
    m, k = z.shape
    return pl.pallas_call(
        functools.partial(_mm_glu_kernel, tn=tn),
        out_shape=jax.ShapeDtypeStruct((m, k), BF16),
        grid=(m // tm,),
        in_specs=[pl.BlockSpec((tm, k), lambda i: (i, 0)), _layer_spec(w_all, layer)],
        out_specs=pl.BlockSpec((tm, k), lambda i: (i, 0)),
        compiler_params=_params("parallel"),
        name="linear_glu",
    )(z, w_all)


def _linear_ln(x, w_all, layer, res, g, b, *, tm, tn=512):
    m, k = x.shape
    n = w_all.shape[-1]
    return pl.pallas_call(
        functools.partial(_mm_ln_kernel, tn=tn),
        out_shape=(jax.ShapeDtypeStruct((m, n), F32), jax.ShapeDtypeStruct((m * PACK_ROWS, LANES), jnp.uint32)),
        grid=(m // tm,),
        in_specs=[pl.BlockSpec((tm, k), lambda i: (i, 0)),
                  _layer_spec(w_all, layer),
                  pl.BlockSpec((tm, n), lambda i: (i, 0)),
                  pl.BlockSpec((1, n), lambda i: (0, 0)),
                  pl.BlockSpec((1, n), lambda i: (0, 0))],
        out_specs=(pl.BlockSpec((tm, n), lambda i: (i, 0)),
                   pl.BlockSpec((tm * PACK_ROWS, LANES), lambda i: (i, 0))),
        compiler_params=_params("parallel"),
        name="linear_ln",
    )(x, w_all, res, g.reshape(1, n), b.reshape(1, n))


def _gelu_tanh(x):
    return 0.5 * x * (1.0 + jnp.tanh(math.sqrt(2.0 / math.pi) * (x + 0.044715 * x * x * x)))


def _s5_kernel(u_ref, ar_ref, ai_ref, ldt_ref, br_ref, bi_ref, cr_ref, ci_ref, d_ref, o_ref, a_scr,
               *, n_chunks):
    cs, p, n = S5_CHUNK, S5_GROUP, S5_STATE
    width = cs * p
    n2 = 2 * n

    @pl.when(pl.program_id(0) == 0)
    def _():
        a_scr[...] = jnp.zeros(a_scr.shape, a_scr.dtype)

    dt = jnp.exp(ldt_ref[0])
    lam_r = jnp.minimum(ar_ref[0], -1e-4)
    lam_i = ai_ref[0]
    lo_half = lax.broadcasted_iota(jnp.int32, (1, n2), 1) < n

    def a_pow(tau):
        mag = jnp.exp(lam_r * dt * tau)
        ph = lam_i * dt * tau
        return mag * jnp.cos(ph), mag * jnp.sin(ph)

    tau = lax.broadcasted_iota(jnp.int32, (cs, 1), 0).astype(F32)
    a0r, a0i = a_pow(tau)
    a1r, a1i = a_pow(tau + 1.0)
    avr, avi = a_pow((cs - 1.0) - tau)

    abr, abi = a1r[0:1], a1i[0:1]
    den = lam_r * lam_r + lam_i * lam_i
    kr = ((abr - 1.0) * lam_r + abi * lam_i) / den
    ki = (abi * lam_r - (abr - 1.0) * lam_i) / den
    b_r, b_i = br_ref[0], bi_ref[0]
    bbr = kr * b_r - ki * b_i
    bbi = kr * b_i + ki * b_r
    c_r, c_i = cr_ref[0], ci_ref[0]

    def rep_rows(x):
        return jnp.concatenate([jnp.broadcast_to(x[t:t + 1, :], (p, n2)) for t in range(cs)], axis=0)

    def tile_rows(x):
        return jnp.concatenate([x] * cs, axis=0)

    def sel(lo, hi):
        return jnp.where(lo_half, lo, hi)

    crr, cii = tile_rows(c_r), tile_rows(c_i)
    brr, bii = tile_rows(bbr), tile_rows(bbi)
    c_taps = crr * rep_rows(sel(a0r, -a0i)) - cii * rep_rows(sel(a0i, a0r))
    c_out = crr * rep_rows(sel(a1r, -a1i)) - cii * rep_rows(sel(a1i, a1r))
    b_in = brr * rep_rows(sel(avr, avi)) + bii * rep_rows(sel(-avi, avr))
    b_in_sw = brr * rep_rows(sel(avi, avr)) + bii * rep_rows(sel(avr, -avi))

    b_mat = sel(bbr, bbi)
    taps = lax.dot_general(b_mat, c_taps, (((1,), (1,)), ((), ())),
                           precision=lax.Precision.HIGHEST, preferred_element_type=F32)

    lane = lax.broadcasted_iota(jnp.int32, (p, width), 1)
    band = [taps]
    for j in range(1, LANES // p):
        band.append(jnp.where(lane >= p * j, pltpu.roll(taps, p * j, axis=1), 0.0))
    band = jnp.concatenate(band, axis=0)
    n_blk = width // LANES
    band_t = jnp.concatenate([band[:, d * LANES:(d + 1) * LANES].T for d in reversed(range(n_blk))],
                             axis=1).astype(BF16)
    for q in range(n_blk):
        a_scr[q * LANES:(q + 1) * LANES, 0:(q + 1) * LANES] = band_t[:, (n_blk - 1 - q) * LANES:width]
    a_scr[width:width + n2, :] = b_in.T.astype(BF16)
    a_scr[width + n2:width + 2 * n2, :] = b_in_sw.T.astype(BF16)

    u = u_ref[...].reshape(width, u_ref.shape[2])
    res = jnp.dot(a_scr[...], u, preferred_element_type=F32)
    y = res[0:width]
    x_in = res[width:width + n2]
    x_sw = res[width + n2:width + 2 * n2]

    n_lanes = x_in.shape[1]
    levels = n_chunks.bit_length() - 1
    mults = []
    for k in range(levels):
        mr, mi = a_pow(jnp.full((1, 1), float(cs * 2 ** k), F32))
        mults += [mr, sel(-mi, mi), sel(mi, -mi)]
    mults = jnp.concatenate(mults + [jnp.zeros((n2 - len(mults), n2), F32)], axis=0)
    mcol = mults.T
    chunk_id = lax.rem(lax.broadcasted_iota(jnp.int32, (n2, n_lanes), 1), n_chunks)
    for k in range(levels):
        sh = 2 ** k
        ok = chunk_id >= sh
        xs = jnp.where(ok, pltpu.roll(x_in, sh, axis=1), 0.0)
        xs_sw = jnp.where(ok, pltpu.roll(x_sw, sh, axis=1), 0.0)
        m_rr, m_ni, m_pi = (mcol[:, 3 * k + j:3 * k + j + 1] for j in range(3))
        x_in, x_sw = (x_in + xs * m_rr + xs_sw * m_ni, x_sw + xs_sw * m_rr + xs * m_pi)
    h_start = jnp.where(chunk_id >= 1, pltpu.roll(x_in, 1, axis=1), 0.0).astype(BF16)

    y = y + jnp.dot(c_out.astype(BF16), h_start, preferred_element_type=F32)
    y3 = y.reshape(cs, p, n_lanes) + d_ref[0] * u_ref[...].astype(F32)
    o_ref[...] = _gelu_tanh(y3).astype(o_ref.dtype)


def _s5_core(u_t, a_re, a_im, log_dt, b_re, b_im, c_re, c_im, d, *, n_chunks):
    cs, dm, n_lanes = u_t.shape
    g = dm // S5_GROUP
    assert n_chunks & (n_chunks - 1) == 0
    dup = lambda x: jnp.concatenate([x, x], axis=-1)
    a_re2 = dup(a_re).reshape(g, 1, 2 * S5_STATE)
    a_im2 = dup(a_im).reshape(g, 1, 2 * S5_STATE)
    ldt = log_dt.reshape(g, 1, 1)
    bt_re2 = dup(jnp.swapaxes(b_re, 1, 2))
    bt_im2 = dup(jnp.swapaxes(b_im, 1, 2))
    c_re2 = dup(c_re)
    c_im2 = dup(c_im)
    d_c = d.reshape(g, S5_GROUP, 1)
    width = cs * S5_GROUP
    vec = lambda w: pl.BlockSpec((1, 1, w), lambda i: (i, 0, 0))
    mat = pl.BlockSpec((1, S5_GROUP, 2 * S5_STATE), lambda i: (i, 0, 0))
    blk = pl.BlockSpec((cs, S5_GROUP, n_lanes), lambda i: (0, i, 0))
    return pl.pallas_call(
        functools.partial(_s5_kernel, n_chunks=n_chunks),
        out_shape=jax.ShapeDtypeStruct((cs, dm, n_lanes), BF16),
        grid=(g,),
        in_specs=[blk, vec(2 * S5_STATE), vec(2 * S5_STATE), vec(1), mat, mat, mat, mat,
                  pl.BlockSpec((1, S5_GROUP, 1), lambda i: (i, 0, 0))],
        out_specs=blk,
        scratch_shapes=[pltpu.VMEM((width + 4 * S5_STATE, width), BF16)],
        compiler_params=_params("arbitrary"),
        name="s5_core",
    )(u_t, a_re2, a_im2, ldt, bt_re2, bt_im2, c_re2, c_im2, d_c)


S5_STEP = 8
LANE_ROWS = 128


def _s5_in_kernel(x_ref, w_ref, o_ref, u_scr):
    rows = LANE_ROWS * S5_STEP
    dm = x_ref.shape[2]
    xb = x_ref[...].reshape(rows, dm).astype(BF16)
    for n0 in range(0, dm, 512):
        acc = jnp.dot(xb, w_ref[:, n0:n0 + 512], preferred_element_type=F32)
        for c in range(512 // LANES):
            u_scr[n0 // LANES + c] = acc[:, c * LANES:(c + 1) * LANES]
    for j in range(S5_STEP):
        for cb in range(dm // LANES):
            tile = u_scr[cb, pl.ds(j, LANE_ROWS, stride=S5_STEP), :]
            o_ref[j, cb * LANES:(cb + 1) * LANES, :] = tile.T.astype(o_ref.dtype)


def _s5_in_proj(h3, w_all, layer):
    n_bc, cs, dm = h3.shape
    return pl.pallas_call(
        _s5_in_kernel,
        out_shape=jax.ShapeDtypeStruct((cs, dm, n_bc), BF16),
        grid=(cs // S5_STEP, n_bc // LANE_ROWS),
        in_specs=[pl.BlockSpec((LANE_ROWS, S5_STEP, dm), lambda s, r: (r, s, 0)), _layer_spec(w_all, layer)],
        out_specs=pl.BlockSpec((S5_STEP, dm, LANE_ROWS), lambda s, r: (s, 0, r)),
        scratch_shapes=[pltpu.VMEM((dm // LANES, LANE_ROWS * S5_STEP, LANES), F32)],
        compiler_params=_params("parallel", "parallel"),
        name="s5_in_proj",
    )(h3, w_all)


def _s5_glu_kernel(z_ref, w_ref, o_ref, z_scr, o_scr):
    nh = pl.program_id(2)
    half = o_ref.shape[2]

    @pl.when(nh == 0)
    def _():
        for j in range(S5_STEP):
            z_scr[j * LANE_ROWS:(j + 1) * LANE_ROWS, :] = z_ref[j].astype(F32).T

    zb = z_scr[...].astype(BF16)
    col0 = pl.multiple_of(nh * half, half)
    for n0 in range(0, half, 512):
        acc = jnp.dot(zb, w_ref[:, n0:n0 + 512], preferred_element_type=F32)
        zg = z_scr[:, pl.ds(col0 + n0, 512)] * jax.nn.sigmoid(acc)
        for j in range(S5_STEP):
            for c in range(512 // LANES):
                o_scr[n0 // LANES + c, pl.ds(j, LANE_ROWS, stride=S5_STEP), :] = (
                    zg[j * LANE_ROWS:(j + 1) * LANE_ROWS, c * LANES:(c + 1) * LANES])
    out = jnp.concatenate([o_scr[cb] for cb in range(half // LANES)], axis=1)
    o_ref[...] = out.reshape(o_ref.shape)


def _s5_glu(z_t, w_all, layer):
    cs, dm, n_bc = z_t.shape
    rows = LANE_ROWS * S5_STEP
    half = dm // 2
    return pl.pallas_call(
        _s5_glu_kernel,
        out_shape=jax.ShapeDtypeStruct((n_bc, cs, dm), F32),
        grid=(cs // S5_STEP, n_bc // LANE_ROWS, 2),
        in_specs=[pl.BlockSpec((S5_STEP, dm, LANE_ROWS), lambda s, r, nh: (s, 0, r)),
                  pl.BlockSpec((None, dm, half), lambda s, r, nh: (layer, 0, nh))],
        out_specs=pl.BlockSpec((LANE_ROWS, S5_STEP, half), lambda s, r, nh: (r, s, nh)),
        scratch_shapes=[pltpu.VMEM((rows, dm), F32), pltpu.VMEM((half // LANES, rows, LANES), F32)],
        compiler_params=_params("parallel", "parallel", "arbitrary"),
        name="s5_glu",
    )(z_t, w_all)


def _s5_mixer(h, res_g, res_b, layer, w_in, a_re, a_im, log_dt, b_re, b_im, c_re, c_im, d, w_glu, w_out,
              *, batch, seq):
    t = batch * seq
    nc = seq // S5_CHUNK
    u_t = _s5_in_proj(h.reshape(batch * nc, S5_CHUNK, D_MODEL), w_in, layer)
    z_t = _s5_core(u_t, a_re, a_im, log_dt, b_re, b_im, c_re, c_im, d, n_chunks=nc)
    zg = _s5_glu(z_t, w_glu, layer).reshape(t, D_MODEL)
    return _linear_ln(zg, w_out, layer, h, res_g, res_b, tm=256)


def _dsa_proj_kernel(x_ref, w_ref, qn_ref, kn_ref, cq_ref, ckv_ref, ckvt_ref, kidx_ref, widxt_ref):
    xb = x_ref[...].astype(BF16)

    def rms(v, g):
        return v * lax.rsqrt(jnp.mean(v * v, axis=-1, keepdims=True) + RMS_EPS) * g

    o1 = Q_LORA + KV_LORA
    cq = jnp.dot(xb, w_ref[:, 0:Q_LORA], preferred_element_type=F32)
    cq_ref[...] = rms(cq, qn_ref[...]).astype(BF16)
    ckv = rms(jnp.dot(xb, w_ref[:, Q_LORA:o1], preferred_element_type=F32), kn_ref[...])
    ckv_ref[...] = ckv.astype(BF16)
    ckvt_ref[...] = ckv.T.astype(BF16)
    kidx_ref[...] = jnp.dot(xb, w_ref[:, o1:o1 + IDX_DIM], preferred_element_type=F32).astype(BF16)
    widx = jnp.dot(xb, w_ref[:, o1 + IDX_DIM:o1 + IDX_DIM + LANES], preferred_element_type=F32)
    widxt_ref[...] = (widx * (IDX_HEADS ** -0.5)).T[0:IDX_HEADS, :]


def _dsa_proj(h, w_in_pad, layer, q_norm, kv_norm, *, batch, seq, tm=512):
    t = h.shape[0]
    per_b = seq // tm
    row = lambda w: pl.BlockSpec((tm, w), lambda i: (i, 0))
    return pl.pallas_call(
        _dsa_proj_kernel,
        out_shape=(jax.ShapeDtypeStruct((t, Q_LORA), BF16), jax.ShapeDtypeStruct((t, KV_LORA), BF16),
                   jax.ShapeDtypeStruct((batch, KV_LORA, seq), BF16),
                   jax.ShapeDtypeStruct((t, IDX_DIM), BF16), jax.ShapeDtypeStruct((IDX_HEADS, t), F32)),
        grid=(t // tm,),
        in_specs=[row(D_MODEL), _layer_spec(w_in_pad, layer),
                  pl.BlockSpec((1, Q_LORA), lambda i: (0, 0)), pl.BlockSpec((1, KV_LORA), lambda i: (0, 0))],
        out_specs=(row(Q_LORA), row(KV_LORA),
                   pl.BlockSpec((None, KV_LORA, tm), lambda i: (i // per_b, 0, i % per_b)),
                   row(IDX_DIM), pl.BlockSpec((IDX_HEADS, tm), lambda i: (0, i))),
        compiler_params=_params("parallel"),
        name="dsa_proj",
    )(h, w_in_pad, q_norm.reshape(1, Q_LORA), kv_norm.reshape(1, KV_LORA))


def _reduce_rows(op, x):
    rows, lanes = x.shape
    part = op(x.reshape(rows // 64, 8, 8, lanes), axis=0)
    return op(op(part, axis=0), axis=0, keepdims=True)


def _indexer_kernel(q_ref, k_ref, w_ref, s_ref, thr_ref, key_scr, *, seq):
    jb = pl.program_id(1)
    n_sel = float(min(IDX_TOPK, seq // 4))
    int_min = jnp.int32(-2 ** 31)
    flip = jnp.int32(0x7FFFFFFF)

    def run(width):
        k = k_ref[0:width, :]
        wrow = w_ref[...] * (IDX_DIM ** -0.5)
        acc = jnp.zeros((width, Q_BLOCK), F32)
        for g in range(IDX_HEADS // 2):
            qg = jnp.concatenate([q_ref[:, h * IDX_DIM:(h + 1) * IDX_DIM] for h in (2 * g, 2 * g + 1)], axis=0)
            s = lax.dot_general(k, qg, (((1,), (1,)), ((), ())), preferred_element_type=F32)
            acc = (acc + jnp.maximum(s[:, 0:Q_BLOCK], 0.0) * wrow[2 * g:2 * g + 1, :]
                   + jnp.maximum(s[:, Q_BLOCK:2 * Q_BLOCK], 0.0) * wrow[2 * g + 1:2 * g + 2, :])
        k_pos = lax.broadcasted_iota(jnp.int32, (width, Q_BLOCK), 0)
        q_pos = jb * Q_BLOCK + lax.broadcasted_iota(jnp.int32, (width, Q_BLOCK), 1)
        acc = jnp.where(k_pos <= q_pos, acc, -jnp.inf)
        s_ref[0:width, :] = acc
        if width < seq:
            s_ref[width:seq, :] = jnp.full((seq - width, Q_BLOCK), -jnp.inf, F32)

        bits = lax.bitcast_convert_type(acc, jnp.int32)
        key_scr[0:width, :] = jnp.where(bits < 0, bits ^ flip, bits)

        def count_ge(cand):
            return _reduce_rows(jnp.sum, (key_scr[0:width, :] >= cand).astype(F32))

        lo = jnp.where(count_ge(jnp.zeros((1, Q_BLOCK), jnp.int32)) >= n_sel, jnp.int32(0), int_min)

        def body(i, lo):
            cand = lo + jnp.left_shift(jnp.int32(1), jnp.int32(30) - i)
            return jnp.where(count_ge(cand) >= n_sel, cand, lo)

        lo = lax.fori_loop(0, 31, body, lo)
        thr = lax.bitcast_convert_type(jnp.where(lo < 0, lo ^ flip, lo), F32)
        thr_ref[...] = jnp.broadcast_to(thr, thr_ref.shape)

    cls_of_block = jb // (IDX_WIDTH_STEP // Q_BLOCK)
    min_cls = -(-int(n_sel) // IDX_WIDTH_STEP) - 1
    for cls in range(min_cls, seq // IDX_WIDTH_STEP):
        here = (cls_of_block <= cls) if cls == min_cls else (cls_of_block == cls)
        pl.when(here)(functools.partial(run, IDX_WIDTH_STEP * (cls + 1)))


def _indexer(qcat, kidx, widx_t, *, batch, seq):
    n_blk = seq // Q_BLOCK
    return pl.pallas_call(
        functools.partial(_indexer_kernel, seq=seq),
        out_shape=(jax.ShapeDtypeStruct((batch, n_blk, seq, Q_BLOCK), F32),
                   jax.ShapeDtypeStruct((batch, n_blk, 8, Q_BLOCK), F32)),
        grid=(batch, n_blk),
        in_specs=[pl.BlockSpec((Q_BLOCK, IDX_HEADS * IDX_DIM), lambda b, j: (b * n_blk + j, 1)),
                  pl.BlockSpec((seq, IDX_DIM), lambda b, j: (b, 0)),
                  pl.BlockSpec((IDX_HEADS, Q_BLOCK), lambda b, j: (0, b * n_blk + j))],
        out_specs=(pl.BlockSpec((None, None, seq, Q_BLOCK), lambda b, j: (b, j, 0, 0)),
                   pl.BlockSpec((None, None, 8, Q_BLOCK), lambda b, j: (b, j, 0, 0))),
        scratch_shapes=[pltpu.VMEM((seq, Q_BLOCK), jnp.int32)],
        compiler_params=_params("parallel", "parallel"),
        name="indexer",
    )(qcat, kidx, widx_t)


def _attn_kernel(jb_tbl, kc_tbl, q_ref, kv_ref, kvt_ref, s_ref, thr_ref, wuk_ref, wuvt_ref, bd_ref, bp_ref,
                 o_ref, ql_scr, m_scr, l_scr, acc_scr):
    jb = jb_tbl[pl.program_id(1)]
    kc = kc_tbl[pl.program_id(1)]
    last = (jb * Q_BLOCK) // KEY_CHUNK
    hd, qb = ATT_HEAD_DIM, Q_BLOCK
    sub = KEY_CHUNK // qb
    pair = 2 * qb

    @pl.when(kc == 0)
    def _():
        for h in range(ATT_HEADS):
            ql = jnp.dot(q_ref[:, h * hd:(h + 1) * hd], wuk_ref[h], preferred_element_type=F32)
            ql_scr[h * qb:(h + 1) * qb, :] = (ql * (hd ** -0.5)).astype(BF16)
        m_scr[...] = jnp.full(m_scr.shape, MASKED, F32)
        l_scr[...] = jnp.zeros(l_scr.shape, F32)
        acc_scr[...] = jnp.zeros(acc_scr.shape, F32)

    def attend(biases):
        width = len(biases) * qb
        kv = kv_ref[0:width, :]
        kvt = kvt_ref[:, 0:width]
        k_pos = kc * KEY_CHUNK + lax.broadcasted_iota(jnp.int32, (width, qb), 0)
        q_pos = jb * qb + lax.broadcasted_iota(jnp.int32, (width, qb), 1)
        keep = (s_ref[0:width, :] >= thr_ref[0:1, :]) & (k_pos <= q_pos)
        mask_add = jnp.where(keep, 0.0, MASKED)
        mask_add = jnp.concatenate([mask_add, mask_add], axis=1)
        for g in range(ATT_HEADS // 2):
            s = lax.dot_general(kv, ql_scr[g * pair:(g + 1) * pair, :], (((1,), (1,)), ((), ())),
                                preferred_element_type=F32)
            lg = s + mask_add
            if any(b is not None for b in biases):
                parts = []
                for c, b in enumerate(biases):
                    part = lg[c * qb:(c + 1) * qb]
                    if b is not None:
                        part = part + jnp.concatenate([b[2 * g], b[2 * g + 1]], axis=1)
                    parts.append(part)
                lg = jnp.concatenate(parts, axis=0)
            m_prev = m_scr[g, 0:1, :]
            m_new = jnp.maximum(m_prev, _reduce_rows(jnp.max, lg))
            corr = jnp.exp(m_prev - m_new)
            pr = jnp.exp(lg - m_new)
            l_scr[g] = jnp.broadcast_to(corr * l_scr[g, 0:1, :] + _reduce_rows(jnp.sum, pr), (8, pair))
            m_scr[g] = jnp.broadcast_to(m_new, (8, pair))
            pv = jnp.dot(kvt, pr.astype(BF16), preferred_element_type=F32)
            acc_scr[g] = acc_scr[g] * corr + pv

    r = jb - last * sub
    for nb in range(1, sub + 1):
        biases = (None,) * (nb - 2) + ((bp_ref,) if nb >= 2 else ()) + (bd_ref,)
        pl.when((kc == last) & (r == nb - 1))(functools.partial(attend, biases))
    prev_only = (kc == last - 1) & (r == 0)
    pl.when(prev_only)(functools.partial(attend, (None,) * (sub - 1) + (bp_ref,)))
    pl.when((kc < last) & jnp.logical_not(prev_only))(functools.partial(attend, (None,) * sub))

    @pl.when(kc == last)
    def _():
        for h in range(ATT_HEADS):
            g, half = h // 2, (h % 2) * qb
            ol = (acc_scr[g, :, half:half + qb] / l_scr[g, 0:1, half:half + qb]).astype(BF16)
            oh = jnp.dot(wuvt_ref[h], ol, preferred_element_type=F32)
            o_ref[:, h * hd:(h + 1) * hd] = oh.T.astype(o_ref.dtype)


def _rel_bucket_table():
    n = np.arange(2 * Q_BLOCK)
    max_exact = REL_BUCKETS // 2
    nf = np.maximum(n, 1).astype(np.float32)
    large = max_exact + (np.log(nf / np.float32(max_exact)) / np.float32(math.log(REL_MAX_DIST / max_exact))
                         * np.float32(REL_BUCKETS - max_exact)).astype(np.int32)
    large = np.minimum(large, REL_BUCKETS - 1)
    return np.where(n < max_exact, n, large)


def _attention(qcat, ckv, ckv_t, scores, thr, w_uk, w_uv_t, layer, rel_bias, *, batch, seq):
    t = batch * seq
    n_blk = seq // Q_BLOCK
    n_kc = seq // KEY_CHUNK
    bucket = _rel_bucket_table()
    assert bucket[Q_BLOCK + 1:].min() == REL_BUCKETS - 1
    kk = np.arange(Q_BLOCK)[:, None]
    qq = np.arange(Q_BLOCK)[None, :]
    rb = rel_bias.astype(F32)
    rb = rb - rb[REL_BUCKETS - 1]
    lookup = lambda dist: jnp.einsum("bh,kqb->hkq", rb, np.eye(REL_BUCKETS, dtype=np.float32)[bucket[dist]],
                                     precision=lax.Precision.HIGHEST)
    diag = jnp.where(jnp.asarray(kk <= qq)[None], lookup(np.maximum(qq - kk, 0)), 0.0)
    prev = lookup(Q_BLOCK + qq - kk)
    n_pair = ATT_HEADS // 2
    pair = 2 * Q_BLOCK

    steps = [(jj, kc) for jj in range(n_blk) for kc in range((jj * Q_BLOCK) // KEY_CHUNK + 1)]
    jb_tbl = jnp.asarray([s[0] for s in steps], jnp.int32)
    kc_tbl = jnp.asarray([s[1] for s in steps], jnp.int32)

    qrow = lambda b, s, jt, kt: (b * n_blk + jt[s], 0)
    full3 = lambda b, s, jt, kt: (0, 0, 0)
    return pl.pallas_call(
        _attn_kernel,
        out_shape=jax.ShapeDtypeStruct((t, ATT_HEADS * ATT_HEAD_DIM), BF16),
        grid_spec=pltpu.PrefetchScalarGridSpec(
            num_scalar_prefetch=2,
            grid=(batch, len(steps)),
            in_specs=[pl.BlockSpec((Q_BLOCK, ATT_HEADS * ATT_HEAD_DIM), qrow),
                      pl.BlockSpec((KEY_CHUNK, KV_LORA), lambda b, s, jt, kt: (b * n_kc + kt[s], 0)),
                      pl.BlockSpec((None, KV_LORA, KEY_CHUNK), lambda b, s, jt, kt: (b, 0, kt[s])),
                      pl.BlockSpec((None, None, KEY_CHUNK, Q_BLOCK), lambda b, s, jt, kt: (b, jt[s], kt[s], 0)),
                      pl.BlockSpec((None, None, 8, Q_BLOCK), lambda b, s, jt, kt: (b, jt[s], 0, 0)),
                      _layer_spec(w_uk, layer),
                      _layer_spec(w_uv_t, layer),
                      pl.BlockSpec((ATT_HEADS, Q_BLOCK, Q_BLOCK), full3),
                      pl.BlockSpec((ATT_HEADS, Q_BLOCK, Q_BLOCK), full3)],
            out_specs=pl.BlockSpec((Q_BLOCK, ATT_HEADS * ATT_HEAD_DIM), qrow),
            scratch_shapes=[pltpu.VMEM((ATT_HEADS * Q_BLOCK, KV_LORA), BF16),
                            pltpu.VMEM((n_pair, 8, pair), F32),
                            pltpu.VMEM((n_pair, 8, pair), F32),
                            pltpu.VMEM((n_pair, KV_LORA, pair), F32)]),
        compiler_params=_params("parallel", "arbitrary"),
        name="latent_attention",
    )(jb_tbl, kc_tbl, qcat, ckv, ckv_t, scores, thr, w_uk, w_uv_t, diag, prev)


def _dsa_mixer(h, res_g, res_b, layer, rel_bias, w_in_pad, q_norm, kv_norm, w_q, w_uk, w_uv_t, w_out,
               *, batch, seq):
    cq, ckv, ckv_t, kidx, widx_t = _dsa_proj(h, w_in_pad, layer, q_norm, kv_norm, batch=batch, seq=seq)
    qcat = _linear(cq, w_q, layer, tm=512, out_dtype=BF16)
    scores, thr = _indexer(qcat, kidx, widx_t, batch=batch, seq=seq)
    o = _attention(qcat, ckv, ckv_t, scores, thr, w_uk, w_uv_t, layer, rel_bias, batch=batch, seq=seq)
    return _linear_ln(o, w_out, layer, h, res_g, res_b, tm=256)


def _router_kernel(x_ref, w_ref, b_ref, route_ref, cnt_ref, cnt_scr):
    x = x_ref[...]
    x_hi = x.astype(BF16)
    x_lo = (x - x_hi.astype(F32)).astype(BF16)
    logits = (jnp.dot(x_hi, w_ref[0], preferred_element_type=F32)
              + jnp.dot(x_lo, w_ref[0], preferred_element_type=F32)
              + jnp.dot(x_hi, w_ref[1], preferred_element_type=F32)) + b_ref[...]
    lane = lax.broadcasted_iota(jnp.int32, logits.shape, 1)
    big = jnp.int32(LANES)
    is_group = (lane >= MOE_EXPERTS) & (lane < MOE_EXPERTS + MOE_GROUPS)
    gl = jnp.where(is_group, logits, -jnp.inf)
    gm = jnp.max(gl, axis=1, keepdims=True)
    g_p = 1.0 / jnp.sum(jnp.exp(gl - gm), axis=1, keepdims=True)
    g_idx = jnp.min(jnp.where(gl == gm, lane, big), axis=1, keepdims=True) - MOE_EXPERTS
    in_group = (lane < MOE_EXPERTS) & (jnp.right_shift(lane, 3) == g_idx)
    el = jnp.where(in_group, logits, -jnp.inf)
    em = jnp.max(el, axis=1, keepdims=True)
    ee = jnp.exp(el - em)
    prob = ee / jnp.sum(ee, axis=1, keepdims=True)
    p1 = jnp.max(prob, axis=1, keepdims=True)
    i1 = jnp.min(jnp.where(in_group & (prob == p1), lane, big), axis=1, keepdims=True)
    rest = in_group & (lane != i1)
    p2 = jnp.max(jnp.where(rest, prob, -1.0), axis=1, keepdims=True)
    i2 = jnp.min(jnp.where(rest & (prob == p2), lane, big), axis=1, keepdims=True)
    den = p1 + p2
    g1 = g_p * (p1 / den)
    g2 = g_p * (p2 / den)

    @pl.when(pl.program_id(0) == 0)
    def _():
        cnt_scr[...] = jnp.zeros(cnt_scr.shape, F32)

    tm = logits.shape[0]
    oh1 = (lane == i1).astype(BF16)
    oh2 = (lane == i2).astype(BF16)
    r_i = lax.broadcasted_iota(jnp.int32, (tm, tm), 0)
    c_i = lax.broadcasted_iota(jnp.int32, (tm, tm), 1)
    tri = (c_i < r_i).astype(BF16)
    pre1 = jnp.dot(tri, oh1, preferred_element_type=F32)
    pre2 = jnp.dot(tri, oh2, preferred_element_type=F32)
    tot1 = jnp.sum(oh1.astype(F32), axis=0, keepdims=True)
    tot2 = jnp.sum(oh2.astype(F32), axis=0, keepdims=True)
    base = cnt_scr[...]
    rank1 = jnp.sum(jnp.where(lane == i1, base + pre1, 0.0), axis=1, keepdims=True)
    rank2 = jnp.sum(jnp.where(lane == i2, base + tot1 + pre2, 0.0), axis=1, keepdims=True)
    cnt_scr[...] = base + tot1 + tot2
    cnt_ref[...] = jnp.broadcast_to(cnt_scr[...], cnt_ref.shape)

    out = jnp.where(lane == 0, i1.astype(F32), jnp.where(lane == 1, i2.astype(F32), 0.0))
    out = jnp.where(lane == 2, g1, jnp.where(lane == 3, g2, out))
    out = jnp.where(lane == 4, rank1, jnp.where(lane == 5, rank2, out))
    route_ref[...] = out


def _router(h, w_r, b_r, *, tm=512):
    t = h.shape[0]
    return pl.pallas_call(
        _router_kernel,
        out_shape=(jax.ShapeDtypeStruct((t, LANES), F32), jax.ShapeDtypeStruct((8, LANES), F32)),
        grid=(t // tm,),
        in_specs=[pl.BlockSpec((tm, D_MODEL), lambda i: (i, 0)),
                  pl.BlockSpec((2, D_MODEL, LANES), lambda i: (0, 0, 0)),
                  pl.BlockSpec((1, LANES), lambda i: (0, 0))],
        out_specs=(pl.BlockSpec((tm, LANES), lambda i: (i, 0)),
                   pl.BlockSpec((8, LANES), lambda i: (0, 0))),
        scratch_shapes=[pltpu.VMEM((1, LANES), F32)],
        compiler_params=_params("arbitrary"),
        name="moe_router",
    )(h, w_r, b_r)


def _moe_expert_kernel(te_ref, nu_ref, rt_ref, x_hbm, wg_ref, wu_ref, wd_ref, y_ref,
                       xbuf, sem, wgb, wub, wdb):
    i = pl.program_id(0)
    n_used = nu_ref[0]
    nbuf = GATHER_AHEAD + 1
    slot = lax.rem(i, nbuf)
    tme = EXPERT_TILE
    pr = PACK_ROWS

    def start_gather(tile, buf):
        def body(r, carry):
            src = pl.multiple_of(rt_ref[tile * tme + r] * pr, pr)
            dst = pl.multiple_of(r * pr, pr)
            pltpu.make_async_copy(x_hbm.at[pl.ds(src, pr)], xbuf.at[buf, pl.ds(dst, pr)], sem.at[buf]).start()
            return carry
        lax.fori_loop(0, tme, body, 0, unroll=8)

    for j in range(GATHER_AHEAD):
        pl.when((i == 0) & (j < n_used))(functools.partial(start_gather, j, j))

    @pl.when(i + GATHER_AHEAD < n_used)
    def _():
        start_gather(i + GATHER_AHEAD, lax.rem(i + GATHER_AHEAD, nbuf))

    te = te_ref[i]
    prev = te_ref[jnp.maximum(i - 1, 0)]

    @pl.when((i == 0) | (te != prev))
    def _():
        wgb[...] = wg_ref[...].astype(BF16)
        wub[...] = wu_ref[...].astype(BF16)
        wdb[...] = wd_ref[...].astype(BF16)

    @pl.when(i < n_used)
    def _():
        pltpu.make_async_copy(x_hbm.at[pl.ds(0, tme * pr)], xbuf.at[slot], sem.at[slot]).wait()
        xb = _load_packed(xbuf.at[slot], tme).astype(BF16)
        gp = jnp.dot(xb, wgb[...], preferred_element_type=F32)
        up = jnp.dot(xb, wub[...], preferred_element_type=F32)
        hidden = (gp * jax.nn.sigmoid(gp)) * up
        _store_packed(y_ref, jnp.dot(hidden.astype(BF16), wdb[...], preferred_element_type=F32))

    @pl.when(i >= n_used)
    def _():
        y_ref[...] = jnp.zeros(y_ref.shape, y_ref.dtype)


def _moe_experts(tile_expert, n_used, row_token, h_packed, w_gate, w_up, w_down, layer):
    n_tiles = tile_expert.shape[0]
    rows = EXPERT_TILE * PACK_ROWS
    wspec = lambda a, b: pl.BlockSpec((None, None, a, b), lambda i, te, nu, rt: (layer, te[i], 0, 0))
    return pl.pallas_call(
        _moe_expert_kernel,
        out_shape=jax.ShapeDtypeStruct((n_tiles * rows, LANES), jnp.uint32),
        grid_spec=pltpu.PrefetchScalarGridSpec(
            num_scalar_prefetch=3,
            grid=(n_tiles,),
            in_specs=[pl.BlockSpec(memory_space=pl.ANY),
                      wspec(D_MODEL, MOE_FF), wspec(D_MODEL, MOE_FF), wspec(MOE_FF, D_MODEL)],
            out_specs=pl.BlockSpec((rows, LANES), lambda i, te, nu, rt: (i, 0)),
            scratch_shapes=[pltpu.VMEM((GATHER_AHEAD + 1, rows, LANES), jnp.uint32),
                            pltpu.SemaphoreType.DMA((GATHER_AHEAD + 1,)),
                            pltpu.VMEM((D_MODEL, MOE_FF), BF16),
                            pltpu.VMEM((D_MODEL, MOE_FF), BF16),
                            pltpu.VMEM((MOE_FF, D_MODEL), BF16)]),
        compiler_params=_params("arbitrary"),
        name="moe_experts",
    )(tile_expert, n_used, row_token, h_packed, w_gate, w_up, w_down)


def _moe_combine_kernel(slot_ref, h_ref, route_ref, y_hbm, g_ref, b_ref, o_ref, ybuf, sem):
    i = pl.program_id(0)
    n = pl.num_programs(0)
    nbuf = GATHER_AHEAD + 1
    slot = lax.rem(i, nbuf)
    tm = h_ref.shape[0]
    pr = PACK_ROWS

    def start_gather(tile, buf):
        def body(r, carry):
            for k in range(2):
                src = pl.multiple_of(slot_ref[(tile * tm + r) * 2 + k] * pr, pr)
                dst = pl.multiple_of((k * tm + r) * pr, pr)
                pltpu.make_async_copy(y_hbm.at[pl.ds(src, pr)], ybuf.at[buf, pl.ds(dst, pr)], sem.at[buf]).start()
            return carry
        lax.fori_loop(0, tm, body, 0, unroll=4)

    for j in range(GATHER_AHEAD):
        pl.when((i == 0) & (j < n))(functools.partial(start_gather, j, j))

    @pl.when(i + GATHER_AHEAD < n)
    def _():
        start_gather(i + GATHER_AHEAD, lax.rem(i + GATHER_AHEAD, nbuf))

    pltpu.make_async_copy(y_hbm.at[pl.ds(0, 2 * tm * pr)], ybuf.at[slot], sem.at[slot]).wait()
    y1 = _load_packed(ybuf.at[slot, pl.ds(0, tm * pr)], tm)
    y2 = _load_packed(ybuf.at[slot, pl.ds(tm * pr, tm * pr)], tm)
    ffn = route_ref[:, 2:3] * y1 + route_ref[:, 3:4] * y2
    o_ref[...] = _layer_norm_rows(DN_ALPHA * h_ref[...] + ffn, g_ref[...], b_ref[...])


def _moe_combine(slots, h, route, y, ln_g, ln_b, *, tm):
    t = h.shape[0]
    row = lambda w: pl.BlockSpec((tm, w), lambda i, s: (i, 0))
    vec = pl.BlockSpec((1, D_MODEL), lambda i, s: (0, 0))
    return pl.pallas_call(
        _moe_combine_kernel,
        out_shape=jax.ShapeDtypeStruct((t, D_MODEL), F32),
        grid_spec=pltpu.PrefetchScalarGridSpec(
            num_scalar_prefetch=1,
            grid=(t // tm,),
            in_specs=[row(D_MODEL), row(LANES), pl.BlockSpec(memory_space=pl.ANY), vec, vec],
            out_specs=row(D_MODEL),
            scratch_shapes=[pltpu.VMEM((GATHER_AHEAD + 1, 2 * tm * PACK_ROWS, LANES), jnp.uint32),
                            pltpu.SemaphoreType.DMA((GATHER_AHEAD + 1,))]),
        compiler_params=_params("arbitrary"),
        name="moe_combine",
    )(slots, h, route, y, ln_g.reshape(1, D_MODEL), ln_b.reshape(1, D_MODEL))


def _moe_plan(route, counts, n_tiles):
    t = route.shape[0]
    tme = EXPERT_TILE
    cnt = counts[0, :MOE_EXPERTS].astype(jnp.int32)
    tiles = (cnt + tme - 1) // tme
    tile_end = jnp.cumsum(tiles)
    tile_start = tile_end - tiles
    n_used = tile_end[-1]
    choice = route[:, 0:2].astype(jnp.int32)
    rank = route[:, 4:6].astype(jnp.int32)
    expert_ids = jnp.arange(MOE_EXPERTS, dtype=jnp.int32)
    start = jnp.sum(jnp.where(choice[..., None] == expert_ids, tile_start, 0), axis=-1)
    slots = (start * tme + rank).reshape(-1)
    tile_ids = jnp.arange(n_tiles, dtype=jnp.int32)
    te = jnp.sum((tile_ids[:, None] >= tile_end[None, :]).astype(jnp.int32), axis=1)
    last_e = jnp.sum((n_used - 1 >= tile_end).astype(jnp.int32))
    te = jnp.where(tile_ids < n_used, te, last_e)
    row_token = jnp.zeros((n_tiles * tme,), jnp.int32).at[slots].set(jnp.arange(2 * t, dtype=jnp.int32) // 2)
    return te, n_used.reshape(1).astype(jnp.int32), row_token, slots.astype(jnp.int32)


def _hier_moe(h, h_packed, layer, w_group, b_group, w_expert, b_expert, w_gate, w_up, w_down, ln_g, ln_b):
    t = h.shape[0]
    pad = LANES - MOE_EXPERTS - MOE_GROUPS
    w_r = jnp.concatenate([w_expert, w_group, jnp.zeros((D_MODEL, pad), F32)], axis=1)
    w_hi = w_r.astype(BF16)
    w_r = jnp.stack([w_hi, (w_r - w_hi.astype(F32)).astype(BF16)])
    b_r = jnp.concatenate([b_expert, b_group, jnp.zeros((pad,), F32)]).reshape(1, LANES)
    route, counts = _router(h, w_r, b_r)
    n_tiles = (2 * t) // EXPERT_TILE + MOE_EXPERTS
    te, n_used, row_token, slots = _moe_plan(route, counts, n_tiles)
    y = _moe_experts(te, n_used, row_token, h_packed, w_gate, w_up, w_down, layer)
    return _moe_combine(slots, h, route, y, ln_g, ln_b, tm=COMBINE_TILE)


def kernel(x, rel_bias, s5_w_in, s5_a_re, s5_a_im, s5_log_dt, s5_b_re, s5_b_im, s5_c_re, s5_c_im, s5_d, s5_w_glu, s5_w_out, dsa_w_in, dsa_q_norm, dsa_kv_norm, dsa_w_uq, dsa_w_qidx, dsa_w_uk, dsa_w_uv, dsa_w_out, moe_w_group, moe_b_group, moe_w_expert, moe_b_expert, moe_w_gate, moe_w_up, moe_w_down, ln_mix_g, ln_mix_b, ln_ffn_g, ln_ffn_b):
    batch, seq, dm = x.shape
    h = x.reshape(batch * seq, dm)
    s5_w_in, s5_w_glu, s5_w_out = (w.astype(BF16) for w in (s5_w_in, s5_w_glu, s5_w_out))
    n_pad = Q_LORA + KV_LORA + IDX_DIM + LANES - dsa_w_in.shape[-1]
    dsa_w_in_pad = jnp.pad(dsa_w_in, ((0, 0), (0, 0), (0, n_pad))).astype(BF16)
    dsa_w_q = jnp.concatenate([dsa_w_uq, dsa_w_qidx], axis=-1).astype(BF16)
    dsa_w_uk, dsa_w_out = dsa_w_uk.astype(BF16), dsa_w_out.astype(BF16)
    dsa_w_uv_t = jnp.swapaxes(dsa_w_uv, -1, -2).astype(BF16)
    for i in range(DEPTH):
        j = i // 2
        if i % 2 == 0:
            h, hp = _s5_mixer(h, ln_mix_g[i], ln_mix_b[i], j, s5_w_in, s5_a_re[j], s5_a_im[j], s5_log_dt[j],
                              s5_b_re[j], s5_b_im[j], s5_c_re[j], s5_c_im[j], s5_d[j], s5_w_glu, s5_w_out,
                              batch=batch, seq=seq)
        else:
            h, hp = _dsa_mixer(h, ln_mix_g[i], ln_mix_b[i], j, rel_bias, dsa_w_in_pad, dsa_q_norm[j],
                               dsa_kv_norm[j], dsa_w_q, dsa_w_uk, dsa_w_uv_t, dsa_w_out, batch=batch, seq=seq)
        h = _hier_moe(h, hp, i, moe_w_group[i], moe_b_group[i], moe_w_expert[i], moe_b_expert[i],
                      moe_w_gate, moe_w_up, moe_w_down, ln_ffn_g[i], ln_ffn_b[i])
    return h.reshape(batch, seq, dm)
```

```python
import functools
import math

import numpy as np
import jax
import jax.numpy as jnp
from jax import lax
from jax.experimental import pallas as pl
from jax.experimental.pallas import tpu as pltpu

F32 = jnp.float32
BF16 = jnp.bfloat16

D_MODEL = 2048
DEPTH = 4
S5_GROUP = 16
S5_GROUPS = D_MODEL // S5_GROUP
S5_STATE = 64
S5_CHUNK = 64
ATT_HEADS = 16
ATT_HEAD_DIM = 128
Q_LORA = 512
KV_LORA = 512
IDX_HEADS = 16
IDX_DIM = 128
IDX_TOPK = 256
Q_BLOCK = 128
KEY_CHUNK = 512
IDX_WIDTH_STEP = 128
REL_BUCKETS = 32
REL_MAX_DIST = 128
MOE_GROUPS = 4
MOE_PER_GROUP = 8
MOE_EXPERTS = 32
MOE_FF = 256
EXPERT_TILE = 256
COMBINE_TILE = 256
GATHER_AHEAD = 2
DN_ALPHA = (2 * DEPTH) ** 0.25
LN_EPS = 1e-5
RMS_EPS = 1e-6

LANES = 128
MASKED = -1e30
VMEM_LIMIT = 56 * 1024 * 1024


def _params(*sem):
    return pltpu.CompilerParams(dimension_semantics=sem, vmem_limit_bytes=VMEM_LIMIT)


def _layer_norm_rows(y, g, b):
    mean = jnp.mean(y, axis=-1, keepdims=True)
    yc = y - mean
    var = jnp.mean(yc * yc, axis=-1, keepdims=True)
    return yc * lax.rsqrt(var + LN_EPS) * g + b


def _mm_plain_kernel(x_ref, w_ref, o_ref, *, tn):
    xb = x_ref[...].astype(BF16)
    for n0 in range(0, o_ref.shape[1], tn):
        acc = jnp.dot(xb, w_ref[:, n0:n0 + tn], preferred_element_type=F32)
        o_ref[:, n0:n0 + tn] = acc.astype(o_ref.dtype)


def _mm_glu_kernel(z_ref, w_ref, o_ref, *, tn):
    zb = z_ref[...]
    for n0 in range(0, o_ref.shape[1], tn):
        acc = jnp.dot(zb, w_ref[:, n0:n0 + tn], preferred_element_type=F32)
        zc = z_ref[:, n0:n0 + tn].astype(F32)
        o_ref[:, n0:n0 + tn] = (zc * jax.nn.sigmoid(acc)).astype(o_ref.dtype)


PACK_ROWS = D_MODEL // (2 * LANES)


def _store_packed(o_ref, y):
    half = y.shape[1] // 2
    bits = lambda v: lax.bitcast_convert_type(v.astype(BF16).astype(F32), jnp.uint32)
    word = (bits(y[:, half:]) & jnp.uint32(0xFFFF0000)) | (bits(y[:, :half]) >> 16)
    rows = y.shape[0]
    for c in range(PACK_ROWS):
        o_ref[pl.ds(c, rows, stride=PACK_ROWS), :] = word[:, c * LANES:(c + 1) * LANES]


def _load_packed(x_ref, rows):
    word = jnp.concatenate([x_ref[pl.ds(c, rows, stride=PACK_ROWS), :] for c in range(PACK_ROWS)], axis=1)
    lo = lax.bitcast_convert_type(word << 16, F32)
    hi = lax.bitcast_convert_type(word & jnp.uint32(0xFFFF0000), F32)
    return jnp.concatenate([lo, hi], axis=1)


def _mm_ln_kernel(x_ref, w_ref, res_ref, g_ref, b_ref, o_ref, op_ref, *, tn):
    xb = x_ref[...].astype(BF16)
    for n0 in range(0, o_ref.shape[1], tn):
        acc = jnp.dot(xb, w_ref[:, n0:n0 + tn], preferred_element_type=F32)
        o_ref[:, n0:n0 + tn] = DN_ALPHA * res_ref[:, n0:n0 + tn] + acc
    y = _layer_norm_rows(o_ref[...], g_ref[...], b_ref[...])
    o_ref[...] = y
    _store_packed(op_ref, y)


def _layer_spec(w_all, layer):
    shape = w_all.shape[1:]
    return pl.BlockSpec((None,) + shape, lambda *_: (layer,) + (0,) * len(shape))


def _linear(x, w_all, layer, *, tm, out_dtype, tn=512):
    m, k = x.shape
    n = w_all.shape[-1]
    return pl.pallas_call(
        functools.partial(_mm_plain_kernel, tn=tn),
        out_shape=jax.ShapeDtypeStruct((m, n), out_dtype),
        grid=(m // tm,),
        in_specs=[pl.BlockSpec((tm, k), lambda i: (i, 0)), _layer_spec(w_all, layer)],
        out_specs=pl.BlockSpec((tm, n), lambda i: (i, 0)),
        compiler_params=_params("parallel"),
        name="linear",
    )(x, w_all)


def _linear_glu(z, w_all, layer, *, tm, tn=512):
    m, k = z.shape
    return pl.pallas_call(
        functools.partial(_mm_glu_kernel, tn=tn),
        out_shape=jax.ShapeDtypeStruct((m, k), BF16),
        grid=(m // tm,),
        in_specs=[pl.BlockSpec((tm, k), lambda i: (i, 0)), _layer_spec(w_all, layer)],
        out_specs=pl.BlockSpec((tm, k), lambda i: (i, 0)),
        compiler_params=_params("parallel"),
        name="linear_glu",
    )(z, w_all)


def _linear_ln(x, w_all, layer, res, g, b, *, tm, tn=512):
    m, k = x.shape
    n = w_all.shape[-1]
    return pl.pallas_call(
        functools.partial(_mm_ln_kernel, tn=tn),
        out_shape=(jax.ShapeDtypeStruct((m, n), F32), jax.ShapeDtypeStruct((m * PACK_ROWS, LANES), jnp.uint32)),
        grid=(m // tm,),
        in_specs=[pl.BlockSpec((tm, k), lambda i: (i, 0)),
                  _layer_spec(w_all, layer),
                  pl.BlockSpec((tm, n), lambda i: (i, 0)),
                  pl.BlockSpec((1, n), lambda i: (0, 0)),
                  pl.BlockSpec((1, n), lambda i: (0, 0))],
        out_specs=(pl.BlockSpec((tm, n), lambda i: (i, 0)),
                   pl.BlockSpec((tm * PACK_ROWS, LANES), lambda i: (i, 0))),
        compiler_params=_params("parallel"),
        name="linear_ln",
    )(x, w_all, res, g.reshape(1, n), b.reshape(1, n))


def _gelu_tanh(x):
    return 0.5 * x * (1.0 + jnp.tanh(math.sqrt(2.0 / math.pi) * (x + 0.044715 * x * x * x)))


def _s5_kernel(u_ref, ar_ref, ai_ref, ldt_ref, br_ref, bi_ref, cr_ref, ci_ref, d_ref, o_ref, a_scr,
               *, n_chunks):
    cs, p, n = S5_CHUNK, S5_GROUP, S5_STATE
    width = cs * p
    n2 = 2 * n

    @pl.when(pl.program_id(0) == 0)
    def _():
        a_scr[...] = jnp.zeros(a_scr.shape, a_scr.dtype)

    dt = jnp.exp(ldt_ref[0])
    lam_r = jnp.minimum(ar_ref[0], -1e-4)
    lam_i = ai_ref[0]
    lo_half = lax.broadcasted_iota(jnp.int32, (1, n2), 1) < n

    def a_pow(tau):
        mag = jnp.exp(lam_r * dt * tau)
        ph = lam_i * dt * tau
        return mag * jnp.cos(ph), mag * jnp.sin(ph)

    tau = lax.broadcasted_iota(jnp.int32, (cs, 1), 0).astype(F32)
    a0r, a0i = a_pow(tau)
    a1r, a1i = a_pow(tau + 1.0)
    avr, avi = a_pow((cs - 1.0) - tau)

    abr, abi = a1r[0:1], a1i[0:1]
    den = lam_r * lam_r + lam_i * lam_i
    kr = ((abr - 1.0) * lam_r + abi * lam_i) / den
    ki = (abi * lam_r - (abr - 1.0) * lam_i) / den
    b_r, b_i = br_ref[0], bi_ref[0]
    bbr = kr * b_r - ki * b_i
    bbi = kr * b_i + ki * b_r
    c_r, c_i = cr_ref[0], ci_ref[0]

    def rep_rows(x):
        return jnp.concatenate([jnp.broadcast_to(x[t:t + 1, :], (p, n2)) for t in range(cs)], axis=0)

    def tile_rows(x):
        return jnp.concatenate([x] * cs, axis=0)

    def sel(lo, hi):
        return jnp.where(lo_half, lo, hi)

    crr, cii = tile_rows(c_r), tile_rows(c_i)
    brr, bii = tile_rows(bbr), tile_rows(bbi)
    c_taps = crr * rep_rows(sel(a0r, -a0i)) - cii * rep_rows(sel(a0i, a0r))
    c_out = crr * rep_rows(sel(a1r, -a1i)) - cii * rep_rows(sel(a1i, a1r))
    b_in = brr * rep_rows(sel(avr, avi)) + bii * rep_rows(sel(-avi, avr))
    b_in_sw = brr * rep_rows(sel(avi, avr)) + bii * rep_rows(sel(avr, -avi))

    b_mat = sel(bbr, bbi)
    taps = lax.dot_general(b_mat, c_taps, (((1,), (1,)), ((), ())),
                           precision=lax.Precision.HIGHEST, preferred_element_type=F32)

    lane = lax.broadcasted_iota(jnp.int32, (p, width), 1)
    band = [taps]
    for j in range(1, LANES // p):
        band.append(jnp.where(lane >= p * j, pltpu.roll(taps, p * j, axis=1), 0.0))
    band = jnp.concatenate(band, axis=0)
    n_blk = width // LANES
    band_t = jnp.concatenate([band[:, d * LANES:(d + 1) * LANES].T for d in reversed(range(n_blk))],
                             axis=1).astype(BF16)
    for q in range(n_blk):
        a_scr[q * LANES:(q + 1) * LANES, 0:(q + 1) * LANES] = band_t[:, (n_blk - 1 - q) * LANES:width]
    a_scr[width:width + n2, :] = b_in.T.astype(BF16)
    a_scr[width + n2:width + 2 * n2, :] = b_in_sw.T.astype(BF16)

    u = u_ref[...].reshape(width, u_ref.shape[2])
    res = jnp.dot(a_scr[...], u, preferred_element_type=F32)
    y = res[0:width]
    x_in = res[width:width + n2]
    x_sw = res[width + n2:width + 2 * n2]

    n_lanes = x_in.shape[1]
    levels = n_chunks.bit_length() - 1
    mults = []
    for k in range(levels):
        mr, mi = a_pow(jnp.full((1, 1), float(cs * 2 ** k), F32))
        mults += [mr, sel(-mi, mi), sel(mi, -mi)]
    mults = jnp.concatenate(mults + [jnp.zeros((n2 - len(mults), n2), F32)], axis=0)
    mcol = mults.T
    chunk_id = lax.rem(lax.broadcasted_iota(jnp.int32, (n2, n_lanes), 1), n_chunks)
    for k in range(levels):
        sh = 2 ** k
        ok = chunk_id >= sh
        xs = jnp.where(ok, pltpu.roll(x_in, sh, axis=1), 0.0)
        xs_sw = jnp.where(ok, pltpu.roll(x_sw, sh, axis=1), 0.0)
        m_rr, m_ni, m_pi = (mcol[:, 3 * k + j:3 * k + j + 1] for j in range(3))
        x_in, x_sw = (x_in + xs * m_rr + xs_sw * m_ni, x_sw + xs_sw * m_rr + xs * m_pi)
    h_start = jnp.where(chunk_id >= 1, pltpu.roll(x_in, 1, axis=1), 0.0).astype(BF16)

    y = y + jnp.dot(c_out.astype(BF16), h_start, preferred_element_type=F32)
    y3 = y.reshape(cs, p, n_lanes) + d_ref[0] * u_ref[...].astype(F32)
    o_ref[...] = _gelu_tanh(y3).astype(o_ref.dtype)


def _s5_core(u_t, a_re, a_im, log_dt, b_re, b_im, c_re, c_im, d, *, n_chunks):
    cs, dm, n_lanes = u_t.shape
    g = dm // S5_GROUP
    assert n_chunks & (n_chunks - 1) == 0
    dup = lambda x: jnp.concatenate([x, x], axis=-1)
    a_re2 = dup(a_re).reshape(g, 1, 2 * S5_STATE)
    a_im2 = dup(a_im).reshape(g, 1, 2 * S5_STATE)
    ldt = log_dt.reshape(g, 1, 1)
    bt_re2 = dup(jnp.swapaxes(b_re, 1, 2))
    bt_im2 = dup(jnp.swapaxes(b_im, 1, 2))
    c_re2 = dup(c_re)
    c_im2 = dup(c_im)
    d_c = d.reshape(g, S5_GROUP, 1)
    width = cs * S5_GROUP
    vec = lambda w: pl.BlockSpec((1, 1, w), lambda i: (i, 0, 0))
    mat = pl.BlockSpec((1, S5_GROUP, 2 * S5_STATE), lambda i: (i, 0, 0))
    blk = pl.BlockSpec((cs, S5_GROUP, n_lanes), lambda i: (0, i, 0))
    return pl.pallas_call(
        functools.partial(_s5_kernel, n_chunks=n_chunks),
        out_shape=jax.ShapeDtypeStruct((cs, dm, n_lanes), BF16),
        grid=(g,),
        in_specs=[blk, vec(2 * S5_STATE), vec(2 * S5_STATE), vec(1), mat, mat, mat, mat,
                  pl.BlockSpec((1, S5_GROUP, 1), lambda i: (i, 0, 0))],
        out_specs=blk,
        scratch_shapes=[pltpu.VMEM((width + 4 * S5_STATE, width), BF16)],
        compiler_params=_params("arbitrary"),
        name="s5_core",
    )(u_t, a_re2, a_im2, ldt, bt_re2, bt_im2, c_re2, c_im2, d_c)


S5_STEP = 8
LANE_ROWS = 128


def _s5_in_kernel(x_ref, w_ref, o_ref, u_scr):
    rows = LANE_ROWS * S5_STEP
    dm = x_ref.shape[2]
    xb = x_ref[...].reshape(rows, dm).astype(BF16)
    for n0 in range(0, dm, 512):
        acc = jnp.dot(xb, w_ref[:, n0:n0 + 512], preferred_element_type=F32)
        for c in range(512 // LANES):
            u_scr[n0 // LANES + c] = acc[:, c * LANES:(c + 1) * LANES]
    for j in range(S5_STEP):
        for cb in range(dm // LANES):
            tile = u_scr[cb, pl.ds(j, LANE_ROWS, stride=S5_STEP), :]
            o_ref[j, cb * LANES:(cb + 1) * LANES, :] = tile.T.astype(o_ref.dtype)


def _s5_in_proj(h3, w_all, layer):
    n_bc, cs, dm = h3.shape
    return pl.pallas_call(
        _s5_in_kernel,
        out_shape=jax.ShapeDtypeStruct((cs, dm, n_bc), BF16),
        grid=(cs // S5_STEP, n_bc // LANE_ROWS),
        in_specs=[pl.BlockSpec((LANE_ROWS, S5_STEP, dm), lambda s, r: (r, s, 0)), _layer_spec(w_all, layer)],
        out_specs=pl.BlockSpec((S5_STEP, dm, LANE_ROWS), lambda s, r: (s, 0, r)),
        scratch_shapes=[pltpu.VMEM((dm // LANES, LANE_ROWS * S5_STEP, LANES), F32)],
        compiler_params=_params("parallel", "parallel"),
        name="s5_in_proj",
    )(h3, w_all)


def _s5_glu_kernel(z_ref, w_ref, o_ref, z_scr, o_scr):
    nh = pl.program_id(2)
    half = o_ref.shape[2]

    @pl.when(nh == 0)
    def _():
        for j in range(S5_STEP):
            z_scr[j * LANE_ROWS:(j + 1) * LANE_ROWS, :] = z_ref[j].astype(F32).T

    zb = z_scr[...].astype(BF16)
    col0 = pl.multiple_of(nh * half, half)
    for n0 in range(0, half, 512):
        acc = jnp.dot(zb, w_ref[:, n0:n0 + 512], preferred_element_type=F32)
        zg = z_scr[:, pl.ds(col0 + n0, 512)] * jax.nn.sigmoid(acc)
        for j in range(S5_STEP):
            for c in range(512 // LANES):
                o_scr[n0 // LANES + c, pl.ds(j, LANE_ROWS, stride=S5_STEP), :] = (
                    zg[j * LANE_ROWS:(j + 1) * LANE_ROWS, c * LANES:(c + 1) * LANES])
    out = jnp.concatenate([o_scr[cb] for cb in range(half // LANES)], axis=1)
    o_ref[...] = out.reshape(o_ref.shape)


def _s5_glu(z_t, w_all, layer):
    cs, dm, n_bc = z_t.shape
    rows = LANE_ROWS * S5_STEP
    half = dm // 2
    return pl.pallas_call(
        _s5_glu_kernel,
        out_shape=jax.ShapeDtypeStruct((n_bc, cs, dm), F32),
        grid=(cs // S5_STEP, n_bc // LANE_ROWS, 2),
        in_specs=[pl.BlockSpec((S5_STEP, dm, LANE_ROWS), lambda s, r, nh: (s, 0, r)),
                  pl.BlockSpec((None, dm, half), lambda s, r, nh: (layer, 0, nh))],
        out_specs=pl.BlockSpec((LANE_ROWS, S5_STEP, half), lambda s, r, nh: (r, s, nh)),
        scratch_shapes=[pltpu.VMEM((rows, dm), F32), pltpu.VMEM((half // LANES, rows, LANES), F32)],
        compiler_params=_params("parallel", "parallel", "arbitrary"),
        name="s5_glu",
    )(z_t, w_all)


def _s5_mixer(h, res_g, res_b, layer, w_in, a_re, a_im, log_dt, b_re, b_im, c_re, c_im, d, w_glu, w_out,
              *, batch, seq):
    t = batch * seq
    nc = seq // S5_CHUNK
    u_t = _s5_in_proj(h.reshape(batch * nc, S5_CHUNK, D_MODEL), w_in, layer)
    z_t = _s5_core(u_t, a_re, a_im, log_dt, b_re, b_im, c_re, c_im, d, n_chunks=nc)
    zg = _s5_glu(z_t, w_glu, layer).reshape(t, D_MODEL)
    return _linear_ln(zg, w_out, layer, h, res_g, res_b, tm=256)


def _dsa_proj_kernel(x_ref, w_ref, qn_ref, kn_ref, cq_ref, ckv_ref, ckvt_ref, kidx_ref, widxt_ref):
    xb = x_ref[...].astype(BF16)

    def rms(v, g):
        return v * lax.rsqrt(jnp.mean(v * v, axis=-1, keepdims=True) + RMS_EPS) * g

    o1 = Q_LORA + KV_LORA
    cq = jnp.dot(xb, w_ref[:, 0:Q_LORA], preferred_element_type=F32)
    cq_ref[...] = rms(cq, qn_ref[...]).astype(BF16)
    ckv = rms(jnp.dot(xb, w_ref[:, Q_LORA:o1], preferred_element_type=F32), kn_ref[...])
    ckv_ref[...] = ckv.astype(BF16)
    ckvt_ref[...] = ckv.T.astype(BF16)
    kidx_ref[...] = jnp.dot(xb, w_ref[:, o1:o1 + IDX_DIM], preferred_element_type=F32).astype(BF16)
    widx = jnp.dot(xb, w_ref[:, o1 + IDX_DIM:o1 + IDX_DIM + LANES], preferred_element_type=F32)
    widxt_ref[...] = (widx * (IDX_HEADS ** -0.5)).T[0:IDX_HEADS, :]


def _dsa_proj(h, w_in_pad, layer, q_norm, kv_norm, *, batch, seq, tm=512):
    t = h.shape[0]
    per_b = seq // tm
    row = lambda w: pl.BlockSpec((tm, w), lambda i: (i, 0))
    return pl.pallas_call(
        _dsa_proj_kernel,
        out_shape=(jax.ShapeDtypeStruct((t, Q_LORA), BF16), jax.ShapeDtypeStruct((t, KV_LORA), BF16),
                   jax.ShapeDtypeStruct((batch, KV_LORA, seq), BF16),
                   jax.ShapeDtypeStruct((t, IDX_DIM), BF16), jax.ShapeDtypeStruct((IDX_HEADS, t), F32)),
        grid=(t // tm,),
        in_specs=[row(D_MODEL), _layer_spec(w_in_pad, layer),
                  pl.BlockSpec((1, Q_LORA), lambda i: (0, 0)), pl.BlockSpec((1, KV_LORA), lambda i: (0, 0))],
        out_specs=(row(Q_LORA), row(KV_LORA),
                   pl.BlockSpec((None, KV_LORA, tm), lambda i: (i // per_b, 0, i % per_b)),
                   row(IDX_DIM), pl.BlockSpec((IDX_HEADS, tm), lambda i: (0, i))),
        compiler_params=_params("parallel"),
        name="dsa_proj",
    )(h, w_in_pad, q_norm.reshape(1, Q_LORA), kv_norm.reshape(1, KV_LORA))


def _reduce_rows(op, x):
    rows, lanes = x.shape
    part = op(x.reshape(rows // 64, 8, 8, lanes), axis=0)
    return op(op(part, axis=0), axis=0, keepdims=True)


def _indexer_kernel(q_ref, k_ref, w_ref, s_ref, thr_ref, key_scr, *, seq):
    jb = pl.program_id(1)
    n_sel = float(min(IDX_TOPK, seq // 4))
    int_min = jnp.int32(-2 ** 31)
    flip = jnp.int32(0x7FFFFFFF)

    def run(width):
        k = k_ref[0:width, :]
        wrow = w_ref[...] * (IDX_DIM ** -0.5)
        acc = jnp.zeros((width, Q_BLOCK), F32)
        for g in range(IDX_HEADS // 2):
            qg = jnp.concatenate([q_ref[:, h * IDX_DIM:(h + 1) * IDX_DIM] for h in (2 * g, 2 * g + 1)], axis=0)
            s = lax.dot_general(k, qg, (((1,), (1,)), ((), ())), preferred_element_type=F32)
            acc = (acc + jnp.maximum(s[:, 0:Q_BLOCK], 0.0) * wrow[2 * g:2 * g + 1, :]
                   + jnp.maximum(s[:, Q_BLOCK:2 * Q_BLOCK], 0.0) * wrow[2 * g + 1:2 * g + 2, :])
        k_pos = lax.broadcasted_iota(jnp.int32, (width, Q_BLOCK), 0)
        q_pos = jb * Q_BLOCK + lax.broadcasted_iota(jnp.int32, (width, Q_BLOCK), 1)
        acc = jnp.where(k_pos <= q_pos, acc, -jnp.inf)
        s_ref[0:width, :] = acc
        if width < seq:
            s_ref[width:seq, :] = jnp.full((seq - width, Q_BLOCK), -jnp.inf, F32)

        bits = lax.bitcast_convert_type(acc, jnp.int32)
        key_scr[0:width, :] = jnp.where(bits < 0, bits ^ flip, bits)

        def count_ge(cand):
            return _reduce_rows(jnp.sum, (key_scr[0:width, :] >= cand).astype(F32))

        lo = jnp.where(count_ge(jnp.zeros((1, Q_BLOCK), jnp.int32)) >= n_sel, jnp.int32(0), int_min)

        def body(i, lo):
            cand = lo + jnp.left_shift(jnp.int32(1), jnp.int32(30) - i)
            return jnp.where(count_ge(cand) >= n_sel, cand, lo)

        lo = lax.fori_loop(0, 31, body, lo)
        thr = lax.bitcast_convert_type(jnp.where(lo < 0, lo ^ flip, lo), F32)
        thr_ref[...] = jnp.broadcast_to(thr, thr_ref.shape)

    cls_of_block = jb // (IDX_WIDTH_STEP // Q_BLOCK)
    min_cls = -(-int(n_sel) // IDX_WIDTH_STEP) - 1
    for cls in range(min_cls, seq // IDX_WIDTH_STEP):
        here = (cls_of_block <= cls) if cls == min_cls else (cls_of_block == cls)
        pl.when(here)(functools.partial(run, IDX_WIDTH_STEP * (cls + 1)))


def _indexer(qcat, kidx, widx_t, *, batch, seq):
    n_blk = seq // Q_BLOCK
    return pl.pallas_call(
        functools.partial(_indexer_kernel, seq=seq),
        out_shape=(jax.ShapeDtypeStruct((batch, n_blk, seq, Q_BLOCK), F32),
                   jax.ShapeDtypeStruct((batch, n_blk, 8, Q_BLOCK), F32)),
        grid=(batch, n_blk),
        in_specs=[pl.BlockSpec((Q_BLOCK, IDX_HEADS * IDX_DIM), lambda b, j: (b * n_blk + j, 1)),
                  pl.BlockSpec((seq, IDX_DIM), lambda b, j: (b, 0)),
                  pl.BlockSpec((IDX_HEADS, Q_BLOCK), lambda b, j: (0, b * n_blk + j))],
        out_specs=(pl.BlockSpec((None, None, seq, Q_BLOCK), lambda b, j: (b, j, 0, 0)),
                   pl.BlockSpec((None, None, 8, Q_BLOCK), lambda b, j: (b, j, 0, 0))),
        scratch_shapes=[pltpu.VMEM((seq, Q_BLOCK), jnp.int32)],
        compiler_params=_params("parallel", "parallel"),
        name="indexer",
    )(qcat, kidx, widx_t)


def _attn_kernel(jb_tbl, kc_tbl, q_ref, kv_ref, kvt_ref, s_ref, thr_ref, wuk_ref, wuvt_ref, bd_ref, bp_ref,
                 o_ref, ql_scr, m_scr, l_scr, acc_scr):
    jb = jb_tbl[pl.program_id(1)]
    kc = kc_tbl[pl.program_id(1)]
    last = (jb * Q_BLOCK) // KEY_CHUNK
    hd, qb = ATT_HEAD_DIM, Q_BLOCK
    sub = KEY_CHUNK // qb
    pair = 2 * qb

    @pl.when(kc == 0)
    def _():
        for h in range(ATT_HEADS):
            ql = jnp.dot(q_ref[:, h * hd:(h + 1) * hd], wuk_ref[h], preferred_element_type=F32)
            ql_scr[h * qb:(h + 1) * qb, :] = (ql * (hd ** -0.5)).astype(BF16)
        m_scr[...] = jnp.full(m_scr.shape, MASKED, F32)
        l_scr[...] = jnp.zeros(l_scr.shape, F32)
        acc_scr[...] = jnp.zeros(acc_scr.shape, F32)

    def attend(biases):
        width = len(biases) * qb
        kv = kv_ref[0:width, :]
        kvt = kvt_ref[:, 0:width]
        k_pos = kc * KEY_CHUNK + lax.broadcasted_iota(jnp.int32, (width, qb), 0)
        q_pos = jb * qb + lax.broadcasted_iota(jnp.int32, (width, qb), 1)
        keep = (s_ref[0:width, :] >= thr_ref[0:1, :]) & (k_pos <= q_pos)
        mask_add = jnp.where(keep, 0.0, MASKED)
        mask_add = jnp.concatenate([mask_add, mask_add], axis=1)
        for g in range(ATT_HEADS // 2):
            s = lax.dot_general(kv, ql_scr[g * pair:(g + 1) * pair, :], (((1,), (1,)), ((), ())),
                                preferred_element_type=F32)
            lg = s + mask_add
            if any(b is not None for b in biases):
                parts = []
                for c, b in enumerate(biases):
                    part = lg[c * qb:(c + 1) * qb]
                    if b is not None:
                        part = part + jnp.concatenate([b[2 * g], b[2 * g + 1]], axis=1)
                    parts.append(part)
                lg = jnp.concatenate(parts, axis=0)
            m_prev = m_scr[g, 0:1, :]
            m_new = jnp.maximum(m_prev, _reduce_rows(jnp.max, lg))
            corr = jnp.exp(m_prev - m_new)
            pr = jnp.exp(lg - m_new)
            l_scr[g] = jnp.broadcast_to(corr * l_scr[g, 0:1, :] + _reduce_rows(jnp.sum, pr), (8, pair))
            m_scr[g] = jnp.broadcast_to(m_new, (8, pair))
            pv = jnp.dot(kvt, pr.astype(BF16), preferred_element_type=F32)
            acc_scr[g] = acc_scr[g] * corr + pv

    r = jb - last * sub
    for nb in range(1, sub + 1):
        biases = (None,) * (nb - 2) + ((bp_ref,) if nb >= 2 else ()) + (bd_ref,)
        pl.when((kc == last) & (r == nb - 1))(functools.partial(attend, biases))
    prev_only = (kc == last - 1) & (r == 0)
    pl.when(prev_only)(functools.partial(attend, (None,) * (sub - 1) + (bp_ref,)))
    pl.when((kc < last) & jnp.logical_not(prev_only))(functools.partial(attend, (None,) * sub))

    @pl.when(kc == last)
    def _():
        for h in range(ATT_HEADS):
            g, half = h // 2, (h % 2) * qb
            ol = (acc_scr[g, :, half:half + qb] / l_scr[g, 0:1, half:half + qb]).astype(BF16)
            oh = jnp.dot(wuvt_ref[h], ol, preferred_element_type=F32)
            o_ref[:, h * hd:(h + 1) * hd] = oh.T.astype(o_ref.dtype)


def _rel_bucket_table():
    n = np.arange(2 * Q_BLOCK)
    max_exact = REL_BUCKETS // 2
    nf = np.maximum(n, 1).astype(np.float32)
    large = max_exact + (np.log(nf / np.float32(max_exact)) / np.float32(math.log(REL_MAX_DIST / max_exact))
                         * np.float32(REL_BUCKETS - max_exact)).astype(np.int32)
    large = np.minimum(large, REL_BUCKETS - 1)
    return np.where(n < max_exact, n, large)


def _attention(qcat, ckv, ckv_t, scores, thr, w_uk, w_uv_t, layer, rel_bias, *, batch, seq):
    t = batch * seq
    n_blk = seq // Q_BLOCK
    n_kc = seq // KEY_CHUNK
    bucket = _rel_bucket_table()
    assert bucket[Q_BLOCK + 1:].min() == REL_BUCKETS - 1
    kk = np.arange(Q_BLOCK)[:, None]
    qq = np.arange(Q_BLOCK)[None, :]
    rb = rel_bias.astype(F32)
    rb = rb - rb[REL_BUCKETS - 1]
    lookup = lambda dist: jnp.einsum("bh,kqb->hkq", rb, np.eye(REL_BUCKETS, dtype=np.float32)[bucket[dist]],
                                     precision=lax.Precision.HIGHEST)
    diag = jnp.where(jnp.asarray(kk <= qq)[None], lookup(np.maximum(qq - kk, 0)), 0.0)
    prev = lookup(Q_BLOCK + qq - kk)
    n_pair = ATT_HEADS // 2
    pair = 2 * Q_BLOCK

    steps = [(jj, kc) for jj in range(n_blk) for kc in range((jj * Q_BLOCK) // KEY_CHUNK + 1)]
    jb_tbl = jnp.asarray([s[0] for s in steps], jnp.int32)
    kc_tbl = jnp.asarray([s[1] for s in steps], jnp.int32)

    qrow = lambda b, s, jt, kt: (b * n_blk + jt[s], 0)
    full3 = lambda b, s, jt, kt: (0, 0, 0)
    return pl.pallas_call(
        _attn_kernel,
        out_shape=jax.ShapeDtypeStruct((t, ATT_HEADS * ATT_HEAD_DIM), BF16),
        grid_spec=pltpu.PrefetchScalarGridSpec(
            num_scalar_prefetch=2,
            grid=(batch, len(steps)),
            in_specs=[pl.BlockSpec((Q_BLOCK, ATT_HEADS * ATT_HEAD_DIM), qrow),
                      pl.BlockSpec((KEY_CHUNK, KV_LORA), lambda b, s, jt, kt: (b * n_kc + kt[s], 0)),
                      pl.BlockSpec((None, KV_LORA, KEY_CHUNK), lambda b, s, jt, kt: (b, 0, kt[s])),
                      pl.BlockSpec((None, None, KEY_CHUNK, Q_BLOCK), lambda b, s, jt, kt: (b, jt[s], kt[s], 0)),
                      pl.BlockSpec((None, None, 8, Q_BLOCK), lambda b, s, jt, kt: (b, jt[s], 0, 0)),
                      _layer_spec(w_uk, layer),
                      _layer_spec(w_uv_t, layer),
                      pl.BlockSpec((ATT_HEADS, Q_BLOCK, Q_BLOCK), full3),
                      pl.BlockSpec((ATT_HEADS, Q_BLOCK, Q_BLOCK), full3)],
            out_specs=pl.BlockSpec((Q_BLOCK, ATT_HEADS * ATT_HEAD_DIM), qrow),
            scratch_shapes=[pltpu.VMEM((ATT_HEADS * Q_BLOCK, KV_LORA), BF16),
                            pltpu.VMEM((n_pair, 8, pair), F32),
                            pltpu.VMEM((n_pair, 8, pair), F32),
                            pltpu.VMEM((n_pair, KV_LORA, pair), F32)]),
        compiler_params=_params("parallel", "arbitrary"),
        name="latent_attention",
    )(jb_tbl, kc_tbl, qcat, ckv, ckv_t, scores, thr, w_uk, w_uv_t, diag, prev)


def _dsa_mixer(h, res_g, res_b, layer, rel_bias, w_in_pad, q_norm, kv_norm, w_q, w_uk, w_uv_t, w_out,
               *, batch, seq):
    cq, ckv, ckv_t, kidx, widx_t = _dsa_proj(h, w_in_pad, layer, q_norm, kv_norm, batch=batch, seq=seq)
    qcat = _linear(cq, w_q, layer, tm=512, out_dtype=BF16)
    scores, thr = _indexer(qcat, kidx, widx_t, batch=batch, seq=seq)
    o = _attention(qcat, ckv, ckv_t, scores, thr, w_uk, w_uv_t, layer, rel_bias, batch=batch, seq=seq)
    return _linear_ln(o, w_out, layer, h, res_g, res_b, tm=256)


def _router_kernel(x_ref, w_ref, b_ref, route_ref, cnt_ref, cnt_scr):
    x = x_ref[...]
    x_hi = x.astype(BF16)
    x_lo = (x - x_hi.astype(F32)).astype(BF16)
    logits = (jnp.dot(x_hi, w_ref[0], preferred_element_type=F32)
              + jnp.dot(x_lo, w_ref[0], preferred_element_type=F32)
              + jnp.dot(x_hi, w_ref[1], preferred_element_type=F32)) + b_ref[...]
    lane = lax.broadcasted_iota(jnp.int32, logits.shape, 1)
    big = jnp.int32(LANES)
    is_group = (lane >= MOE_EXPERTS) & (lane < MOE_EXPERTS + MOE_GROUPS)
    gl = jnp.where(is_group, logits, -jnp.inf)
    gm = jnp.max(gl, axis=1, keepdims=True)
    g_p = 1.0 / jnp.sum(jnp.exp(gl - gm), axis=1, keepdims=True)
    g_idx = jnp.min(jnp.where(gl == gm, lane, big), axis=1, keepdims=True) - MOE_EXPERTS
    in_group = (lane < MOE_EXPERTS) & (jnp.right_shift(lane, 3) == g_idx)
    el = jnp.where(in_group, logits, -jnp.inf)
    em = jnp.max(el, axis=1, keepdims=True)
    ee = jnp.exp(el - em)
    prob = ee / jnp.sum(ee, axis=1, keepdims=True)
    p1 = jnp.max(prob, axis=1, keepdims=True)
    i1 = jnp.min(jnp.where(in_group & (prob == p1), lane, big), axis=1, keepdims=True)
    rest = in_group & (lane != i1)
    p2 = jnp.max(jnp.where(rest, prob, -1.0), axis=1, keepdims=True)
    i2 = jnp.min(jnp.where(rest & (prob == p2), lane, big), axis=1, keepdims=True)
    den = p1 + p2
    g1 = g_p * (p1 / den)
    g2 = g_p * (p2 / den)

    @pl.when(pl.program_id(0) == 0)
    def _():
        cnt_scr[...] = jnp.zeros(cnt_scr.shape, F32)

    tm = logits.shape[0]
    oh1 = (lane == i1).astype(BF16)
    oh2 = (lane == i2).astype(BF16)
    r_i = lax.broadcasted_iota(jnp.int32, (tm, tm), 0)
    c_i = lax.broadcasted_iota(jnp.int32, (tm, tm), 1)
    tri = (c_i < r_i).astype(BF16)
    pre1 = jnp.dot(tri, oh1, preferred_element_type=F32)
    pre2 = jnp.dot(tri, oh2, preferred_element_type=F32)
    tot1 = jnp.sum(oh1.astype(F32), axis=0, keepdims=True)
    tot2 = jnp.sum(oh2.astype(F32), axis=0, keepdims=True)
    base = cnt_scr[...]
    rank1 = jnp.sum(jnp.where(lane == i1, base + pre1, 0.0), axis=1, keepdims=True)
    rank2 = jnp.sum(jnp.where(lane == i2, base + tot1 + pre2, 0.0), axis=1, keepdims=True)
    cnt_scr[...] = base + tot1 + tot2
    cnt_ref[...] = jnp.broadcast_to(cnt_scr[...], cnt_ref.shape)

    out = jnp.where(lane == 0, i1.astype(F32), jnp.where(lane == 1, i2.astype(F32), 0.0))
    out = jnp.where(lane == 2, g1, jnp.where(lane == 3, g2, out))
    out = jnp.where(lane == 4, rank1, jnp.where(lane == 5, rank2, out))
    route_ref[...] = out


def _router(h, w_r, b_r, *, tm=512):
    t = h.shape[0]
    return pl.pallas_call(
        _router_kernel,
        out_shape=(jax.ShapeDtypeStruct((t, LANES), F32), jax.ShapeDtypeStruct((8, LANES), F32)),
        grid=(t // tm,),
        in_specs=[pl.BlockSpec((tm, D_MODEL), lambda i: (i, 0)),
                  pl.BlockSpec((2, D_MODEL, LANES), lambda i: (0, 0, 0)),
                  pl.BlockSpec((1, LANES), lambda i: (0, 0))],
        out_specs=(pl.BlockSpec((tm, LANES), lambda i: (i, 0)),
                   pl.BlockSpec((8, LANES), lambda i: (0, 0))),
        scratch_shapes=[pltpu.VMEM((1, LANES), F32)],
        compiler_params=_params("arbitrary"),
        name="moe_router",
    )(h, w_r, b_r)


def _moe_expert_kernel(te_ref, nu_ref, rt_ref, x_hbm, wg_ref, wu_ref, wd_ref, y_ref,
                       xbuf, sem, wgb, wub, wdb):
    i = pl.program_id(0)
    n_used = nu_ref[0]
    nbuf = GATHER_AHEAD + 1
    slot = lax.rem(i, nbuf)
    tme = EXPERT_TILE
    pr = PACK_ROWS

    def start_gather(tile, buf):
        def body(r2, carry):
            for j in range(2):
                r = r2 * 2 + j
                src = pl.multiple_of(rt_ref[tile * tme + r] * pr, pr)
                dst = pl.multiple_of(r * pr, pr)
                pltpu.make_async_copy(x_hbm.at[pl.ds(src, pr)], xbuf.at[buf, pl.ds(dst, pr)],
                                      sem.at[buf]).start(priority=j)
            return carry
        lax.fori_loop(0, tme // 2, body, 0, unroll=4)

    for j in range(GATHER_AHEAD):
        pl.when((i == 0) & (j < n_used))(functools.partial(start_gather, j, j))

    @pl.when(i + GATHER_AHEAD < n_used)
    def _():
        start_gather(i + GATHER_AHEAD, lax.rem(i + GATHER_AHEAD, nbuf))

    te = te_ref[i]
    prev = te_ref[jnp.maximum(i - 1, 0)]

    @pl.when((i == 0) | (te != prev))
    def _():
        wgb[...] = wg_ref[...].astype(BF16)
        wub[...] = wu_ref[...].astype(BF16)
        wdb[...] = wd_ref[...].astype(BF16)

    @pl.when(i < n_used)
    def _():
        pltpu.make_async_copy(x_hbm.at[pl.ds(0, tme * pr)], xbuf.at[slot], sem.at[slot]).wait()
        xb = _load_packed(xbuf.at[slot], tme).astype(BF16)
        gp = jnp.dot(xb, wgb[...], preferred_element_type=F32)
        up = jnp.dot(xb, wub[...], preferred_element_type=F32)
        hidden = (gp * jax.nn.sigmoid(gp)) * up
        _store_packed(y_ref, jnp.dot(hidden.astype(BF16), wdb[...], preferred_element_type=F32))

    @pl.when(i >= n_used)
    def _():
        y_ref[...] = jnp.zeros(y_ref.shape, y_ref.dtype)


def _moe_experts(tile_expert, n_used, row_token, h_packed, w_gate, w_up, w_down, layer):
    n_tiles = tile_expert.shape[0]
    rows = EXPERT_TILE * PACK_ROWS
    wspec = lambda a, b: pl.BlockSpec((None, None, a, b), lambda i, te, nu, rt: (layer, te[i], 0, 0))
    return pl.pallas_call(
        _moe_expert_kernel,
        out_shape=jax.ShapeDtypeStruct((n_tiles * rows, LANES), jnp.uint32),
        grid_spec=pltpu.PrefetchScalarGridSpec(
            num_scalar_prefetch=3,
            grid=(n_tiles,),
            in_specs=[pl.BlockSpec(memory_space=pl.ANY),
                      wspec(D_MODEL, MOE_FF), wspec(D_MODEL, MOE_FF), wspec(MOE_FF, D_MODEL)],
            out_specs=pl.BlockSpec((rows, LANES), lambda i, te, nu, rt: (i, 0)),
            scratch_shapes=[pltpu.VMEM((GATHER_AHEAD + 1, rows, LANES), jnp.uint32),
                            pltpu.SemaphoreType.DMA((GATHER_AHEAD + 1,)),
                            pltpu.VMEM((D_MODEL, MOE_FF), BF16),
                            pltpu.VMEM((D_MODEL, MOE_FF), BF16),
                            pltpu.VMEM((MOE_FF, D_MODEL), BF16)]),
        compiler_params=_params("arbitrary"),
        name="moe_experts",
    )(tile_expert, n_used, row_token, h_packed, w_gate, w_up, w_down)


def _moe_combine_kernel(slot_ref, h_ref, route_ref, y_hbm, g_ref, b_ref, o_ref, ybuf, sem):
    i = pl.program_id(0)
    n = pl.num_programs(0)
    nbuf = GATHER_AHEAD + 1
    slot = lax.rem(i, nbuf)
    tm = h_ref.shape[0]
    pr = PACK_ROWS

    def start_gather(tile, buf):
        def body(r, carry):
            for k in range(2):
                src = pl.multiple_of(slot_ref[(tile * tm + r) * 2 + k] * pr, pr)
                dst = pl.multiple_of((k * tm + r) * pr, pr)
                pltpu.make_async_copy(y_hbm.at[pl.ds(src, pr)], ybuf.at[buf, pl.ds(dst, pr)],
                                      sem.at[buf]).start(priority=k)
            return carry
        lax.fori_loop(0, tm, body, 0, unroll=4)

    for j in range(GATHER_AHEAD):
        pl.when((i == 0) & (j < n))(functools.partial(start_gather, j, j))

    @pl.when(i + GATHER_AHEAD < n)
    def _():
        start_gather(i + GATHER_AHEAD, lax.rem(i + GATHER_AHEAD, nbuf))

    pltpu.make_async_copy(y_hbm.at[pl.ds(0, 2 * tm * pr)], ybuf.at[slot], sem.at[slot]).wait()
    y1 = _load_packed(ybuf.at[slot, pl.ds(0, tm * pr)], tm)
    y2 = _load_packed(ybuf.at[slot, pl.ds(tm * pr, tm * pr)], tm)
    ffn = route_ref[:, 2:3] * y1 + route_ref[:, 3:4] * y2
    o_ref[...] = _layer_norm_rows(DN_ALPHA * h_ref[...] + ffn, g_ref[...], b_ref[...])


def _moe_combine(slots, h, route, y, ln_g, ln_b, *, tm):
    t = h.shape[0]
    row = lambda w: pl.BlockSpec((tm, w), lambda i, s: (i, 0))
    vec = pl.BlockSpec((1, D_MODEL), lambda i, s: (0, 0))
    return pl.pallas_call(
        _moe_combine_kernel,
        out_shape=jax.ShapeDtypeStruct((t, D_MODEL), F32),
        grid_spec=pltpu.PrefetchScalarGridSpec(
            num_scalar_prefetch=1,
            grid=(t // tm,),
            in_specs=[row(D_MODEL), row(LANES), pl.BlockSpec(memory_space=pl.ANY), vec, vec],
            out_specs=row(D_MODEL),
            scratch_shapes=[pltpu.VMEM((GATHER_AHEAD + 1, 2 * tm * PACK_ROWS, LANES), jnp.uint32),
                            pltpu.SemaphoreType.DMA((GATHER_AHEAD + 1,))]),
        compiler_params=_params("arbitrary"),
        name="moe_combine",
    )(slots, h, route, y, ln_g.reshape(1, D_MODEL), ln_b.reshape(1, D_MODEL))


def _moe_plan(route, counts, n_tiles):
    t = route.shape[0]
    tme = EXPERT_TILE
    cnt = counts[0, :MOE_EXPERTS].astype(jnp.int32)
    tiles = (cnt + tme - 1) // tme
    tile_end = jnp.cumsum(tiles)
    tile_start = tile_end - tiles
    n_used = tile_end[-1]
    choice = route[:, 0:2].astype(jnp.int32)
    rank = route[:, 4:6].astype(jnp.int32)
    expert_ids = jnp.arange(MOE_EXPERTS, dtype=jnp.int32)
    start = jnp.sum(jnp.where(choice[..., None] == expert_ids, tile_start, 0), axis=-1)
    slots = (start * tme + rank).reshape(-1)
    tile_ids = jnp.arange(n_tiles, dtype=jnp.int32)
    te = jnp.sum((tile_ids[:, None] >= tile_end[None, :]).astype(jnp.int32), axis=1)
    last_e = jnp.sum((n_used - 1 >= tile_end).astype(jnp.int32))
    te = jnp.where(tile_ids < n_used, te, last_e)
    row_token = jnp.zeros((n_tiles * tme,), jnp.int32).at[slots].set(jnp.arange(2 * t, dtype=jnp.int32) // 2)
    return te, n_used.reshape(1).astype(jnp.int32), row_token, slots.astype(jnp.int32)


def _hier_moe(h, h_packed, layer, w_group, b_group, w_expert, b_expert, w_gate, w_up, w_down, ln_g, ln_b):
    t = h.shape[0]
    pad = LANES - MOE_EXPERTS - MOE_GROUPS
    w_r = jnp.concatenate([w_expert, w_group, jnp.zeros((D_MODEL, pad), F32)], axis=1)
    w_hi = w_r.astype(BF16)
    w_r = jnp.stack([w_hi, (w_r - w_hi.astype(F32)).astype(BF16)])
    b_r = jnp.concatenate([b_expert, b_group, jnp.zeros((pad,), F32)]).reshape(1, LANES)
    route, counts = _router(h, w_r, b_r)
    n_tiles = (2 * t) // EXPERT_TILE + MOE_EXPERTS
    te, n_used, row_token, slots = _moe_plan(route, counts, n_tiles)
    y = _moe_experts(te, n_used, row_token, h_packed, w_gate, w_up, w_down, layer)
    return _moe_combine(slots, h, route, y, ln_g, ln_b, tm=COMBINE_TILE)


def kernel(x, rel_bias, s5_w_in, s5_a_re, s5_a_im, s5_log_dt, s5_b_re, s5_b_im, s5_c_re, s5_c_im, s5_d, s5_w_glu, s5_w_out, dsa_w_in, dsa_q_norm, dsa_kv_norm, dsa_w_uq, dsa_w_qidx, dsa_w_uk, dsa_w_uv, dsa_w_out, moe_w_group, moe_b_group, moe_w_expert, moe_b_expert, moe_w_gate, moe_w_up, moe_w_down, ln_mix_g, ln_mix_b, ln_ffn_g, ln_ffn_b):
    batch, seq, dm = x.shape
    h = x.reshape(batch * seq, dm)
    s5_w_in, s5_w_glu, s5_w_out = (w.astype(BF16) for w in (s5_w_in, s5_w_glu, s5_w_out))
    n_pad = Q_LORA + KV_LORA + IDX_DIM + LANES - dsa_w_in.shape[-1]
    dsa_w_in_pad = jnp.pad(dsa_w_in, ((0, 0), (0, 0), (0, n_pad))).astype(BF16)
    dsa_w_q = jnp.concatenate([dsa_w_uq, dsa_w_qidx], axis=-1).astype(BF16)
    dsa_w_uk, dsa_w_out = dsa_w_uk.astype(BF16), dsa_w_out.astype(BF16)
    dsa_w_uv_t = jnp.swapaxes(dsa_w_uv, -1, -2).astype(BF16)
    for i in range(DEPTH):
        j = i // 2
        if i % 2 == 0:
            h, hp = _s5_mixer(h, ln_mix_g[i], ln_mix_b[i], j, s5_w_in, s5_a_re[j], s5_a_im[j], s5_log_dt[j],
                              s5_b_re[j], s5_b_im[j], s5_c_re[j], s5_c_im[j], s5_d[j], s5_w_glu, s5_w_out,
                              batch=batch, seq=seq)
        else:
            h, hp = _dsa_mixer(h, ln_mix_g[i], ln_mix_b[i], j, rel_bias, dsa_w_in_pad, dsa_q_norm[j],
                               dsa_kv_norm[j], dsa_w_q, dsa_w_uk, dsa_w_uv_t, dsa_w_out, batch=batch, seq=seq)
        h = _hier_moe(h, hp, i, moe_w_group[i], moe_b_group[i], moe_w_expert[i], moe_b_expert[i],
                      moe_w_gate, moe_w_up, moe_w_down, ln_ffn_g[i], ln_ffn_b[i])
    return h.reshape(batch, seq, dm)
```
